```python
import jax, jax.numpy as jnp
from jax import lax
import numpy as np

D_MODEL = 4096
BATCH = 2
SEQ = 4096
DEPTH = 2

GRID_W = 64
CTX_LEN = 256
HEAD_DIM = 128
POOL_WINDOWS = (2, 4, 8, 16)
POOL_GROUPS = 4
POOL_W = D_MODEL // 4
POOL_GROUP_DIM = POOL_W // POOL_GROUPS
NA_W = 3 * D_MODEL // 8
NA_HEADS = NA_W // HEAD_DIM
NA_ROWS = 8
NA_COLS = 16
HG_W = D_MODEL - POOL_W - NA_W
HG_HEADS = HG_W // HEAD_DIM
HG_CHUNK = 64
FORGET_EPS = 1e-20
MIX_W = POOL_W + NA_W + HG_W
IN_W = POOL_W + 3 * NA_W + 5 * HG_W
N_EXPERTS = 32
TOP_K = 4
D_EXPERT = D_MODEL // 8
SWIGLU_LIMIT = 7.0
SWIGLU_ALPHA = 1.702
ROPE_BASE = 10000.0
LN_EPS = 1e-6
DEEPNORM_ALPHA = (2 * DEPTH) ** 0.25
DEEPNORM_BETA = (8 * DEPTH) ** -0.25

kernel_name = 'hybrid_pool_na_hgrn2_moe_dit'


def layer_norm(x, g=None, b=None):
    xf = x.astype(jnp.float32)
    mu = jnp.mean(xf, -1, keepdims=True)
    var = jnp.mean(jnp.square(xf - mu), -1, keepdims=True)
    y = (xf - mu) * lax.rsqrt(var + LN_EPS)
    if g is not None:
        y = y * g.astype(jnp.float32) + b.astype(jnp.float32)
    return y.astype(x.dtype)


def adaln(cond, w, b):
    return jnp.split(jax.nn.silu(cond) @ w + b, 6, axis=-1)


def modulate(y, shift, scale):
    return y * (1 + scale) + shift


def heads(t):
    return t.reshape(t.shape[:-1] + (-1, HEAD_DIM))


def flip(t):
    return jnp.flip(t, axis=1)


def split_proj(p):
    widths = (POOL_W, NA_W, NA_W, NA_W, HG_W, HG_W, HG_W, HG_W, HG_W)
    offs = np.cumsum(widths)[:-1].tolist()
    return jnp.split(p, offs, axis=-1)


def multiscale_pool(u, pool_w, pool_scale):
    B, L, _ = u.shape
    uf = u.astype(jnp.float32)
    cs = jnp.pad(jnp.cumsum(uf, axis=1), ((0, 0), (1, 0), (0, 0)))
    t = jnp.arange(L)
    diffs = []
    for gi, w in enumerate(POOL_WINDOWS):
        lo = jnp.clip(t - w // 2, 0, L)
        hi = jnp.clip(t + (w - w // 2), 0, L)
        sl = slice(gi * POOL_GROUP_DIM, (gi + 1) * POOL_GROUP_DIM)
        csg = cs[..., sl]
        mean = (csg[:, hi] - csg[:, lo]) / (hi - lo).astype(jnp.float32)[None, :, None]
        diffs.append(mean - uf[..., sl])
    d = jnp.stack(diffs, axis=2)
    y = jnp.einsum('blgi,gio->blgo', d, pool_w.astype(jnp.float32)).reshape(B, L, POOL_W)
    return (y * pool_scale.astype(jnp.float32)).astype(u.dtype)


def axial_rope(t):
    L = t.shape[1]
    pos = jnp.arange(L)
    half = HEAD_DIM // 2
    inv = ROPE_BASE ** (-jnp.arange(0, half, 2, dtype=jnp.float32) / half)

    def rot(u, p):
        ang = p.astype(jnp.float32)[:, None] * inv[None]
        cos = jnp.cos(ang)[None, :, None].astype(u.dtype)
        sin = jnp.sin(ang)[None, :, None].astype(u.dtype)
        u1, u2 = jnp.split(u, 2, axis=-1)
        return jnp.concatenate([u1 * cos - u2 * sin, u1 * sin + u2 * cos], -1)

    return jnp.concatenate([rot(t[..., :half], pos // GRID_W), rot(t[..., half:], pos % GRID_W)], -1)


def neighborhood_attention(q, k, v, kc, vc, rpb):
    B, L, H, d = q.shape
    rows = L // GRID_W
    kr = min(NA_ROWS, rows)
    scale = HEAD_DIM ** -0.5
    qg = q.reshape(B, rows, GRID_W, H, d)
    kg = k.reshape(B, rows, GRID_W, H, d)
    vg = v.reshape(B, rows, GRID_W, H, d)
    cols = jnp.arange(GRID_W)
    c_idx = jnp.clip(cols - NA_COLS // 2, 0, GRID_W - NA_COLS)[:, None] + jnp.arange(NA_COLS)[None]
    c_off = c_idx - cols[:, None] + (NA_COLS - 1)
    rpb_cols = rpb[:, :, c_off]

    def row_block(r):
        rs = jnp.clip(r - kr // 2, 0, rows - kr)
        kb = lax.dynamic_slice_in_dim(kg, rs, kr, axis=1)[:, :, c_idx]
        vb = lax.dynamic_slice_in_dim(vg, rs, kr, axis=1)[:, :, c_idx]
        qr = lax.dynamic_index_in_dim(qg, r, axis=1, keepdims=False)
        r_off = rs + jnp.arange(kr) - r + (NA_ROWS - 1)
        bias = jnp.transpose(rpb_cols[:, r_off], (0, 2, 1, 3)).astype(jnp.float32)
        s_loc = jnp.einsum('bqhd,bkqjhd->bhqkj', qr, kb).astype(jnp.float32) * scale + bias
        s_ctx = jnp.einsum('bqhd,bchd->bhqc', qr, kc).astype(jnp.float32) * scale
        s = jnp.concatenate([s_loc.reshape(B, H, GRID_W, kr * NA_COLS), s_ctx], -1)
        p = jax.nn.softmax(s, axis=-1).astype(v.dtype)
        p_loc = p[..., :kr * NA_COLS].reshape(B, H, GRID_W, kr, NA_COLS)
        p_ctx = p[..., kr * NA_COLS:]
        return (jnp.einsum('bhqkj,bkqjhd->bqhd', p_loc, vb)
                + jnp.einsum('bhqc,bchd->bqhd', p_ctx, vc))

    out = lax.map(row_block, jnp.arange(rows))
    return jnp.moveaxis(out, 0, 1).reshape(B, L, H * d)


def context_attention(q, k, v):
    B, Lc, H, d = q.shape
    s = jnp.einsum('bqhd,bkhd->bhqk', q, k).astype(jnp.float32) * (HEAD_DIM ** -0.5)
    p = jax.nn.softmax(s, axis=-1).astype(v.dtype)
    return jnp.einsum('bhqk,bkhd->bqhd', p, v).reshape(B, Lc, H * d)


def hgrn_keys(f_raw, lb):
    z = heads(f_raw.astype(jnp.float32))
    lb = lb.reshape(HG_HEADS, HEAD_DIM)
    sig = jax.nn.sigmoid(z)
    f = lb + (1.0 - lb) * sig
    log_f = jnp.log(jnp.maximum(f, FORGET_EPS))
    k = (1.0 - lb) * (1.0 - sig)
    return k, log_f


def hgrn_scan(q, k, v, log_f, s0):
    B, L, H, dk = q.shape
    C = min(HG_CHUNK, L)
    n = L // C

    def chunks(t):
        return jnp.moveaxis(t.reshape((B, n, C) + t.shape[2:]), 1, 0)

    causal = jnp.tril(jnp.ones((C, C), dtype=bool))[None, :, :, None, None]

    def step(s, inp):
        qc, kc, vc, gc = inp
        b = jnp.cumsum(gc, axis=1)
        o_inter = jnp.einsum('bthk,bhkv->bthv', qc * jnp.exp(b), s)
        diff = jnp.where(causal, b[:, :, None] - b[:, None, :], 0.0)
        decay = jnp.where(causal, jnp.exp(diff), 0.0)
        a = jnp.einsum('bthk,bshk,btshk->bhts', qc, kc, decay)
        o_intra = jnp.einsum('bhts,bshv->bthv', a, vc)
        b_end = b[:, -1]
        s_new = (jnp.exp(b_end)[..., None] * s
                 + jnp.einsum('bshk,bshv->bhkv', kc * jnp.exp(b_end[:, None] - b), vc))
        return s_new, o_inter + o_intra

    s_fin, o = lax.scan(step, s0, (chunks(q), chunks(k), chunks(v), chunks(log_f)))
    return jnp.moveaxis(o, 0, 1).reshape(B, L, H, -1), s_fin


def hgrn_final_state(k, v, log_f):
    b = jnp.cumsum(log_f, axis=1)
    return jnp.einsum('bshk,bshv->bhkv', k * jnp.exp(b[:, -1:] - b), v)


def hgrn_readout(o, g, norm_g):
    o = o * lax.rsqrt(jnp.mean(o * o, -1, keepdims=True) + LN_EPS)
    o = o.reshape(o.shape[:2] + (HG_W,)) * norm_g.astype(jnp.float32)
    return (o * jax.nn.silu(g.astype(jnp.float32))).astype(g.dtype)


def token_mixers(px, pc, pool_w, pool_scale, rpb, lb_f, lb_b, norm_g, ctx_out):
    ux, qx, kx, vx, hqx, hfx, hbx, hix, hgx = split_proj(px)
    uc, qc, kc, vc, hqc, hfc, hbc, hic, hgc = split_proj(pc)
    B = px.shape[0]
    a_x = multiscale_pool(ux, pool_w, pool_scale)
    kc_h, vc_h = heads(kc), heads(vc)
    b_x = neighborhood_attention(axial_rope(heads(qx)), axial_rope(heads(kx)), heads(vx), kc_h, vc_h, rpb)
    v_x, v_c = heads(hix.astype(jnp.float32)), heads(hic.astype(jnp.float32))
    q_x = heads(jax.nn.silu(hqx.astype(jnp.float32)))
    k_xf, g_xf = hgrn_keys(hfx, lb_f)
    k_xb, g_xb = hgrn_keys(hbx, lb_b)
    k_cf, g_cf = hgrn_keys(hfc, lb_f)
    k_cb, g_cb = hgrn_keys(hbc, lb_b)
    if ctx_out:
        q_c = heads(jax.nn.silu(hqc.astype(jnp.float32)))
        s0 = jnp.zeros((B, HG_HEADS, HEAD_DIM, HEAD_DIM), jnp.float32)
        o_cf, s_cf = hgrn_scan(q_c, k_cf, v_c, g_cf, s0)
        o_cb, s_cb = hgrn_scan(flip(q_c), flip(k_cb), flip(v_c), flip(g_cb), s0)
        c_c = hgrn_readout(o_cf + flip(o_cb), hgc, norm_g)
    else:
        s_cf = hgrn_final_state(k_cf, v_c, g_cf)
        s_cb = hgrn_final_state(flip(k_cb), flip(v_c), flip(g_cb))
    o_xf, _ = hgrn_scan(q_x, k_xf, v_x, g_xf, s_cf)
    o_xb, _ = hgrn_scan(flip(q_x), flip(k_xb), flip(v_x), flip(g_xb), s_cb)
    c_x = hgrn_readout(o_xf + flip(o_xb), hgx, norm_g)
    mix_x = jnp.concatenate([a_x, b_x, c_x], -1)
    if not ctx_out:
        return mix_x, None
    a_c = multiscale_pool(uc, pool_w, pool_scale)
    b_c = context_attention(heads(qc), kc_h, vc_h)
    mix_c = jnp.concatenate([a_c, b_c, c_c], -1)
    return mix_x, mix_c


def moe_ffn(t, rw, rb, w1, b1, w2, b2):
    logits = (t @ rw + rb).astype(jnp.float32)
    top_v, top_i = lax.top_k(logits, TOP_K)
    wts = jax.nn.softmax(top_v, axis=-1)
    gates = jnp.einsum('tk,tke->te', wts, jax.nn.one_hot(top_i, N_EXPERTS, dtype=jnp.float32)).astype(t.dtype)
    out = jnp.zeros_like(t)
    for e in range(N_EXPERTS):
        gate, up = jnp.split(t @ w1[e] + b1[e], 2, axis=-1)
        gate = jnp.minimum(gate, SWIGLU_LIMIT)
        up = jnp.clip(up, -SWIGLU_LIMIT, SWIGLU_LIMIT)
        hdn = (up + 1) * (gate * jax.nn.sigmoid(SWIGLU_ALPHA * gate))
        out = out + gates[:, e:e + 1] * (hdn @ w2[e] + b2[e])
    return out


def setup_inputs(seed: int = 0) -> dict:
    key = jax.random.key(seed)
    ks = jax.random.split(key, 24)
    f32 = jnp.float32

    def nrm(k, shape, scale):
        return jax.random.normal(k, shape, f32) * scale

    return {
        'x': nrm(ks[0], (BATCH, SEQ, D_MODEL), 1.0),
        'c': nrm(ks[1], (BATCH, D_MODEL), 1.0),
        'ctx': nrm(ks[2], (BATCH, CTX_LEN, D_MODEL), 1.0),
        'c_ctx': nrm(ks[3], (D_MODEL,), 1.0),
        'w_mod': nrm(ks[4], (DEPTH, D_MODEL, 6 * D_MODEL), 0.5 * D_MODEL ** -0.5),
        'b_mod': nrm(ks[5], (DEPTH, 6 * D_MODEL), 0.02),
        'w_in': nrm(ks[6], (DEPTH, D_MODEL, IN_W), D_MODEL ** -0.5),
        'pool_w': nrm(ks[7], (DEPTH, POOL_GROUPS, POOL_GROUP_DIM, POOL_GROUP_DIM), POOL_GROUP_DIM ** -0.5),
        'pool_scale': 1.0 + nrm(ks[8], (DEPTH, POOL_W), 0.1),
        'na_rpb': nrm(ks[9], (DEPTH, NA_HEADS, 2 * NA_ROWS - 1, 2 * NA_COLS - 1), 0.1),
        'hg_lb': nrm(ks[10], (2, DEPTH, HG_W), 1.0),
        'hg_norm_g': 1.0 + nrm(ks[11], (DEPTH, HG_W), 0.1),
        'w_out': nrm(ks[12], (DEPTH, MIX_W, D_MODEL), MIX_W ** -0.5 * DEEPNORM_BETA),
        'ln1_g': 1.0 + nrm(ks[13], (DEPTH, D_MODEL), 0.1),
        'ln1_b': nrm(ks[14], (DEPTH, D_MODEL), 0.02),
        'ln2_g': 1.0 + nrm(ks[15], (DEPTH, D_MODEL), 0.1),
        'ln2_b': nrm(ks[16], (DEPTH, D_MODEL), 0.02),
        'router_w': nrm(ks[17], (DEPTH, D_MODEL, N_EXPERTS), D_MODEL ** -0.5),
        'router_b': nrm(ks[18], (DEPTH, N_EXPERTS), 0.01),
        'exp_w1': nrm(ks[19], (DEPTH, N_EXPERTS, D_MODEL, 2 * D_EXPERT), D_MODEL ** -0.5),
        'exp_b1': nrm(ks[20], (DEPTH, N_EXPERTS, 2 * D_EXPERT), 0.01),
        'exp_w2': nrm(ks[21], (DEPTH, N_EXPERTS, D_EXPERT, D_MODEL), D_EXPERT ** -0.5 * DEEPNORM_BETA),
        'exp_b2': nrm(ks[22], (DEPTH, N_EXPERTS, D_MODEL), 0.01),
    }


def reference(x, c, ctx, c_ctx, w_mod, b_mod, w_in, pool_w, pool_scale, na_rpb, hg_lb, hg_norm_g,
              w_out, ln1_g, ln1_b, ln2_g, ln2_b, router_w, router_b, exp_w1, exp_b1, exp_w2, exp_b2):
    B, L, D = x.shape
    Lc = ctx.shape[1]
    lb_soft = jax.nn.softmax(hg_lb.astype(jnp.float32), axis=1)
    lower_bounds = jnp.cumsum(lb_soft, axis=1) - lb_soft[:, :1]
    h, hc = x, ctx
    for l in range(DEPTH):
        ctx_out = l < DEPTH - 1
        sx1, ax1, gx1, sx2, ax2, gx2 = [m[:, None] for m in adaln(c, w_mod[l], b_mod[l])]
        sc1, ac1, gc1, sc2, ac2, gc2 = adaln(c_ctx, w_mod[l], b_mod[l])
        px = modulate(layer_norm(h), sx1, ax1) @ w_in[l]
        pc = modulate(layer_norm(hc), sc1, ac1) @ w_in[l]
        mix_x, mix_c = token_mixers(px, pc, pool_w[l], pool_scale[l], na_rpb[l],
                                    lower_bounds[0, l], lower_bounds[1, l], hg_norm_g[l], ctx_out)
        h = layer_norm(DEEPNORM_ALPHA * h + gx1 * (mix_x @ w_out[l]), ln1_g[l], ln1_b[l])
        fx = modulate(layer_norm(h), sx2, ax2).reshape(B * L, D)
        moe_args = (router_w[l], router_b[l], exp_w1[l], exp_b1[l], exp_w2[l], exp_b2[l])
        if ctx_out:
            hc = layer_norm(DEEPNORM_ALPHA * hc + gc1 * (mix_c @ w_out[l]), ln1_g[l], ln1_b[l])
            fc = modulate(layer_norm(hc), sc2, ac2).reshape(B * Lc, D)
            f_all = moe_ffn(jnp.concatenate([fx, fc], axis=0), *moe_args)
            f_x = f_all[:B * L].reshape(B, L, D)
            hc = layer_norm(DEEPNORM_ALPHA * hc + gc2 * f_all[B * L:].reshape(B, Lc, D), ln2_g[l], ln2_b[l])
        else:
            f_x = moe_ffn(fx, *moe_args).reshape(B, L, D)
        h = layer_norm(DEEPNORM_ALPHA * h + gx2 * f_x, ln2_g[l], ln2_b[l])
    return h
```

```python
import functools

import jax
import jax.numpy as jnp
import numpy as np
from jax import lax
from jax.experimental import pallas as pl
from jax.experimental.pallas import tpu as pltpu

F32 = jnp.float32
BF16 = jnp.bfloat16

HEAD_DIM = 128
GRID_W = 64
POOL_WINDOWS = (2, 4, 8, 16)
NA_ROWS = 8
NA_COLS = 16
FORGET_EPS = 1e-20
N_EXPERTS = 32
TOP_K = 4
SWIGLU_LIMIT = 7.0
SWIGLU_ALPHA = 1.702
ROPE_BASE = 10000.0
LN_EPS = 1e-6

VMEM_LIMIT_BYTES = 56 * 1024 * 1024
HG_CHUNK = 256
NEG_BIG = -1e30


def _params(sem):
    return pltpu.CompilerParams(dimension_semantics=sem, vmem_limit_bytes=VMEM_LIMIT_BYTES)


def _dot(a, b):
    return jnp.dot(a, b, preferred_element_type=F32)


def _dot_nt(a, b):
    return lax.dot_general(a, b, (((1,), (1,)), ((), ())), preferred_element_type=F32)


def _sigmoid(x):
    return 1.0 / (1.0 + jnp.exp(-x))


def _hgrn_direction(q, z, v_bf, lb, st_ref, level_of, tri_bf, reverse):
    C = q.shape[0]
    q = q * _sigmoid(q)
    sig = _sigmoid(z)
    f = lb + (1.0 - lb) * sig
    g = jnp.log(jnp.maximum(f, FORGET_EPS))
    k = (1.0 - lb) * (1.0 - sig)

    g_hi = g.astype(BF16)
    r1 = g - g_hi.astype(F32)
    g_mid = r1.astype(BF16)
    g_lo = (r1 - g_mid.astype(F32)).astype(BF16)
    b = _dot(tri_bf, g_hi) + _dot(tri_bf, g_mid) + _dot(tri_bf, g_lo)

    row = lax.broadcasted_iota(jnp.int32, (C, HEAD_DIM), 0)
    q_bf = q.astype(BF16)
    k_bf = k.astype(BF16)
    a = jnp.where(level_of < 0, _dot_nt(q_bf, k_bf), 0.0)

    h = 1
    lvl = 0
    while h < C:
        upper = (row & h) != 0
        t_role = jnp.logical_not(upper) if reverse else upper
        m0 = h if reverse else h - 1
        if h == 1:
            sh = C - 1 if reverse else 1
            nb = pltpu.roll(b, sh, 0)
            ref = jnp.where(t_role, nb, b)
        elif h == 2:
            r4 = row & 3
            if reverse:
                ref = jnp.where(r4 == 0, pltpu.roll(b, C - 2, 0),
                                jnp.where(r4 == 1, pltpu.roll(b, C - 1, 0),
                                          jnp.where(r4 == 2, b, pltpu.roll(b, 1, 0))))
            else:
                ref = jnp.where(r4 == 0, pltpu.roll(b, C - 1, 0),
                                jnp.where(r4 == 1, b,
                                          jnp.where(r4 == 2, pltpu.roll(b, 1, 0), pltpu.roll(b, 2, 0))))
        else:
            b3 = b.reshape(C // (2 * h), 2 * h, HEAD_DIM)
            ref = jnp.broadcast_to(b3[:, m0:m0 + 1, :], b3.shape).reshape(C, HEAD_DIM)
        e = jnp.exp(-jnp.abs(b - ref))
        x = jnp.where(t_role, q, k) * e
        qs = jnp.where(t_role, x, 0.0).astype(BF16)
        ks = jnp.where(t_role, 0.0, x).astype(BF16)
        a = jnp.where(level_of == lvl, _dot_nt(qs, ks), a)
        h *= 2
        lvl += 1

    o_intra = _dot(a.astype(BF16), v_bf)

    st = st_ref[...]
    tot = b[0:1, :] if reverse else b[C - 1:C, :]
    qdec = (q * jnp.exp(b)).astype(BF16)
    o_inter = _dot_nt(qdec, st.astype(BF16))
    kdec = (k * jnp.exp(tot - b)).astype(BF16)
    vt = v_bf.astype(F32).T.astype(BF16)
    st_ref[...] = jnp.exp(tot) * st + _dot(vt, kdec)
    return o_intra + o_inter


def _hgrn_kernel(qf_ref, zf_ref, vf_ref, qb_ref, zb_ref, vb_ref, lbf_ref, lbb_ref, lev_ref,
                 trif_ref, trib_ref, of_ref, ob_ref, sf_ref, sb_ref):
    @pl.when(pl.program_id(2) == 0)
    def _():
        sf_ref[...] = jnp.zeros_like(sf_ref)
        sb_ref[...] = jnp.zeros_like(sb_ref)

    lev = lev_ref[...]
    of_ref[...] = _hgrn_direction(qf_ref[...].astype(F32), zf_ref[...].astype(F32), vf_ref[...],
                                  lbf_ref[0], sf_ref, lev, trif_ref[...], False)
    ob_ref[...] = _hgrn_direction(qb_ref[...].astype(F32), zb_ref[...].astype(F32), vb_ref[...],
                                  lbb_ref[0], sb_ref, lev, trib_ref[...], True)


def _hgrn_call(px, lb_f, lb_b, B, L, Lc, col_q, col_ff, col_fb, col_i, n_heads):
    C = HG_CHUNK
    T = px.shape[0]
    nlc, ncc = L // C, Lc // C
    ns = nlc + ncc
    ctx0 = B * nlc

    def fwd_row(b, s):
        return jnp.where(s < ncc, ctx0 + b * ncc + s, b * nlc + s - ncc)

    def bwd_row(b, s):
        return jnp.where(s < ncc, ctx0 + b * ncc + (ncc - 1 - s), b * nlc + (nlc - 1 - (s - ncc)))

    def spec(row_fn, col):
        return pl.BlockSpec((C, HEAD_DIM), lambda b, h, s: (row_fn(b, s), col + h))

    idx = np.arange(C)
    xr = idx[:, None] ^ idx[None, :]
    level_of = np.where(xr == 0, -1, np.floor(np.log2(np.maximum(xr, 1)))).astype(np.int32)
    tri_f = (idx[:, None] >= idx[None, :]).astype(np.float32)
    tri_b = (idx[:, None] <= idx[None, :]).astype(np.float32)
    const = lambda shape: pl.BlockSpec(shape, lambda b, h, s: (0,) * len(shape))
    lb_spec = pl.BlockSpec((1, 1, HEAD_DIM), lambda b, h, s: (h, 0, 0))
    out_sds = jax.ShapeDtypeStruct((T, n_heads * HEAD_DIM), F32)
    return pl.pallas_call(
        _hgrn_kernel,
        grid=(B, n_heads, ns),
        in_specs=[spec(fwd_row, col_q), spec(fwd_row, col_ff), spec(fwd_row, col_i),
                  spec(bwd_row, col_q), spec(bwd_row, col_fb), spec(bwd_row, col_i),
                  lb_spec, lb_spec, const((C, C)), const((C, C)), const((C, C))],
        out_specs=[pl.BlockSpec((C, HEAD_DIM), lambda b, h, s: (fwd_row(b, s), h)),
                   pl.BlockSpec((C, HEAD_DIM), lambda b, h, s: (bwd_row(b, s), h))],
        out_shape=[out_sds, out_sds],
        scratch_shapes=[pltpu.VMEM((HEAD_DIM, HEAD_DIM), F32), pltpu.VMEM((HEAD_DIM, HEAD_DIM), F32)],
        compiler_params=_params(("parallel", "parallel", "arbitrary")),
        name="hgrn_scan",
    )(px, px, px, px, px, px, lb_f, lb_b, jnp.asarray(level_of), jnp.asarray(tri_f, BF16),
      jnp.asarray(tri_b, BF16))


NA_QROWS = 4
NA_KROWS = NA_QROWS + NA_ROWS


def _na_block_cases(rows):
    assert rows % NA_QROWS == 0 and rows >= 2 * NA_ROWS
    nb = rows // NA_QROWS
    cases = []
    for i in (0, 1, nb - 1):
        r0 = NA_QROWS * i
        ks = min(max(r0 - NA_ROWS // 2, 0), rows - NA_KROWS)
        per_q = []
        for qr in range(NA_QROWS):
            r = r0 + qr
            rs = min(max(r - NA_ROWS // 2, 0), rows - NA_ROWS)
            per_q.append((rs - ks, ks - r + NA_ROWS - 1))
        cases.append(per_q)
    return cases


def _na_kernel(rpb_ref, q_ref, k_ref, v_ref, qc_ref, kc_ref, vc_ref, cos_ref, sin_ref,
               ox_ref, oc_ref, qr_ref, kr_ref, tb_ref, bias_ref, *, rows):
    h = pl.program_id(1)
    L = q_ref.shape[0]
    W = GRID_W
    scale = HEAD_DIM ** -0.5
    n_dr = 2 * NA_ROWS - 1
    n_dc = 2 * NA_COLS - 1

    c_i = lax.broadcasted_iota(jnp.int32, (W, W), 0)
    kc_i = lax.broadcasted_iota(jnp.int32, (W, W), 1)
    cs = jnp.clip(c_i - NA_COLS // 2, 0, W - NA_COLS)
    in_win = (kc_i >= cs) & (kc_i < cs + NA_COLS)
    c_off = kc_i - c_i + (NA_COLS - 1)
    for dr in range(n_dr):
        acc = jnp.zeros((W, W), F32)
        for j in range(n_dc):
            acc = jnp.where(c_off == j, rpb_ref[h, dr, j], acc)
        tb_ref[dr] = jnp.where(in_win, acc, NEG_BIG)
    neg = jnp.full((W, W), NEG_BIG, F32)
    for ci, per_q in enumerate(_na_block_cases(rows)):
        for qr, (j0, dr0) in enumerate(per_q):
            for j in range(NA_KROWS):
                valid = j0 <= j < j0 + NA_ROWS
                blk = tb_ref[dr0 + j] if valid else neg
                bias_ref[ci, qr * W:(qr + 1) * W, j * W:(j + 1) * W] = blk

    RC = 256
    lane = lax.broadcasted_iota(jnp.int32, (RC, HEAD_DIM), 1)
    first = (lane & (HEAD_DIM // 2 - 1)) < HEAD_DIM // 4

    def rope_chunk(i, carry):
        sl = pl.ds(pl.multiple_of(i * RC, RC), RC)
        cos = cos_ref[sl, :]
        sin = sin_ref[sl, :]
        for src, dst, mul in ((q_ref, qr_ref, scale), (k_ref, kr_ref, 1.0)):
            t = src[sl, :].astype(F32)
            partner = jnp.where(first, pltpu.roll(t, HEAD_DIM - HEAD_DIM // 4, 1), pltpu.roll(t, HEAD_DIM // 4, 1))
            dst[sl, :] = ((t * cos + partner * sin) * mul).astype(BF16)
        return carry

    lax.fori_loop(0, L // RC, rope_chunk, 0)

    kc = kc_ref[...]
    vc = vc_ref[...]
    nb = rows // NA_QROWS
    QB = NA_QROWS * W
    KB = NA_KROWS * W

    def block(i, carry):
        ks = jnp.clip(i * NA_QROWS - NA_ROWS // 2, 0, rows - NA_KROWS)
        case = jnp.where(i == 0, 0, jnp.where(i == nb - 1, 2, 1))
        qs = pl.ds(pl.multiple_of(i * QB, QB), QB)
        kslice = pl.ds(pl.multiple_of(ks * W, W), KB)
        qb = qr_ref[qs, :]
        s_loc = _dot_nt(qb, kr_ref[kslice, :]) + bias_ref[case]
        s_ctx = _dot_nt(qb, kc)
        m = jnp.maximum(jnp.max(s_loc, -1, keepdims=True), jnp.max(s_ctx, -1, keepdims=True))
        p_loc = jnp.exp(s_loc - m)
        p_ctx = jnp.exp(s_ctx - m)
        den = jnp.sum(p_loc, -1, keepdims=True) + jnp.sum(p_ctx, -1, keepdims=True)
        o = _dot(p_loc.astype(BF16), v_ref[kslice, :]) + _dot(p_ctx.astype(BF16), vc)
        ox_ref[qs, :] = (o / den).astype(ox_ref.dtype)
        return carry

    lax.fori_loop(0, nb, block, 0)

    qc = (qc_ref[...].astype(F32) * scale).astype(BF16)
    s = _dot_nt(qc, kc)
    p = jnp.exp(s - jnp.max(s, -1, keepdims=True))
    oc = _dot(p.astype(BF16), vc) / jnp.sum(p, -1, keepdims=True)
    oc_ref[...] = oc.astype(oc_ref.dtype)


def _rope_tables(L):
    pos = jnp.arange(L)
    half = HEAD_DIM // 2
    inv = ROPE_BASE ** (-jnp.arange(0, half, 2, dtype=F32) / half)

    def tab(p):
        ang = p.astype(F32)[:, None] * inv[None]
        cos, sin = jnp.cos(ang), jnp.sin(ang)
        return jnp.concatenate([cos, cos], -1), jnp.concatenate([-sin, sin], -1)

    c_r, s_r = tab(pos // GRID_W)
    c_c, s_c = tab(pos % GRID_W)
    return jnp.concatenate([c_r, c_c], -1), jnp.concatenate([s_r, s_c], -1)


def _na_call(px, rpb, B, L, Lc, col_q, col_k, col_v, n_heads):
    rows = L // GRID_W
    cos, sin = _rope_tables(L)
    ctx0 = B * L // Lc

    def lat(col):
        return pl.BlockSpec((L, HEAD_DIM), lambda b, h: (b, col + h))

    def cx(col):
        return pl.BlockSpec((Lc, HEAD_DIM), lambda b, h: (ctx0 + b, col + h))

    tab = pl.BlockSpec((L, HEAD_DIM), lambda b, h: (0, 0))
    W = GRID_W
    return pl.pallas_call(
        functools.partial(_na_kernel, rows=rows),
        grid=(B, n_heads),
        in_specs=[pl.BlockSpec(memory_space=pltpu.SMEM), lat(col_q), lat(col_k), lat(col_v),
                  cx(col_q), cx(col_k), cx(col_v), tab, tab],
        out_specs=[pl.BlockSpec((L, HEAD_DIM), lambda b, h: (b, h)),
                   pl.BlockSpec((Lc, HEAD_DIM), lambda b, h: (b, h))],
        out_shape=[jax.ShapeDtypeStruct((B * L, n_heads * HEAD_DIM), BF16),
                   jax.ShapeDtypeStruct((B * Lc, n_heads * HEAD_DIM), BF16)],
        scratch_shapes=[pltpu.VMEM((L, HEAD_DIM), BF16), pltpu.VMEM((L, HEAD_DIM), BF16),
                        pltpu.VMEM((2 * NA_ROWS - 1, W, W), F32),
                        pltpu.VMEM((3, NA_QROWS * W, NA_KROWS * W), F32)],
        compiler_params=_params(("parallel", "parallel")),
        name="neighborhood_attention",
    )(rpb, px, px, px, px, px, px, cos, sin)


POOL_TILE = 256
POOL_HALO = 16


def _pool_kernel(prev_ref, cur_ref, next_ref, w_ref, scale_ref, o_ref, *, n_lat_tiles, nl, nc):
    i = pl.program_id(0)
    is_lat = i < n_lat_tiles
    j = jnp.where(is_lat, i % nl, (i - n_lat_tiles) % nc)
    n = jnp.where(is_lat, nl, nc)
    TQ, HL = POOL_TILE, POOL_HALO
    lo_min = jnp.where(j > 0, -HL, 0)
    hi_max = jnp.where(j < n - 1, TQ + HL, TQ)
    src = jnp.concatenate([prev_ref[...], cur_ref[...], next_ref[...]], axis=0)
    t = lax.broadcasted_iota(jnp.int32, (TQ, TQ + 2 * HL), 0)
    s = lax.broadcasted_iota(jnp.int32, (TQ, TQ + 2 * HL), 1) - HL
    Dg = w_ref.shape[1]
    t1 = lax.broadcasted_iota(jnp.int32, (TQ, Dg), 0)
    for g, w in enumerate(POOL_WINDOWS):
        lo = jnp.maximum(t - w // 2, lo_min)
        hi = jnp.minimum(t + (w - w // 2), hi_max)
        band = jnp.where(s >= lo, jnp.where(s < hi, 1.0, 0.0), 0.0).astype(BF16)
        cnt = (jnp.minimum(t1 + (w - w // 2), hi_max) - jnp.maximum(t1 - w // 2, lo_min)).astype(F32)
        cols = slice(g * Dg, (g + 1) * Dg)
        win = _dot(band, src[:, cols])
        d = win / cnt - cur_ref[:, cols].astype(F32)
        y = _dot(d.astype(BF16), w_ref[g]) * scale_ref[:, cols]
        o_ref[:, cols] = y.astype(o_ref.dtype)


def _pool_call(px, pool_w_bf, pool_scale, B, L, Lc):
    T = px.shape[0]
    G, Dg, _ = pool_w_bf.shape
    PW = G * Dg
    TQ, HL = POOL_TILE, POOL_HALO
    r = TQ // HL
    n_tiles = T // TQ
    n_halo = T // HL
    kern = functools.partial(_pool_kernel, n_lat_tiles=B * L // TQ, nl=L // TQ, nc=Lc // TQ)
    return pl.pallas_call(
        kern,
        grid=(n_tiles,),
        in_specs=[pl.BlockSpec((HL, PW), lambda i: (jnp.maximum(i * r - 1, 0), 0)),
                  pl.BlockSpec((TQ, PW), lambda i: (i, 0)),
                  pl.BlockSpec((HL, PW), lambda i: (jnp.minimum((i + 1) * r, n_halo - 1), 0)),
                  pl.BlockSpec((G, Dg, Dg), lambda i: (0, 0, 0)),
                  pl.BlockSpec((1, PW), lambda i: (0, 0))],
        out_specs=pl.BlockSpec((TQ, PW), lambda i: (i, 0)),
        out_shape=jax.ShapeDtypeStruct((T, PW), BF16),
        compiler_params=_params(("parallel",)),
        name="multiscale_pool",
    )(px, px, px, pool_w_bf, pool_scale.reshape(1, PW).astype(F32))


def _readout_kernel(of_ref, ob_ref, gate_ref, ng_ref, o_ref):
    o = of_ref[...] + ob_ref[...]
    o = o * lax.rsqrt(jnp.mean(o * o, -1, keepdims=True) + LN_EPS)
    g = gate_ref[...].astype(F32)
    o_ref[...] = (o * ng_ref[...] * (g * _sigmoid(g))).astype(o_ref.dtype)


def _readout_call(o_f, o_b, px, norm_g, col_gate, tm):
    T, HW = o_f.shape
    blk = lambda col: pl.BlockSpec((tm, HEAD_DIM), lambda i, h: (i, col + h))
    return pl.pallas_call(
        _readout_kernel,
        grid=(T // tm, HW // HEAD_DIM),
        in_specs=[blk(0), blk(0), blk(col_gate), pl.BlockSpec((1, HEAD_DIM), lambda i, h: (0, h))],
        out_specs=blk(0),
        out_shape=jax.ShapeDtypeStruct((T, HW), BF16),
        compiler_params=_params(("parallel", "parallel")),
        name="hgrn_readout",
    )(o_f, o_b, px, norm_g.reshape(1, HW).astype(F32))


COND_ROWS = 16


def _adaln_kernel(cond_ref, w_ref, b_ref, o_ref):
    cnd = cond_ref[...]
    a = (cnd * _sigmoid(cnd)).astype(BF16)
    o_ref[0] = _dot(a, w_ref[0].astype(BF16)) + b_ref[0]


def _adaln_call(cond, w_mod, b_mod, tn=1024):
    depth, D, N = w_mod.shape
    return pl.pallas_call(
        _adaln_kernel,
        grid=(depth, N // tn),
        in_specs=[pl.BlockSpec((COND_ROWS, D), lambda l, j: (0, 0)),
                  pl.BlockSpec((1, D, tn), lambda l, j: (l, 0, j)),
                  pl.BlockSpec((1, 1, tn), lambda l, j: (l, 0, j))],
        out_specs=pl.BlockSpec((1, COND_ROWS, tn), lambda l, j: (l, 0, j)),
        out_shape=jax.ShapeDtypeStruct((depth, COND_ROWS, N), F32),
        compiler_params=_params(("parallel", "parallel")),
        name="adaln",
    )(cond, w_mod, b_mod.reshape(depth, 1, N))


def _cond_row(i, tm, n_lat_rows, L, B):
    return jnp.where(i * tm < n_lat_rows, (i * tm) // L, B)


def _layer_norm_rows(x):
    mu = jnp.mean(x, -1, keepdims=True)
    xc = x - mu
    var = jnp.mean(xc * xc, -1, keepdims=True)
    return xc * lax.rsqrt(var + LN_EPS)


def _in_proj_kernel(x_ref, mod_ref, w_ref, o_ref, a_ref, *, tm, n_lat_rows, L, B, D):
    @pl.when(pl.program_id(1) == 0)
    def _():
        r = _cond_row(pl.program_id(0), tm, n_lat_rows, L, B)
        shift = mod_ref[pl.ds(r, 1), 0:D]
        scale = mod_ref[pl.ds(r, 1), D:2 * D]
        y = _layer_norm_rows(x_ref[...])
        a_ref[...] = (y * (1.0 + scale) + shift).astype(BF16)

    o_ref[...] = _dot(a_ref[...], w_ref[...]).astype(o_ref.dtype)


def _in_proj_call(h, mod_l, w_bf, B, L, tm=512, tn=1024):
    T, D = h.shape
    N = w_bf.shape[1]
    kern = functools.partial(_in_proj_kernel, tm=tm, n_lat_rows=B * L, L=L, B=B, D=D)
    return pl.pallas_call(
        kern,
        grid=(T // tm, N // tn),
        in_specs=[pl.BlockSpec((tm, D), lambda i, j: (i, 0)),
                  pl.BlockSpec(mod_l.shape, lambda i, j: (0, 0)),
                  pl.BlockSpec((D, tn), lambda i, j: (0, j))],
        out_specs=pl.BlockSpec((tm, tn), lambda i, j: (i, j)),
        out_shape=jax.ShapeDtypeStruct((T, N), BF16),
        scratch_shapes=[pltpu.VMEM((tm, D), BF16)],
        compiler_params=_params(("parallel", "arbitrary")),
        name="in_proj",
    )(h, mod_l, w_bf)


def _split_bf16(x):
    hi = x.astype(BF16)
    return hi, (x - hi.astype(F32)).astype(BF16)


def _out_proj_kernel(mix_ref, w_ref, h_ref, mod_ref, g_ref, b_ref, rw_ref, rb_ref,
                     h1_ref, fin_ref, gates_ref, topi_ref, acc_ref, *, tm, n_lat_rows, L, B, D, alpha):
    k = pl.program_id(1)

    @pl.when(k == 0)
    def _():
        acc_ref[...] = jnp.zeros_like(acc_ref)

    acc_ref[...] += _dot(mix_ref[...], w_ref[...])

    @pl.when(k == pl.num_programs(1) - 1)
    def _():
        r = _cond_row(pl.program_id(0), tm, n_lat_rows, L, B)
        mrow = lambda slot: mod_ref[pl.ds(r, 1), slot * D:(slot + 1) * D]
        u = alpha * h_ref[...] + mrow(2) * acc_ref[...]
        h1 = _layer_norm_rows(u) * g_ref[...] + b_ref[...]
        h1_ref[...] = h1
        fin = _layer_norm_rows(h1) * (1.0 + mrow(4)) + mrow(3)
        fin_ref[...] = fin.astype(BF16)
        f_hi, f_lo = _split_bf16(fin)
        rw = rw_ref[...]
        w_hi, w_lo = _split_bf16(rw)
        logits = _dot(f_hi, w_hi) + _dot(f_hi, w_lo) + _dot(f_lo, w_hi) + rb_ref[...]
        lane = lax.broadcasted_iota(jnp.int32, logits.shape, 1)
        ninf = jnp.float32(-jnp.inf)
        xs = jnp.where(lane < N_EXPERTS, logits, ninf)
        sel_v, sel_i = [], []
        for _ in range(TOP_K):
            m = jnp.max(xs, -1, keepdims=True)
            idx = jnp.min(jnp.where(xs == m, lane, logits.shape[1]), -1, keepdims=True)
            sel_v.append(m)
            sel_i.append(idx)
            xs = jnp.where(lane == idx, ninf, xs)
        ex = [jnp.exp(v - sel_v[0]) for v in sel_v]
        den = ex[0]
        for e in ex[1:]:
            den = den + e
        gates = jnp.zeros(logits.shape, F32)
        topi = jnp.zeros(logits.shape, jnp.int32)
        for j in range(TOP_K):
            gates = jnp.where(lane == sel_i[j], ex[j] / den, gates)
            topi = jnp.where(lane == j, sel_i[j], topi)
        gates_ref[...] = gates
        topi_ref[...] = topi


def _out_proj_call(mix, h, w_bf, mod_l, ln_g, ln_b, rw_pad, rb_pad, n_out, B, L, alpha, tm=256, tk=512):
    D = h.shape[1]
    K = mix.shape[1]
    NE = rw_pad.shape[1]
    kern = functools.partial(_out_proj_kernel, tm=tm, n_lat_rows=B * L, L=L, B=B, D=D, alpha=alpha)
    row = lambda i, k: (i, 0)
    fixed = lambda i, k: (0, 0)
    return pl.pallas_call(
        kern,
        grid=(n_out // tm, K // tk),
        in_specs=[pl.BlockSpec((tm, tk), lambda i, k: (i, k)),
                  pl.BlockSpec((tk, D), lambda i, k: (k, 0)),
                  pl.BlockSpec((tm, D), row),
                  pl.BlockSpec(mod_l.shape, fixed),
                  pl.BlockSpec((1, D), fixed), pl.BlockSpec((1, D), fixed),
                  pl.BlockSpec((D, NE), fixed), pl.BlockSpec((1, NE), fixed)],
        out_specs=[pl.BlockSpec((tm, D), row), pl.BlockSpec((tm, D), row),
                   pl.BlockSpec((tm, NE), row), pl.BlockSpec((tm, NE), row)],
        out_shape=[jax.ShapeDtypeStruct((n_out, D), F32), jax.ShapeDtypeStruct((n_out, D), BF16),
                   jax.ShapeDtypeStruct((n_out, NE), F32), jax.ShapeDtypeStruct((n_out, NE), jnp.int32)],
        scratch_shapes=[pltpu.VMEM((tm, D), F32)],
        compiler_params=_params(("parallel", "arbitrary")),
        name="out_proj_router",
    )(mix, w_bf, h, mod_l, ln_g.reshape(1, D), ln_b.reshape(1, D), rw_pad, rb_pad)


MOE_TILE = 256


def _moe_kernel(te_ref, tv_ref, x_ref, rw_ref, w1_ref, b1_ref, w2_ref, b2_ref, o_ref, *, DE):
    i = pl.program_id(0)

    @pl.when(tv_ref[i] > 0)
    def _():
        hcat = _dot(x_ref[...], w1_ref[0]) + b1_ref[0]
        gate = jnp.minimum(hcat[:, :DE], SWIGLU_LIMIT)
        up = jnp.clip(hcat[:, DE:], -SWIGLU_LIMIT, SWIGLU_LIMIT)
        hdn = (up + 1.0) * (gate * _sigmoid(SWIGLU_ALPHA * gate))
        y = _dot(hdn.astype(BF16), w2_ref[0]) + b2_ref[0]
        o_ref[...] = (y * rw_ref[...]).astype(o_ref.dtype)

    @pl.when(tv_ref[i] == 0)
    def _():
        o_ref[...] = jnp.zeros_like(o_ref)


def _moe_call(xs, row_w, tile_expert, tile_valid, w1_bf, b1, w2_bf, b2):
    NP, D = xs.shape
    E, _, DE2 = w1_bf.shape
    DE = DE2 // 2
    tm = MOE_TILE
    grid_spec = pltpu.PrefetchScalarGridSpec(
        num_scalar_prefetch=2,
        grid=(NP // tm,),
        in_specs=[pl.BlockSpec((tm, D), lambda i, te, tv: (i, 0)),
                  pl.BlockSpec((tm, 1), lambda i, te, tv: (i, 0)),
                  pl.BlockSpec((1, D, DE2), lambda i, te, tv: (te[i], 0, 0)),
                  pl.BlockSpec((1, 1, DE2), lambda i, te, tv: (te[i], 0, 0)),
                  pl.BlockSpec((1, DE, D), lambda i, te, tv: (te[i], 0, 0)),
                  pl.BlockSpec((1, 1, D), lambda i, te, tv: (te[i], 0, 0))],
        out_specs=pl.BlockSpec((tm, D), lambda i, te, tv: (i, 0)),
    )
    return pl.pallas_call(
        functools.partial(_moe_kernel, DE=DE),
        grid_spec=grid_spec,
        out_shape=jax.ShapeDtypeStruct((NP, D), BF16),
        compiler_params=_params(("arbitrary",)),
        name="moe_experts",
    )(tile_expert, tile_valid, xs, row_w, w1_bf, b1.reshape(E, 1, DE2), w2_bf, b2.reshape(E, 1, D))


def _moe_dispatch(topi, gates, tm):
    T = topi.shape[0]
    E = N_EXPERTS
    flat_e = topi.reshape(-1)
    flat_w = jnp.take_along_axis(gates, topi, axis=1).reshape(-1)
    n = T * TOP_K
    n_tiles = (n + E * (tm - 1)) // tm
    counts = jnp.zeros((E,), jnp.int32).at[flat_e].add(1)
    padded = ((counts + tm - 1) // tm) * tm
    starts = jnp.cumsum(padded) - padded
    order = jnp.argsort(flat_e, stable=True)
    sorted_e = flat_e[order]
    rank = jnp.arange(n, dtype=jnp.int32) - (jnp.cumsum(counts) - counts)[sorted_e]
    pos_sorted = starts[sorted_e] + rank
    row_token = jnp.zeros((n_tiles * tm,), jnp.int32).at[pos_sorted].set((order // TOP_K).astype(jnp.int32))
    row_w = jnp.zeros((n_tiles * tm,), F32).at[pos_sorted].set(flat_w[order])
    pos = jnp.zeros((n,), jnp.int32).at[order].set(pos_sorted).reshape(T, TOP_K)
    tile_start = jnp.arange(n_tiles, dtype=jnp.int32) * tm
    ends = starts + padded
    tile_expert = jnp.minimum(jnp.sum(tile_start[:, None] >= ends[None, :], axis=1), E - 1).astype(jnp.int32)
    tile_valid = (tile_start < ends[E - 1]).astype(jnp.int32)
    return row_token, row_w.reshape(-1, 1), pos, tile_expert, tile_valid


def _final_kernel(h_ref, f_ref, mod_ref, g_ref, b_ref, o_ref, *, tm, n_lat_rows, L, B, D, alpha):
    r = _cond_row(pl.program_id(0), tm, n_lat_rows, L, B)
    gate = mod_ref[pl.ds(r, 1), 5 * D:6 * D]
    u = alpha * h_ref[...] + gate * f_ref[...].astype(F32)
    o_ref[...] = _layer_norm_rows(u) * g_ref[...] + b_ref[...]


def _final_call(h1, f, mod_l, ln_g, ln_b, B, L, alpha, tm=256):
    n, D = h1.shape
    kern = functools.partial(_final_kernel, tm=tm, n_lat_rows=B * L, L=L, B=B, D=D, alpha=alpha)
    row = lambda i: (i, 0)
    fixed = lambda i: (0, 0)
    return pl.pallas_call(
        kern,
        grid=(n // tm,),
        in_specs=[pl.BlockSpec((tm, D), row), pl.BlockSpec((tm, D), row), pl.BlockSpec(mod_l.shape, fixed),
                  pl.BlockSpec((1, D), fixed), pl.BlockSpec((1, D), fixed)],
        out_specs=pl.BlockSpec((tm, D), row),
        out_shape=jax.ShapeDtypeStruct((n, D), F32),
        compiler_params=_params(("parallel",)),
        name="final_ln",
    )(h1, f, mod_l, ln_g.reshape(1, D), ln_b.reshape(1, D))


def kernel(x, c, ctx, c_ctx, w_mod, b_mod, w_in, pool_w, pool_scale, na_rpb, hg_lb, hg_norm_g, w_out, ln1_g, ln1_b, ln2_g, ln2_b, router_w, router_b, exp_w1, exp_b1, exp_w2, exp_b2):
    B, L, D = x.shape
    Lc = ctx.shape[1]
    depth = w_in.shape[0]
    alpha = (2 * depth) ** 0.25
    pool_width = pool_w.shape[1] * pool_w.shape[2]
    na_width = na_rpb.shape[1] * HEAD_DIM
    hg_width = hg_norm_g.shape[1]
    na_heads, hg_heads = na_width // HEAD_DIM, hg_width // HEAD_DIM
    c_naq = pool_width // HEAD_DIM
    c_nak, c_nav = c_naq + na_heads, c_naq + 2 * na_heads
    c_hq = c_naq + 3 * na_heads
    c_hff, c_hfb, c_hi, c_hg = (c_hq + k * hg_heads for k in range(1, 5))
    assert B + 1 <= COND_ROWS

    hall = jnp.concatenate([x.reshape(B * L, D), ctx.reshape(B * Lc, D)], axis=0)
    cond = jnp.concatenate([c, c_ctx[None], jnp.zeros((COND_ROWS - B - 1, D), F32)], axis=0)
    mod = _adaln_call(cond, w_mod, b_mod)
    lb_soft = jax.nn.softmax(hg_lb.astype(F32), axis=1)
    lower = jnp.cumsum(lb_soft, axis=1) - lb_soft[:, :1]
    rw_pad = jnp.pad(router_w, ((0, 0), (0, 0), (0, HEAD_DIM - N_EXPERTS)))
    rb_pad = jnp.pad(router_b, ((0, 0), (0, HEAD_DIM - N_EXPERTS)))

    for l in range(depth):
        last = l == depth - 1
        px = _in_proj_call(hall, mod[l], w_in[l].astype(BF16), B, L)
        a_mix = _pool_call(px, pool_w[l].astype(BF16), pool_scale[l], B, L, Lc)
        b_x, b_c = _na_call(px, na_rpb[l], B, L, Lc, c_naq, c_nak, c_nav, na_heads)
        o_f, o_b = _hgrn_call(px, lower[0, l].reshape(hg_heads, 1, HEAD_DIM),
                              lower[1, l].reshape(hg_heads, 1, HEAD_DIM), B, L, Lc,
                              c_hq, c_hff, c_hfb, c_hi, hg_heads)
        c_mix = _readout_call(o_f, o_b, px, hg_norm_g[l], c_hg, 512)
        mix = jnp.concatenate([a_mix, jnp.concatenate([b_x, b_c], axis=0), c_mix], axis=1)
        n_out = B * L if last else hall.shape[0]
        h1, fin, gates, topi = _out_proj_call(mix, hall, w_out[l].astype(BF16), mod[l], ln1_g[l], ln1_b[l],
                                              rw_pad[l], rb_pad[l].reshape(1, -1), n_out, B, L, alpha)
        row_token, row_w, pos, tile_expert, tile_valid = _moe_dispatch(topi[:, :TOP_K], gates, MOE_TILE)
        xs = jnp.take(fin, row_token, axis=0)
        ys = _moe_call(xs, row_w, tile_expert, tile_valid, exp_w1[l].astype(BF16), exp_b1[l],
                       exp_w2[l].astype(BF16), exp_b2[l])
        f = jnp.sum(jnp.take(ys, pos.reshape(-1), axis=0).reshape(n_out, TOP_K, D).astype(F32), axis=1)
        hall = _final_call(h1, f, mod[l], ln2_g[l], ln2_b[l], B, L, alpha)
    return hall.reshape(B, L, D)
```

```python
import functools

import jax
import jax.numpy as jnp
import numpy as np
from jax import lax
from jax.experimental import pallas as pl
from jax.experimental.pallas import tpu as pltpu

F32 = jnp.float32
BF16 = jnp.bfloat16

HEAD_DIM = 128
GRID_W = 64
POOL_WINDOWS = (2, 4, 8, 16)
NA_ROWS = 8
NA_COLS = 16
FORGET_EPS = 1e-20
N_EXPERTS = 32
TOP_K = 4
SWIGLU_LIMIT = 7.0
SWIGLU_ALPHA = 1.702
ROPE_BASE = 10000.0
LN_EPS = 1e-6

VMEM_LIMIT_BYTES = 56 * 1024 * 1024
HG_CHUNK = 256
NEG_BIG = -1e30


def _params(sem):
    return pltpu.CompilerParams(dimension_semantics=sem, vmem_limit_bytes=VMEM_LIMIT_BYTES)


def _dot(a, b):
    return jnp.dot(a, b, preferred_element_type=F32)


def _dot_nt(a, b):
    return lax.dot_general(a, b, (((1,), (1,)), ((), ())), preferred_element_type=F32)


def _sigmoid(x):
    return 1.0 / (1.0 + jnp.exp(-x))


LOG2E = 1.4426950408889634
HG_HALF = HG_CHUNK // 2


def _hgrn_half_scores(q, k, b2, lev, reverse):
    n = q.shape[0]
    row = lax.broadcasted_iota(jnp.int32, (n, HEAD_DIM), 0)
    a = jnp.where(lev == -1, _dot_nt(q.astype(BF16), k.astype(BF16)), 0.0)
    h, lvl = 1, 0
    while h < n:
        m0 = h if reverse else h - 1
        if h < 8:
            upper = (row & h) != 0
            t_role = jnp.logical_not(upper) if reverse else upper
            qk = jnp.where(t_role, q, k)
        else:
            q4 = q.reshape(n // (2 * h), 2, h, HEAD_DIM)
            k4 = k.reshape(n // (2 * h), 2, h, HEAD_DIM)
            parts = [q4[:, :1], k4[:, 1:]] if reverse else [k4[:, :1], q4[:, 1:]]
            qk = jnp.concatenate(parts, axis=1).reshape(n, HEAD_DIM)
        if h == 1:
            ref = jnp.where(t_role, pltpu.roll(b2, n - 1 if reverse else 1, 0), b2)
        elif h == 2:
            r4 = row & 3
            if reverse:
                ref = jnp.where(r4 == 0, pltpu.roll(b2, n - 2, 0),
                                jnp.where(r4 == 1, pltpu.roll(b2, n - 1, 0),
                                          jnp.where(r4 == 2, b2, pltpu.roll(b2, 1, 0))))
            else:
                ref = jnp.where(r4 == 0, pltpu.roll(b2, n - 1, 0),
                                jnp.where(r4 == 1, b2,
                                          jnp.where(r4 == 2, pltpu.roll(b2, 1, 0), pltpu.roll(b2, 2, 0))))
        else:
            b3 = b2.reshape(n // (2 * h), 2 * h, HEAD_DIM)
            ref = jnp.broadcast_to(b3[:, m0:m0 + 1, :], b3.shape).reshape(n, HEAD_DIM)
        x = (qk * jnp.exp2(-jnp.abs(b2 - ref))).astype(BF16)
        a = jnp.where(lev == lvl, _dot_nt(x, x), a)
        h *= 2
        lvl += 1
    return a


def _hgrn_direction(q, z, v_bf, lb, st_ref, lev, tri_bf, reverse):
    C, H = HG_CHUNK, HG_HALF
    q = q * _sigmoid(q)
    sig = _sigmoid(z)
    f = lb + (1.0 - lb) * sig
    g2 = jnp.log(jnp.maximum(f, FORGET_EPS)) * LOG2E
    k = (1.0 - lb) * (1.0 - sig)

    g_hi = g2.astype(BF16)
    r1 = g2 - g_hi.astype(F32)
    g_mid = r1.astype(BF16)
    g_lo = (r1 - g_mid.astype(F32)).astype(BF16)
    b2 = _dot(tri_bf, g_hi) + _dot(tri_bf, g_mid) + _dot(tri_bf, g_lo)

    lo, hi = slice(0, H), slice(H, C)
    a_lo = _hgrn_half_scores(q[lo], k[lo], b2[lo], lev, reverse).astype(BF16)
    a_hi = _hgrn_half_scores(q[hi], k[hi], b2[hi], lev, reverse).astype(BF16)
    t_sl, s_sl, m = (lo, hi, H) if reverse else (hi, lo, H - 1)
    ref = b2[m:m + 1, :]
    qt = (q[t_sl] * jnp.exp2(-jnp.abs(b2[t_sl] - ref))).astype(BF16)
    ks = (k[s_sl] * jnp.exp2(-jnp.abs(b2[s_sl] - ref))).astype(BF16)
    cross = _dot(_dot_nt(qt, ks).astype(BF16), v_bf[s_sl])
    o_lo = _dot(a_lo, v_bf[lo])
    o_hi = _dot(a_hi, v_bf[hi])
    o_intra = jnp.concatenate([o_lo + cross, o_hi] if reverse else [o_lo, o_hi + cross], axis=0)

    st = st_ref[...]
    tot = b2[0:1, :] if reverse else b2[C - 1:C, :]
    qdec = (q * jnp.exp2(b2)).astype(BF16)
    o_inter = _dot_nt(qdec, st.astype(BF16))
    kdec = (k * jnp.exp2(tot - b2)).astype(BF16)
    vt = v_bf.astype(F32).T.astype(BF16)
    st_ref[...] = jnp.exp2(tot) * st + _dot(vt, kdec)
    return o_intra + o_inter


def _hgrn_kernel(qf_ref, zf_ref, vf_ref, qb_ref, zb_ref, vb_ref, lbf_ref, lbb_ref, levf_ref, levb_ref,
                 trif_ref, trib_ref, of_ref, ob_ref, sf_ref, sb_ref):
    @pl.when(pl.program_id(2) == 0)
    def _():
        sf_ref[...] = jnp.zeros_like(sf_ref)
        sb_ref[...] = jnp.zeros_like(sb_ref)

    of_ref[...] = _hgrn_direction(qf_ref[...].astype(F32), zf_ref[...].astype(F32), vf_ref[...],
                                  lbf_ref[0], sf_ref, levf_ref[...], trif_ref[...], False)
    ob_ref[...] = _hgrn_direction(qb_ref[...].astype(F32), zb_ref[...].astype(F32), vb_ref[...],
                                  lbb_ref[0], sb_ref, levb_ref[...], trib_ref[...], True)


def _hgrn_call(px, lb_f, lb_b, B, L, Lc, col_q, col_ff, col_fb, col_i, n_heads):
    C, H = HG_CHUNK, HG_HALF
    T = px.shape[0]
    nlc, ncc = L // C, Lc // C
    ns = nlc + ncc
    ctx0 = B * nlc

    def fwd_row(b, s):
        return jnp.where(s < ncc, ctx0 + b * ncc + s, b * nlc + s - ncc)

    def bwd_row(b, s):
        return jnp.where(s < ncc, ctx0 + b * ncc + (ncc - 1 - s), b * nlc + (nlc - 1 - (s - ncc)))

    def spec(row_fn, col):
        return pl.BlockSpec((C, HEAD_DIM), lambda b, h, s: (row_fn(b, s), col + h))

    idx = np.arange(H)
    xr = idx[:, None] ^ idx[None, :]
    level = np.floor(np.log2(np.maximum(xr, 1))).astype(np.int32)
    lev_f = np.where(xr == 0, -1, np.where(idx[:, None] > idx[None, :], level, -2)).astype(np.int32)
    lev_b = np.ascontiguousarray(lev_f.T)
    idc = np.arange(C)
    tri_f = (idc[:, None] >= idc[None, :]).astype(np.float32)
    tri_b = (idc[:, None] <= idc[None, :]).astype(np.float32)
    const = lambda shape: pl.BlockSpec(shape, lambda b, h, s: (0,) * len(shape))
    lb_spec = pl.BlockSpec((1, 1, HEAD_DIM), lambda b, h, s: (h, 0, 0))
    out_sds = jax.ShapeDtypeStruct((T, n_heads * HEAD_DIM), F32)
    return pl.pallas_call(
        _hgrn_kernel,
        grid=(B, n_heads, ns),
        in_specs=[spec(fwd_row, col_q), spec(fwd_row, col_ff), spec(fwd_row, col_i),
                  spec(bwd_row, col_q), spec(bwd_row, col_fb), spec(bwd_row, col_i),
                  lb_spec, lb_spec, const((H, H)), const((H, H)), const((C, C)), const((C, C))],
        out_specs=[pl.BlockSpec((C, HEAD_DIM), lambda b, h, s: (fwd_row(b, s), h)),
                   pl.BlockSpec((C, HEAD_DIM), lambda b, h, s: (bwd_row(b, s), h))],
        out_shape=[out_sds, out_sds],
        scratch_shapes=[pltpu.VMEM((HEAD_DIM, HEAD_DIM), F32), pltpu.VMEM((HEAD_DIM, HEAD_DIM), F32)],
        compiler_params=_params(("parallel", "parallel", "arbitrary")),
        name="hgrn_scan",
    )(px, px, px, px, px, px, lb_f, lb_b, jnp.asarray(lev_f), jnp.asarray(lev_b),
      jnp.asarray(tri_f, BF16), jnp.asarray(tri_b, BF16))


NA_QROWS = 4
NA_KROWS = NA_QROWS + NA_ROWS


def _na_block_cases(rows):
    assert rows % NA_QROWS == 0 and rows >= 2 * NA_ROWS
    nb = rows // NA_QROWS
    cases = []
    for i in (0, 1, nb - 1):
        r0 = NA_QROWS * i
        ks = min(max(r0 - NA_ROWS // 2, 0), rows - NA_KROWS)
        per_q = []
        for qr in range(NA_QROWS):
            r = r0 + qr
            rs = min(max(r - NA_ROWS // 2, 0), rows - NA_ROWS)
            per_q.append((rs - ks, ks - r + NA_ROWS - 1))
        cases.append(per_q)
    return cases


def _na_kernel(rpb_ref, q_ref, k_ref, v_ref, qc_ref, kc_ref, vc_ref, cos_ref, sin_ref,
               ox_ref, oc_ref, qr_ref, kr_ref, tb_ref, bias_ref, *, rows):
    h = pl.program_id(1)
    L = q_ref.shape[0]
    W = GRID_W
    scale = HEAD_DIM ** -0.5
    n_dr = 2 * NA_ROWS - 1
    n_dc = 2 * NA_COLS - 1

    c_i = lax.broadcasted_iota(jnp.int32, (W, W), 0)
    kc_i = lax.broadcasted_iota(jnp.int32, (W, W), 1)
    cs = jnp.clip(c_i - NA_COLS // 2, 0, W - NA_COLS)
    in_win = (kc_i >= cs) & (kc_i < cs + NA_COLS)
    c_off = kc_i - c_i + (NA_COLS - 1)
    for dr in range(n_dr):
        acc = jnp.zeros((W, W), F32)
        for j in range(n_dc):
            acc = jnp.where(c_off == j, rpb_ref[h, dr, j], acc)
        tb_ref[dr] = jnp.where(in_win, acc, NEG_BIG)
    neg = jnp.full((W, W), NEG_BIG, F32)
    for ci, per_q in enumerate(_na_block_cases(rows)):
        for qr, (j0, dr0) in enumerate(per_q):
            for j in range(NA_KROWS):
                valid = j0 <= j < j0 + NA_ROWS
                blk = tb_ref[dr0 + j] if valid else neg
                bias_ref[ci, qr * W:(qr + 1) * W, j * W:(j + 1) * W] = blk

    RC = 256
    lane = lax.broadcasted_iota(jnp.int32, (RC, HEAD_DIM), 1)
    first = (lane & (HEAD_DIM // 2 - 1)) < HEAD_DIM // 4

    def rope_chunk(i, carry):
        sl = pl.ds(pl.multiple_of(i * RC, RC), RC)
        cos = cos_ref[sl, :]
        sin = sin_ref[sl, :]
        for src, dst, mul in ((q_ref, qr_ref, scale), (k_ref, kr_ref, 1.0)):
            t = src[sl, :].astype(F32)
            partner = jnp.where(first, pltpu.roll(t, HEAD_DIM - HEAD_DIM // 4, 1), pltpu.roll(t, HEAD_DIM // 4, 1))
            dst[sl, :] = ((t * cos + partner * sin) * mul).astype(BF16)
        return carry

    lax.fori_loop(0, L // RC, rope_chunk, 0)

    kc = kc_ref[...]
    vc = vc_ref[...]
    nb = rows // NA_QROWS
    QB = NA_QROWS * W
    KB = NA_KROWS * W

    def block(i, carry):
        ks = jnp.clip(i * NA_QROWS - NA_ROWS // 2, 0, rows - NA_KROWS)
        case = jnp.where(i == 0, 0, jnp.where(i == nb - 1, 2, 1))
        qs = pl.ds(pl.multiple_of(i * QB, QB), QB)
        kslice = pl.ds(pl.multiple_of(ks * W, W), KB)
        qb = qr_ref[qs, :]
        s_loc = _dot_nt(qb, kr_ref[kslice, :]) + bias_ref[case]
        s_ctx = _dot_nt(qb, kc)
        m = jnp.maximum(jnp.max(s_loc, -1, keepdims=True), jnp.max(s_ctx, -1, keepdims=True))
        p_loc = jnp.exp(s_loc - m)
        p_ctx = jnp.exp(s_ctx - m)
        den = jnp.sum(p_loc, -1, keepdims=True) + jnp.sum(p_ctx, -1, keepdims=True)
        o = _dot(p_loc.astype(BF16), v_ref[kslice, :]) + _dot(p_ctx.astype(BF16), vc)
        ox_ref[qs, :] = (o / den).astype(ox_ref.dtype)
        return carry

    lax.fori_loop(0, nb, block, 0)

    qc = (qc_ref[...].astype(F32) * scale).astype(BF16)
    s = _dot_nt(qc, kc)
    p = jnp.exp(s - jnp.max(s, -1, keepdims=True))
    oc = _dot(p.astype(BF16), vc) / jnp.sum(p, -1, keepdims=True)
    oc_ref[...] = oc.astype(oc_ref.dtype)


def _rope_tables(L):
    pos = jnp.arange(L)
    half = HEAD_DIM // 2
    inv = ROPE_BASE ** (-jnp.arange(0, half, 2, dtype=F32) / half)

    def tab(p):
        ang = p.astype(F32)[:, None] * inv[None]
        cos, sin = jnp.cos(ang), jnp.sin(ang)
        return jnp.concatenate([cos, cos], -1), jnp.concatenate([-sin, sin], -1)

    c_r, s_r = tab(pos // GRID_W)
    c_c, s_c = tab(pos % GRID_W)
    return jnp.concatenate([c_r, c_c], -1), jnp.concatenate([s_r, s_c], -1)


def _na_call(px, rpb, B, L, Lc, col_q, col_k, col_v, n_heads):
    rows = L // GRID_W
    cos, sin = _rope_tables(L)
    ctx0 = B * L // Lc

    def lat(col):
        return pl.BlockSpec((L, HEAD_DIM), lambda b, h: (b, col + h))

    def cx(col):
        return pl.BlockSpec((Lc, HEAD_DIM), lambda b, h: (ctx0 + b, col + h))

    tab = pl.BlockSpec((L, HEAD_DIM), lambda b, h: (0, 0))
    W = GRID_W
    return pl.pallas_call(
        functools.partial(_na_kernel, rows=rows),
        grid=(B, n_heads),
        in_specs=[pl.BlockSpec(memory_space=pltpu.SMEM), lat(col_q), lat(col_k), lat(col_v),
                  cx(col_q), cx(col_k), cx(col_v), tab, tab],
        out_specs=[pl.BlockSpec((L, HEAD_DIM), lambda b, h: (b, h)),
                   pl.BlockSpec((Lc, HEAD_DIM), lambda b, h: (b, h))],
        out_shape=[jax.ShapeDtypeStruct((B * L, n_heads * HEAD_DIM), BF16),
                   jax.ShapeDtypeStruct((B * Lc, n_heads * HEAD_DIM), BF16)],
        scratch_shapes=[pltpu.VMEM((L, HEAD_DIM), BF16), pltpu.VMEM((L, HEAD_DIM), BF16),
                        pltpu.VMEM((2 * NA_ROWS - 1, W, W), F32),
                        pltpu.VMEM((3, NA_QROWS * W, NA_KROWS * W), F32)],
        compiler_params=_params(("parallel", "parallel")),
        name="neighborhood_attention",
    )(rpb, px, px, px, px, px, px, cos, sin)


POOL_TILE = 256
POOL_HALO = 16


def _pool_kernel(prev_ref, cur_ref, next_ref, w_ref, scale_ref, o_ref, *, n_lat_tiles, nl, nc):
    i = pl.program_id(0)
    is_lat = i < n_lat_tiles
    j = jnp.where(is_lat, i % nl, (i - n_lat_tiles) % nc)
    n = jnp.where(is_lat, nl, nc)
    TQ, HL = POOL_TILE, POOL_HALO
    lo_min = jnp.where(j > 0, -HL, 0)
    hi_max = jnp.where(j < n - 1, TQ + HL, TQ)
    src = jnp.concatenate([prev_ref[...], cur_ref[...], next_ref[...]], axis=0)
    t = lax.broadcasted_iota(jnp.int32, (TQ, TQ + 2 * HL), 0)
    s = lax.broadcasted_iota(jnp.int32, (TQ, TQ + 2 * HL), 1) - HL
    Dg = w_ref.shape[1]
    t1 = lax.broadcasted_iota(jnp.int32, (TQ, Dg), 0)
    for g, w in enumerate(POOL_WINDOWS):
        lo = jnp.maximum(t - w // 2, lo_min)
        hi = jnp.minimum(t + (w - w // 2), hi_max)
        band = jnp.where(s >= lo, jnp.where(s < hi, 1.0, 0.0), 0.0).astype(BF16)
        cnt = (jnp.minimum(t1 + (w - w // 2), hi_max) - jnp.maximum(t1 - w // 2, lo_min)).astype(F32)
        cols = slice(g * Dg, (g + 1) * Dg)
        win = _dot(band, src[:, cols])
        d = win / cnt - cur_ref[:, cols].astype(F32)
        y = _dot(d.astype(BF16), w_ref[g]) * scale_ref[:, cols]
        o_ref[:, cols] = y.astype(o_ref.dtype)


def _pool_call(px, pool_w_bf, pool_scale, B, L, Lc):
    T = px.shape[0]
    G, Dg, _ = pool_w_bf.shape
    PW = G * Dg
    TQ, HL = POOL_TILE, POOL_HALO
    r = TQ // HL
    n_tiles = T // TQ
    n_halo = T // HL
    kern = functools.partial(_pool_kernel, n_lat_tiles=B * L // TQ, nl=L // TQ, nc=Lc // TQ)
    return pl.pallas_call(
        kern,
        grid=(n_tiles,),
        in_specs=[pl.BlockSpec((HL, PW), lambda i: (jnp.maximum(i * r - 1, 0), 0)),
                  pl.BlockSpec((TQ, PW), lambda i: (i, 0)),
                  pl.BlockSpec((HL, PW), lambda i: (jnp.minimum((i + 1) * r, n_halo - 1), 0)),
                  pl.BlockSpec((G, Dg, Dg), lambda i: (0, 0, 0)),
                  pl.BlockSpec((1, PW), lambda i: (0, 0))],
        out_specs=pl.BlockSpec((TQ, PW), lambda i: (i, 0)),
        out_shape=jax.ShapeDtypeStruct((T, PW), BF16),
        compiler_params=_params(("parallel",)),
        name="multiscale_pool",
    )(px, px, px, pool_w_bf, pool_scale.reshape(1, PW).astype(F32))


def _readout_kernel(of_ref, ob_ref, gate_ref, ng_ref, o_ref):
    o = of_ref[...] + ob_ref[...]
    o = o * lax.rsqrt(jnp.mean(o * o, -1, keepdims=True) + LN_EPS)
    g = gate_ref[...].astype(F32)
    o_ref[...] = (o * ng_ref[...] * (g * _sigmoid(g))).astype(o_ref.dtype)


def _readout_call(o_f, o_b, px, norm_g, col_gate, tm):
    T, HW = o_f.shape
    blk = lambda col: pl.BlockSpec((tm, HEAD_DIM), lambda i, h: (i, col + h))
    return pl.pallas_call(
        _readout_kernel,
        grid=(T // tm, HW // HEAD_DIM),
        in_specs=[blk(0), blk(0), blk(col_gate), pl.BlockSpec((1, HEAD_DIM), lambda i, h: (0, h))],
        out_specs=blk(0),
        out_shape=jax.ShapeDtypeStruct((T, HW), BF16),
        compiler_params=_params(("parallel", "parallel")),
        name="hgrn_readout",
    )(o_f, o_b, px, norm_g.reshape(1, HW).astype(F32))


COND_ROWS = 16


def _adaln_kernel(cond_ref, w_ref, b_ref, o_ref):
    cnd = cond_ref[...]
    a = (cnd * _sigmoid(cnd)).astype(BF16)
    o_ref[0] = _dot(a, w_ref[0].astype(BF16)) + b_ref[0]


def _adaln_call(cond, w_mod, b_mod, tn=1024):
    depth, D, N = w_mod.shape
    return pl.pallas_call(
        _adaln_kernel,
        grid=(depth, N // tn),
        in_specs=[pl.BlockSpec((COND_ROWS, D), lambda l, j: (0, 0)),
                  pl.BlockSpec((1, D, tn), lambda l, j: (l, 0, j)),
                  pl.BlockSpec((1, 1, tn), lambda l, j: (l, 0, j))],
        out_specs=pl.BlockSpec((1, COND_ROWS, tn), lambda l, j: (l, 0, j)),
        out_shape=jax.ShapeDtypeStruct((depth, COND_ROWS, N), F32),
        compiler_params=_params(("parallel", "parallel")),
        name="adaln",
    )(cond, w_mod, b_mod.reshape(depth, 1, N))


def _cond_row(i, tm, n_lat_rows, L, B):
    return jnp.where(i * tm < n_lat_rows, (i * tm) // L, B)


def _layer_norm_rows(x):
    mu = jnp.mean(x, -1, keepdims=True)
    xc = x - mu
    var = jnp.mean(xc * xc, -1, keepdims=True)
    return xc * lax.rsqrt(var + LN_EPS)


def _ln_mod_kernel(x_ref, mod_ref, o_ref, *, tm, n_lat_rows, L, B, D):
    r = _cond_row(pl.program_id(0), tm, n_lat_rows, L, B)
    shift = mod_ref[pl.ds(r, 1), 0:D]
    scale = mod_ref[pl.ds(r, 1), D:2 * D]
    o_ref[...] = (_layer_norm_rows(x_ref[...]) * (1.0 + scale) + shift).astype(o_ref.dtype)


def _ln_mod_call(h, mod_l, B, L, tm=256):
    T, D = h.shape
    kern = functools.partial(_ln_mod_kernel, tm=tm, n_lat_rows=B * L, L=L, B=B, D=D)
    return pl.pallas_call(
        kern,
        grid=(T // tm,),
        in_specs=[pl.BlockSpec((tm, D), lambda i: (i, 0)), pl.BlockSpec(mod_l.shape, lambda i: (0, 0))],
        out_specs=pl.BlockSpec((tm, D), lambda i: (i, 0)),
        out_shape=jax.ShapeDtypeStruct((T, D), BF16),
        compiler_params=_params(("parallel",)),
        name="ln_modulate",
    )(h, mod_l)


def _wstat_matmul_kernel(*refs, widths):
    n = len(widths)
    a_refs, w_ref, o_ref, wbf_ref = refs[:n], refs[n], refs[n + 1], refs[n + 2]

    @pl.when(pl.program_id(1) == 0)
    def _():
        wbf_ref[...] = w_ref[0].astype(BF16)

    acc = None
    k0 = 0
    for a_ref, kw in zip(a_refs, widths):
        part = _dot(a_ref[...], wbf_ref[k0:k0 + kw, :])
        acc = part if acc is None else acc + part
        k0 += kw
    o_ref[...] = acc.astype(o_ref.dtype)


def _wstat_matmul_call(a_list, w_all, layer, n_rows, tm, tn, name):
    widths = tuple(a.shape[1] for a in a_list)
    _, K, N = w_all.shape
    assert sum(widths) == K and n_rows % tm == 0 and N % tn == 0
    return pl.pallas_call(
        functools.partial(_wstat_matmul_kernel, widths=widths),
        grid=(N // tn, n_rows // tm),
        in_specs=[pl.BlockSpec((tm, kw), lambda j, i: (i, 0)) for kw in widths]
        + [pl.BlockSpec((1, K, tn), lambda j, i: (layer, 0, j))],
        out_specs=pl.BlockSpec((tm, tn), lambda j, i: (i, j)),
        out_shape=jax.ShapeDtypeStruct((n_rows, N), BF16),
        scratch_shapes=[pltpu.VMEM((K, tn), BF16)],
        compiler_params=_params(("parallel", "arbitrary")),
        name=name,
    )(*a_list, w_all)


def _split_bf16(x):
    hi = x.astype(BF16)
    return hi, (x - hi.astype(F32)).astype(BF16)


def _post_mix_kernel(y_ref, h_ref, mod_ref, g_ref, b_ref, rw_ref, rb_ref,
                     h1_ref, fin_ref, gates_ref, topi_ref, *, tm, n_lat_rows, L, B, D, alpha):
    r = _cond_row(pl.program_id(0), tm, n_lat_rows, L, B)
    mrow = lambda slot: mod_ref[pl.ds(r, 1), slot * D:(slot + 1) * D]
    u = alpha * h_ref[...] + mrow(2) * y_ref[...].astype(F32)
    h1 = _layer_norm_rows(u) * g_ref[...] + b_ref[...]
    h1_ref[...] = h1
    fin = _layer_norm_rows(h1) * (1.0 + mrow(4)) + mrow(3)
    fin_ref[...] = fin.astype(BF16)
    f_hi, f_lo = _split_bf16(fin)
    w_hi, w_lo = _split_bf16(rw_ref[...])
    logits = _dot(f_hi, w_hi) + _dot(f_hi, w_lo) + _dot(f_lo, w_hi) + rb_ref[...]
    lane = lax.broadcasted_iota(jnp.int32, logits.shape, 1)
    ninf = jnp.float32(-jnp.inf)
    xs = jnp.where(lane < N_EXPERTS, logits, ninf)
    sel_v, sel_i = [], []
    for _ in range(TOP_K):
        m = jnp.max(xs, -1, keepdims=True)
        idx = jnp.min(jnp.where(xs == m, lane, logits.shape[1]), -1, keepdims=True)
        sel_v.append(m)
        sel_i.append(idx)
        xs = jnp.where(lane == idx, ninf, xs)
    ex = [jnp.exp(v - sel_v[0]) for v in sel_v]
    den = ex[0]
    for e in ex[1:]:
        den = den + e
    gates = jnp.zeros(logits.shape, F32)
    topi = jnp.zeros(logits.shape, jnp.int32)
    for j in range(TOP_K):
        gates = jnp.where(lane == sel_i[j], ex[j] / den, gates)
        topi = jnp.where(lane == j, sel_i[j], topi)
    gates_ref[...] = gates
    topi_ref[...] = topi


def _post_mix_call(y, h, mod_l, ln_g, ln_b, rw_pad, rb_pad, B, L, alpha, tm=256):
    n_out, D = y.shape
    NE = rw_pad.shape[1]
    kern = functools.partial(_post_mix_kernel, tm=tm, n_lat_rows=B * L, L=L, B=B, D=D, alpha=alpha)
    row = lambda i: (i, 0)
    fixed = lambda i: (0, 0)
    return pl.pallas_call(
        kern,
        grid=(n_out // tm,),
        in_specs=[pl.BlockSpec((tm, D), row), pl.BlockSpec((tm, D), row),
                  pl.BlockSpec(mod_l.shape, fixed),
                  pl.BlockSpec((1, D), fixed), pl.BlockSpec((1, D), fixed),
                  pl.BlockSpec((D, NE), fixed), pl.BlockSpec((1, NE), fixed)],
        out_specs=[pl.BlockSpec((tm, D), row), pl.BlockSpec((tm, D), row),
                   pl.BlockSpec((tm, NE), row), pl.BlockSpec((tm, NE), row)],
        out_shape=[jax.ShapeDtypeStruct((n_out, D), F32), jax.ShapeDtypeStruct((n_out, D), BF16),
                   jax.ShapeDtypeStruct((n_out, NE), F32), jax.ShapeDtypeStruct((n_out, NE), jnp.int32)],
        compiler_params=_params(("parallel",)),
        name="post_mix_router",
    )(y, h, mod_l, ln_g.reshape(1, D), ln_b.reshape(1, D), rw_pad, rb_pad)


MOE_TILE = 256


def _visit_flags(v, vt_ref, ve_ref):
    p = jnp.maximum(v - 1, 0)
    return (v == 0) | (vt_ref[p] != vt_ref[v]), (v == 0) | (ve_ref[p] != ve_ref[v])


def _visit_row_mask(v, vt_ref, ve_ref, offs_ref, shape):
    e = ve_ref[v]
    row = lax.broadcasted_iota(jnp.int32, shape, 0) + vt_ref[v] * shape[0]
    return jnp.logical_and(row >= offs_ref[e], row < offs_ref[e + 1])


def _masked_tile_store(o_ref, val, mask, first_tile):
    @pl.when(first_tile)
    def _():
        o_ref[...] = jnp.where(mask, val, 0.0).astype(o_ref.dtype)

    @pl.when(jnp.logical_not(first_tile))
    def _():
        o_ref[...] = jnp.where(mask, val, o_ref[...].astype(F32)).astype(o_ref.dtype)


def _moe_up_kernel(vt_ref, ve_ref, vv_ref, offs_ref, x_ref, w1_ref, b1_ref, o_ref, wbf_ref, *, DE):
    v = pl.program_id(0)
    first_tile, first_expert = _visit_flags(v, vt_ref, ve_ref)

    @pl.when(first_expert)
    def _():
        wbf_ref[...] = w1_ref[0, 0].astype(BF16)

    @pl.when(vv_ref[v] > 0)
    def _():
        hcat = _dot(x_ref[...], wbf_ref[...]) + b1_ref[0, 0]
        gate = jnp.minimum(hcat[:, :DE], SWIGLU_LIMIT)
        up = jnp.clip(hcat[:, DE:], -SWIGLU_LIMIT, SWIGLU_LIMIT)
        hdn = (up + 1.0) * (gate * _sigmoid(SWIGLU_ALPHA * gate))
        mask = _visit_row_mask(v, vt_ref, ve_ref, offs_ref, hdn.shape)
        _masked_tile_store(o_ref, hdn, mask, first_tile)


def _moe_down_kernel(vt_ref, ve_ref, vv_ref, offs_ref, h_ref, rw_ref, w2_ref, b2_ref, o_ref, wbf_ref):
    v = pl.program_id(0)
    first_tile, first_expert = _visit_flags(v, vt_ref, ve_ref)

    @pl.when(first_expert)
    def _():
        wbf_ref[...] = w2_ref[0, 0].astype(BF16)

    @pl.when(vv_ref[v] > 0)
    def _():
        y = (_dot(h_ref[...], wbf_ref[...]) + b2_ref[0, 0]) * rw_ref[...]
        mask = _visit_row_mask(v, vt_ref, ve_ref, offs_ref, y.shape)
        _masked_tile_store(o_ref, y, mask, first_tile)


def _moe_call(xs, row_w, visits, w1_all, b1_all, w2_all, b2_all, layer):
    n, D = xs.shape
    _, E, _, DE2 = w1_all.shape
    DE = DE2 // 2
    tm = MOE_TILE
    nv = visits[0].shape[0]
    tile = lambda v, vt, ve, vv, offs: (vt[v], 0)
    expert = lambda v, vt, ve, vv, offs: (layer, ve[v], 0, 0)
    hdn = pl.pallas_call(
        functools.partial(_moe_up_kernel, DE=DE),
        grid_spec=pltpu.PrefetchScalarGridSpec(
            num_scalar_prefetch=4, grid=(nv,),
            in_specs=[pl.BlockSpec((tm, D), tile),
                      pl.BlockSpec((1, 1, D, DE2), expert),
                      pl.BlockSpec((1, 1, 1, DE2), expert)],
            out_specs=pl.BlockSpec((tm, DE), tile),
            scratch_shapes=[pltpu.VMEM((D, DE2), BF16)]),
        out_shape=jax.ShapeDtypeStruct((n, DE), BF16),
        compiler_params=_params(("arbitrary",)),
        name="moe_up",
    )(*visits, xs, w1_all, b1_all.reshape(b1_all.shape[0], E, 1, DE2))
    return pl.pallas_call(
        _moe_down_kernel,
        grid_spec=pltpu.PrefetchScalarGridSpec(
            num_scalar_prefetch=4, grid=(nv,),
            in_specs=[pl.BlockSpec((tm, DE), tile),
                      pl.BlockSpec((tm, 1), tile),
                      pl.BlockSpec((1, 1, DE, D), expert),
                      pl.BlockSpec((1, 1, 1, D), expert)],
            out_specs=pl.BlockSpec((tm, D), tile),
            scratch_shapes=[pltpu.VMEM((DE, D), BF16)]),
        out_shape=jax.ShapeDtypeStruct((n, D), BF16),
        compiler_params=_params(("arbitrary",)),
        name="moe_down",
    )(*visits, hdn, row_w, w2_all, b2_all.reshape(b2_all.shape[0], E, 1, D))


def _moe_dispatch(topi, gates, tm):
    T = topi.shape[0]
    E = N_EXPERTS
    n = T * TOP_K
    n_tiles = n // tm
    flat_e = topi.reshape(-1)
    flat_w = jnp.take_along_axis(gates, topi, axis=1).reshape(-1)
    order = jnp.argsort(flat_e, stable=True).astype(jnp.int32)
    inv = jnp.argsort(order).astype(jnp.int32)
    sorted_tok = order // TOP_K
    sorted_w = flat_w[order].reshape(n, 1)
    pos_kmajor = inv.reshape(T, TOP_K).T.reshape(-1)
    counts = jnp.sum((flat_e[:, None] == jnp.arange(E, dtype=jnp.int32)[None, :]).astype(jnp.int32), axis=0)
    ends = jnp.cumsum(counts)
    offs = jnp.concatenate([jnp.zeros((1,), jnp.int32), ends]).astype(jnp.int32)
    first_tile = offs[:-1] // tm
    n_vis = jnp.where(counts > 0, (ends - 1) // tm - first_tile + 1, 0)
    v_end = jnp.cumsum(n_vis)
    v_start = v_end - n_vis
    total = v_end[-1]
    nv = n_tiles + E - 1
    v = jnp.arange(nv, dtype=jnp.int32)
    ve = jnp.minimum(jnp.sum((v[:, None] >= v_end[None, :]).astype(jnp.int32), axis=1), E - 1)
    vt = first_tile[ve] + v - v_start[ve]
    valid = v < total
    ve = jnp.where(valid, ve, ve[total - 1]).astype(jnp.int32)
    vt = jnp.where(valid, vt, n_tiles - 1).astype(jnp.int32)
    return sorted_tok, sorted_w, pos_kmajor, (vt, ve, valid.astype(jnp.int32), offs)


def _final_kernel(h_ref, *refs, tm, n_lat_rows, L, B, D, alpha):
    y_refs, (mod_ref, g_ref, b_ref, o_ref) = refs[:TOP_K], refs[TOP_K:]
    r = _cond_row(pl.program_id(0), tm, n_lat_rows, L, B)
    gate = mod_ref[pl.ds(r, 1), 5 * D:6 * D]
    f = y_refs[0][...].astype(F32)
    for y_ref in y_refs[1:]:
        f = f + y_ref[...].astype(F32)
    u = alpha * h_ref[...] + gate * f
    o_ref[...] = _layer_norm_rows(u) * g_ref[...] + b_ref[...]


def _final_call(h1, y_assign, mod_l, ln_g, ln_b, B, L, alpha, tm=256):
    n, D = h1.shape
    nt = n // tm
    kern = functools.partial(_final_kernel, tm=tm, n_lat_rows=B * L, L=L, B=B, D=D, alpha=alpha)
    row = lambda i: (i, 0)
    fixed = lambda i: (0, 0)
    y_specs = [pl.BlockSpec((tm, D), functools.partial(lambda i, k: (k * nt + i, 0), k=k)) for k in range(TOP_K)]
    return pl.pallas_call(
        kern,
        grid=(nt,),
        in_specs=[pl.BlockSpec((tm, D), row)] + y_specs
        + [pl.BlockSpec(mod_l.shape, fixed), pl.BlockSpec((1, D), fixed), pl.BlockSpec((1, D), fixed)],
        out_specs=pl.BlockSpec((tm, D), row),
        out_shape=jax.ShapeDtypeStruct((n, D), F32),
        compiler_params=_params(("parallel",)),
        name="final_ln",
    )(h1, *([y_assign] * TOP_K), mod_l, ln_g.reshape(1, D), ln_b.reshape(1, D))


def kernel(x, c, ctx, c_ctx, w_mod, b_mod, w_in, pool_w, pool_scale, na_rpb, hg_lb, hg_norm_g, w_out, ln1_g, ln1_b, ln2_g, ln2_b, router_w, router_b, exp_w1, exp_b1, exp_w2, exp_b2):
    B, L, D = x.shape
    Lc = ctx.shape[1]
    depth = w_in.shape[0]
    alpha = (2 * depth) ** 0.25
    pool_width = pool_w.shape[1] * pool_w.shape[2]
    na_width = na_rpb.shape[1] * HEAD_DIM
    hg_width = hg_norm_g.shape[1]
    na_heads, hg_heads = na_width // HEAD_DIM, hg_width // HEAD_DIM
    c_naq = pool_width // HEAD_DIM
    c_nak, c_nav = c_naq + na_heads, c_naq + 2 * na_heads
    c_hq = c_naq + 3 * na_heads
    c_hff, c_hfb, c_hi, c_hg = (c_hq + k * hg_heads for k in range(1, 5))
    assert B + 1 <= COND_ROWS

    hall = jnp.concatenate([x.reshape(B * L, D), ctx.reshape(B * Lc, D)], axis=0)
    cond = jnp.concatenate([c, c_ctx[None], jnp.zeros((COND_ROWS - B - 1, D), F32)], axis=0)
    mod = _adaln_call(cond, w_mod, b_mod)
    lb_soft = jax.nn.softmax(hg_lb.astype(F32), axis=1)
    lower = jnp.cumsum(lb_soft, axis=1) - lb_soft[:, :1]
    rw_pad = jnp.pad(router_w, ((0, 0), (0, 0), (0, HEAD_DIM - N_EXPERTS)))
    rb_pad = jnp.pad(router_b, ((0, 0), (0, HEAD_DIM - N_EXPERTS)))

    T = hall.shape[0]
    n_row_tiles = 8
    for l in range(depth):
        last = l == depth - 1
        a_in = _ln_mod_call(hall, mod[l], B, L)
        px = _wstat_matmul_call([a_in], w_in, l, T, T // n_row_tiles, 512, "in_proj")
        a_mix = _pool_call(px, pool_w[l].astype(BF16), pool_scale[l], B, L, Lc)
        b_x, b_c = _na_call(px, na_rpb[l], B, L, Lc, c_naq, c_nak, c_nav, na_heads)
        b_mix = jnp.concatenate([b_x, b_c], axis=0)
        o_f, o_b = _hgrn_call(px, lower[0, l].reshape(hg_heads, 1, HEAD_DIM),
                              lower[1, l].reshape(hg_heads, 1, HEAD_DIM), B, L, Lc,
                              c_hq, c_hff, c_hfb, c_hi, hg_heads)
        c_mix = _readout_call(o_f, o_b, px, hg_norm_g[l], c_hg, 512)
        n_out = B * L if last else T
        y = _wstat_matmul_call([a_mix, b_mix, c_mix], w_out, l, n_out, n_out // n_row_tiles, 512, "out_proj")
        h1, fin, gates, topi = _post_mix_call(y, hall, mod[l], ln1_g[l], ln1_b[l], rw_pad[l],
                                              rb_pad[l].reshape(1, -1), B, L, alpha)
        sorted_tok, sorted_w, pos_kmajor, visits = _moe_dispatch(topi[:, :TOP_K], gates, MOE_TILE)
        xs = jnp.take(fin, sorted_tok, axis=0)
        ys = _moe_call(xs, sorted_w, visits, exp_w1, exp_b1, exp_w2, exp_b2, l)
        y_assign = jnp.take(ys, pos_kmajor, axis=0)
        hall = _final_call(h1, y_assign, mod[l], ln2_g[l], ln2_b[l], B, L, alpha)
    return hall.reshape(B, L, D)
```

```python
import functools

import jax
import jax.numpy as jnp
import numpy as np
from jax import lax
from jax.experimental import pallas as pl
from jax.experimental.pallas import tpu as pltpu

F32 = jnp.float32
BF16 = jnp.bfloat16

HEAD_DIM = 128
GRID_W = 64
POOL_WINDOWS = (2, 4, 8, 16)
NA_ROWS = 8
NA_COLS = 16
FORGET_EPS = 1e-20
N_EXPERTS = 32
TOP_K = 4
SWIGLU_LIMIT = 7.0
SWIGLU_ALPHA = 1.702
ROPE_BASE = 10000.0
LN_EPS = 1e-6

VMEM_LIMIT_BYTES = 56 * 1024 * 1024
HG_CHUNK = 256
NEG_BIG = -1e30


def _params(sem):
    return pltpu.CompilerParams(dimension_semantics=sem, vmem_limit_bytes=VMEM_LIMIT_BYTES)


def _dot(a, b):
    return jnp.dot(a, b, preferred_element_type=F32)


def _dot_nt(a, b):
    return lax.dot_general(a, b, (((1,), (1,)), ((), ())), preferred_element_type=F32)


def _sigmoid(x):
    return 1.0 / (1.0 + jnp.exp(-x))


LOG2E = 1.4426950408889634
HG_HALF = HG_CHUNK // 2
HG_HEADS_PER_STEP = 2


def _hgrn_half_scores(q, k, b2, lev, reverse):
    n = q.shape[0]
    row = lax.broadcasted_iota(jnp.int32, (n, HEAD_DIM), 0)
    a = jnp.where(lev == -1, _dot_nt(q.astype(BF16), k.astype(BF16)), 0.0)
    h, lvl = 1, 0
    while h < n:
        m0 = h if reverse else h - 1
        if h < 8:
            upper = (row & h) != 0
            t_role = jnp.logical_not(upper) if reverse else upper
            qk = jnp.where(t_role, q, k)
        else:
            q4 = q.reshape(n // (2 * h), 2, h, HEAD_DIM)
            k4 = k.reshape(n // (2 * h), 2, h, HEAD_DIM)
            parts = [q4[:, :1], k4[:, 1:]] if reverse else [k4[:, :1], q4[:, 1:]]
            qk = jnp.concatenate(parts, axis=1).reshape(n, HEAD_DIM)
        if h == 1:
            ref = jnp.where(t_role, pltpu.roll(b2, n - 1 if reverse else 1, 0), b2)
        elif h == 2:
            r4 = row & 3
            if reverse:
                ref = jnp.where(r4 == 0, pltpu.roll(b2, n - 2, 0),
                                jnp.where(r4 == 1, pltpu.roll(b2, n - 1, 0),
                                          jnp.where(r4 == 2, b2, pltpu.roll(b2, 1, 0))))
            else:
                ref = jnp.where(r4 == 0, pltpu.roll(b2, n - 1, 0),
                                jnp.where(r4 == 1, b2,
                                          jnp.where(r4 == 2, pltpu.roll(b2, 1, 0), pltpu.roll(b2, 2, 0))))
        else:
            b3 = b2.reshape(n // (2 * h), 2 * h, HEAD_DIM)
            ref = jnp.broadcast_to(b3[:, m0:m0 + 1, :], b3.shape).reshape(n, HEAD_DIM)
        x = (qk * jnp.exp2(-jnp.abs(b2 - ref))).astype(BF16)
        a = jnp.where(lev == lvl, _dot_nt(x, x), a)
        h *= 2
        lvl += 1
    return a


def _hgrn_direction(q, z, v_bf, lb, st_ref, lev, tri_bf, reverse):
    C, H = HG_CHUNK, HG_HALF
    q = q * _sigmoid(q)
    sig = _sigmoid(z)
    f = lb + (1.0 - lb) * sig
    g2 = jnp.log(jnp.maximum(f, FORGET_EPS)) * LOG2E
    k = (1.0 - lb) * (1.0 - sig)

    g_hi = g2.astype(BF16)
    r1 = g2 - g_hi.astype(F32)
    g_mid = r1.astype(BF16)
    g_lo = (r1 - g_mid.astype(F32)).astype(BF16)
    b2 = _dot(tri_bf, g_hi) + _dot(tri_bf, g_mid) + _dot(tri_bf, g_lo)

    lo, hi = slice(0, H), slice(H, C)
    a_lo = _hgrn_half_scores(q[lo], k[lo], b2[lo], lev, reverse).astype(BF16)
    a_hi = _hgrn_half_scores(q[hi], k[hi], b2[hi], lev, reverse).astype(BF16)
    t_sl, s_sl, m = (lo, hi, H) if reverse else (hi, lo, H - 1)
    ref = b2[m:m + 1, :]
    qt = (q[t_sl] * jnp.exp2(-jnp.abs(b2[t_sl] - ref))).astype(BF16)
    ks = (k[s_sl] * jnp.exp2(-jnp.abs(b2[s_sl] - ref))).astype(BF16)
    cross = _dot(_dot_nt(qt, ks).astype(BF16), v_bf[s_sl])
    o_lo = _dot(a_lo, v_bf[lo])
    o_hi = _dot(a_hi, v_bf[hi])
    o_intra = jnp.concatenate([o_lo + cross, o_hi] if reverse else [o_lo, o_hi + cross], axis=0)

    st = st_ref[...]
    tot = b2[0:1, :] if reverse else b2[C - 1:C, :]
    qdec = (q * jnp.exp2(b2)).astype(BF16)
    o_inter = _dot_nt(qdec, st.astype(BF16))
    kdec = (k * jnp.exp2(tot - b2)).astype(BF16)
    vt = v_bf.astype(F32).T.astype(BF16)
    st_ref[...] = jnp.exp2(tot) * st + _dot(vt, kdec)
    return o_intra + o_inter


def _hgrn_kernel(qf_ref, zf_ref, vf_ref, qb_ref, zb_ref, vb_ref, lbf_ref, lbb_ref, levf_ref, levb_ref,
                 trif_ref, trib_ref, of_ref, ob_ref, sf_ref, sb_ref):
    @pl.when(pl.program_id(2) == 0)
    def _():
        sf_ref[...] = jnp.zeros_like(sf_ref)
        sb_ref[...] = jnp.zeros_like(sb_ref)

    for hh in range(HG_HEADS_PER_STEP):
        sl = slice(hh * HEAD_DIM, (hh + 1) * HEAD_DIM)
        of_ref[:, sl] = _hgrn_direction(qf_ref[:, sl].astype(F32), zf_ref[:, sl].astype(F32), vf_ref[:, sl],
                                        lbf_ref[0][:, sl], sf_ref.at[hh], levf_ref[...], trif_ref[...],
                                        False).astype(of_ref.dtype)
        ob_ref[:, sl] = _hgrn_direction(qb_ref[:, sl].astype(F32), zb_ref[:, sl].astype(F32), vb_ref[:, sl],
                                        lbb_ref[0][:, sl], sb_ref.at[hh], levb_ref[...], trib_ref[...],
                                        True).astype(ob_ref.dtype)


def _hgrn_call(px, lb_f, lb_b, B, L, Lc, col_q, col_ff, col_fb, col_i, n_heads):
    C, H = HG_CHUNK, HG_HALF
    T = px.shape[0]
    nlc, ncc = L // C, Lc // C
    ns = nlc + ncc
    ctx0 = B * nlc

    def fwd_row(b, s):
        return jnp.where(s < ncc, ctx0 + b * ncc + s, b * nlc + s - ncc)

    def bwd_row(b, s):
        return jnp.where(s < ncc, ctx0 + b * ncc + (ncc - 1 - s), b * nlc + (nlc - 1 - (s - ncc)))

    G = HG_HEADS_PER_STEP
    GW = G * HEAD_DIM
    assert n_heads % G == 0 and all(c % G == 0 for c in (col_q, col_ff, col_fb, col_i))

    def spec(row_fn, col):
        return pl.BlockSpec((C, GW), lambda b, h, s: (row_fn(b, s), col // G + h))

    idx = np.arange(H)
    xr = idx[:, None] ^ idx[None, :]
    level = np.floor(np.log2(np.maximum(xr, 1))).astype(np.int32)
    lev_f = np.where(xr == 0, -1, np.where(idx[:, None] > idx[None, :], level, -2)).astype(np.int32)
    lev_b = np.ascontiguousarray(lev_f.T)
    idc = np.arange(C)
    tri_f = (idc[:, None] >= idc[None, :]).astype(np.float32)
    tri_b = (idc[:, None] <= idc[None, :]).astype(np.float32)
    const = lambda shape: pl.BlockSpec(shape, lambda b, h, s: (0,) * len(shape))
    lb_spec = pl.BlockSpec((1, 1, GW), lambda b, h, s: (h, 0, 0))
    out_sds = jax.ShapeDtypeStruct((T, n_heads * HEAD_DIM), BF16)
    state = pltpu.VMEM((G, HEAD_DIM, HEAD_DIM), F32)
    return pl.pallas_call(
        _hgrn_kernel,
        grid=(B, n_heads // G, ns),
        in_specs=[spec(fwd_row, col_q), spec(fwd_row, col_ff), spec(fwd_row, col_i),
                  spec(bwd_row, col_q), spec(bwd_row, col_fb), spec(bwd_row, col_i),
                  lb_spec, lb_spec, const((H, H)), const((H, H)), const((C, C)), const((C, C))],
        out_specs=[pl.BlockSpec((C, GW), lambda b, h, s: (fwd_row(b, s), h)),
                   pl.BlockSpec((C, GW), lambda b, h, s: (bwd_row(b, s), h))],
        out_shape=[out_sds, out_sds],
        scratch_shapes=[state, state],
        compiler_params=_params(("parallel", "parallel", "arbitrary")),
        name="hgrn_scan",
    )(px, px, px, px, px, px, lb_f.reshape(n_heads // G, 1, GW), lb_b.reshape(n_heads // G, 1, GW),
      jnp.asarray(lev_f), jnp.asarray(lev_b), jnp.asarray(tri_f, BF16), jnp.asarray(tri_b, BF16))


NA_QROWS = 4
NA_KROWS = NA_QROWS + NA_ROWS


def _na_block_cases(rows):
    assert rows % NA_QROWS == 0 and rows >= 2 * NA_ROWS
    nb = rows // NA_QROWS
    cases = []
    for i in (0, 1, nb - 1):
        r0 = NA_QROWS * i
        ks = min(max(r0 - NA_ROWS // 2, 0), rows - NA_KROWS)
        per_q = []
        for qr in range(NA_QROWS):
            r = r0 + qr
            rs = min(max(r - NA_ROWS // 2, 0), rows - NA_ROWS)
            per_q.append((rs - ks, ks - r + NA_ROWS - 1))
        cases.append(per_q)
    return cases


def _na_kernel(rpb_ref, q_ref, k_ref, v_ref, qc_ref, kc_ref, vc_ref, cos_ref, sin_ref,
               ox_ref, oc_ref, qr_ref, kr_ref, tb_ref, bias_ref, *, rows):
    h = pl.program_id(1)
    L = q_ref.shape[0]
    W = GRID_W
    scale = HEAD_DIM ** -0.5
    n_dr = 2 * NA_ROWS - 1
    n_dc = 2 * NA_COLS - 1

    c_i = lax.broadcasted_iota(jnp.int32, (W, W), 0)
    kc_i = lax.broadcasted_iota(jnp.int32, (W, W), 1)
    cs = jnp.clip(c_i - NA_COLS // 2, 0, W - NA_COLS)
    in_win = (kc_i >= cs) & (kc_i < cs + NA_COLS)
    c_off = kc_i - c_i + (NA_COLS - 1)
    for dr in range(n_dr):
        acc = jnp.zeros((W, W), F32)
        for j in range(n_dc):
            acc = jnp.where(c_off == j, rpb_ref[h, dr, j], acc)
        tb_ref[dr] = jnp.where(in_win, acc, NEG_BIG)
    neg = jnp.full((W, W), NEG_BIG, F32)
    for ci, per_q in enumerate(_na_block_cases(rows)):
        for qr, (j0, dr0) in enumerate(per_q):
            for j in range(NA_KROWS):
                valid = j0 <= j < j0 + NA_ROWS
                blk = tb_ref[dr0 + j] if valid else neg
                bias_ref[ci, qr * W:(qr + 1) * W, j * W:(j + 1) * W] = blk

    RC = 256
    lane = lax.broadcasted_iota(jnp.int32, (RC, HEAD_DIM), 1)
    first = (lane & (HEAD_DIM // 2 - 1)) < HEAD_DIM // 4

    def rope_chunk(i, carry):
        sl = pl.ds(pl.multiple_of(i * RC, RC), RC)
        cos = cos_ref[sl, :]
        sin = sin_ref[sl, :]
        for src, dst, mul in ((q_ref, qr_ref, scale), (k_ref, kr_ref, 1.0)):
            t = src[sl, :].astype(F32)
            partner = jnp.where(first, pltpu.roll(t, HEAD_DIM - HEAD_DIM // 4, 1), pltpu.roll(t, HEAD_DIM // 4, 1))
            dst[sl, :] = ((t * cos + partner * sin) * mul).astype(BF16)
        return carry

    lax.fori_loop(0, L // RC, rope_chunk, 0)

    kc = kc_ref[...]
    vc = vc_ref[...]
    nb = rows // NA_QROWS
    QB = NA_QROWS * W
    KB = NA_KROWS * W

    def block(i, carry):
        ks = jnp.clip(i * NA_QROWS - NA_ROWS // 2, 0, rows - NA_KROWS)
        case = jnp.where(i == 0, 0, jnp.where(i == nb - 1, 2, 1))
        qs = pl.ds(pl.multiple_of(i * QB, QB), QB)
        kslice = pl.ds(pl.multiple_of(ks * W, W), KB)
        qb = qr_ref[qs, :]
        s_loc = _dot_nt(qb, kr_ref[kslice, :]) + bias_ref[case]
        s_ctx = _dot_nt(qb, kc)
        m = jnp.maximum(jnp.max(s_loc, -1, keepdims=True), jnp.max(s_ctx, -1, keepdims=True))
        p_loc = jnp.exp(s_loc - m)
        p_ctx = jnp.exp(s_ctx - m)
        den = jnp.sum(p_loc, -1, keepdims=True) + jnp.sum(p_ctx, -1, keepdims=True)
        o = _dot(p_loc.astype(BF16), v_ref[kslice, :]) + _dot(p_ctx.astype(BF16), vc)
        ox_ref[qs, :] = (o / den).astype(ox_ref.dtype)
        return carry

    lax.fori_loop(0, nb, block, 0)

    qc = (qc_ref[...].astype(F32) * scale).astype(BF16)
    s = _dot_nt(qc, kc)
    p = jnp.exp(s - jnp.max(s, -1, keepdims=True))
    oc = _dot(p.astype(BF16), vc) / jnp.sum(p, -1, keepdims=True)
    oc_ref[...] = oc.astype(oc_ref.dtype)


def _rope_tables(L):
    pos = jnp.arange(L)
    half = HEAD_DIM // 2
    inv = ROPE_BASE ** (-jnp.arange(0, half, 2, dtype=F32) / half)

    def tab(p):
        ang = p.astype(F32)[:, None] * inv[None]
        cos, sin = jnp.cos(ang), jnp.sin(ang)
        return jnp.concatenate([cos, cos], -1), jnp.concatenate([-sin, sin], -1)

    c_r, s_r = tab(pos // GRID_W)
    c_c, s_c = tab(pos % GRID_W)
    return jnp.concatenate([c_r, c_c], -1), jnp.concatenate([s_r, s_c], -1)


def _na_call(px, rpb, B, L, Lc, col_q, col_k, col_v, n_heads):
    rows = L // GRID_W
    cos, sin = _rope_tables(L)
    ctx0 = B * L // Lc

    def lat(col):
        return pl.BlockSpec((L, HEAD_DIM), lambda b, h: (b, col + h))

    def cx(col):
        return pl.BlockSpec((Lc, HEAD_DIM), lambda b, h: (ctx0 + b, col + h))

    tab = pl.BlockSpec((L, HEAD_DIM), lambda b, h: (0, 0))
    W = GRID_W
    return pl.pallas_call(
        functools.partial(_na_kernel, rows=rows),
        grid=(B, n_heads),
        in_specs=[pl.BlockSpec(memory_space=pltpu.SMEM), lat(col_q), lat(col_k), lat(col_v),
                  cx(col_q), cx(col_k), cx(col_v), tab, tab],
        out_specs=[pl.BlockSpec((L, HEAD_DIM), lambda b, h: (b, h)),
                   pl.BlockSpec((Lc, HEAD_DIM), lambda b, h: (b, h))],
        out_shape=[jax.ShapeDtypeStruct((B * L, n_heads * HEAD_DIM), BF16),
                   jax.ShapeDtypeStruct((B * Lc, n_heads * HEAD_DIM), BF16)],
        scratch_shapes=[pltpu.VMEM((L, HEAD_DIM), BF16), pltpu.VMEM((L, HEAD_DIM), BF16),
                        pltpu.VMEM((2 * NA_ROWS - 1, W, W), F32),
                        pltpu.VMEM((3, NA_QROWS * W, NA_KROWS * W), F32)],
        compiler_params=_params(("parallel", "parallel")),
        name="neighborhood_attention",
    )(rpb, px, px, px, px, px, px, cos, sin)


POOL_TILE = 256
POOL_HALO = 16


def _pool_kernel(prev_ref, cur_ref, next_ref, w_ref, scale_ref, o_ref, *, n_lat_tiles, nl, nc):
    i = pl.program_id(0)
    is_lat = i < n_lat_tiles
    j = jnp.where(is_lat, i % nl, (i - n_lat_tiles) % nc)
    n = jnp.where(is_lat, nl, nc)
    TQ, HL = POOL_TILE, POOL_HALO
    lo_min = jnp.where(j > 0, -HL, 0)
    hi_max = jnp.where(j < n - 1, TQ + HL, TQ)
    src = jnp.concatenate([prev_ref[...], cur_ref[...], next_ref[...]], axis=0)
    t = lax.broadcasted_iota(jnp.int32, (TQ, TQ + 2 * HL), 0)
    s = lax.broadcasted_iota(jnp.int32, (TQ, TQ + 2 * HL), 1) - HL
    Dg = w_ref.shape[1]
    t1 = lax.broadcasted_iota(jnp.int32, (TQ, Dg), 0)
    for g, w in enumerate(POOL_WINDOWS):
        lo = jnp.maximum(t - w // 2, lo_min)
        hi = jnp.minimum(t + (w - w // 2), hi_max)
        band = jnp.where(s >= lo, jnp.where(s < hi, 1.0, 0.0), 0.0).astype(BF16)
        cnt = (jnp.minimum(t1 + (w - w // 2), hi_max) - jnp.maximum(t1 - w // 2, lo_min)).astype(F32)
        cols = slice(g * Dg, (g + 1) * Dg)
        win = _dot(band, src[:, cols])
        d = win / cnt - cur_ref[:, cols].astype(F32)
        y = _dot(d.astype(BF16), w_ref[g]) * scale_ref[:, cols]
        o_ref[:, cols] = y.astype(o_ref.dtype)


def _pool_call(px, pool_w_bf, pool_scale, B, L, Lc):
    T = px.shape[0]
    G, Dg, _ = pool_w_bf.shape
    PW = G * Dg
    TQ, HL = POOL_TILE, POOL_HALO
    r = TQ // HL
    n_tiles = T // TQ
    n_halo = T // HL
    kern = functools.partial(_pool_kernel, n_lat_tiles=B * L // TQ, nl=L // TQ, nc=Lc // TQ)
    return pl.pallas_call(
        kern,
        grid=(n_tiles,),
        in_specs=[pl.BlockSpec((HL, PW), lambda i: (jnp.maximum(i * r - 1, 0), 0)),
                  pl.BlockSpec((TQ, PW), lambda i: (i, 0)),
                  pl.BlockSpec((HL, PW), lambda i: (jnp.minimum((i + 1) * r, n_halo - 1), 0)),
                  pl.BlockSpec((G, Dg, Dg), lambda i: (0, 0, 0)),
                  pl.BlockSpec((1, PW), lambda i: (0, 0))],
        out_specs=pl.BlockSpec((TQ, PW), lambda i: (i, 0)),
        out_shape=jax.ShapeDtypeStruct((T, PW), BF16),
        compiler_params=_params(("parallel",)),
        name="multiscale_pool",
    )(px, px, px, pool_w_bf, pool_scale.reshape(1, PW).astype(F32))


def _readout_kernel(of_ref, ob_ref, gate_ref, ng_ref, o_ref):
    o = of_ref[...].astype(F32) + ob_ref[...].astype(F32)
    o = o * lax.rsqrt(jnp.mean(o * o, -1, keepdims=True) + LN_EPS)
    g = gate_ref[...].astype(F32)
    o_ref[...] = (o * ng_ref[...] * (g * _sigmoid(g))).astype(o_ref.dtype)


def _readout_call(o_f, o_b, px, norm_g, col_gate, tm):
    T, HW = o_f.shape
    blk = lambda col: pl.BlockSpec((tm, HEAD_DIM), lambda i, h: (i, col + h))
    return pl.pallas_call(
        _readout_kernel,
        grid=(T // tm, HW // HEAD_DIM),
        in_specs=[blk(0), blk(0), blk(col_gate), pl.BlockSpec((1, HEAD_DIM), lambda i, h: (0, h))],
        out_specs=blk(0),
        out_shape=jax.ShapeDtypeStruct((T, HW), BF16),
        compiler_params=_params(("parallel", "parallel")),
        name="hgrn_readout",
    )(o_f, o_b, px, norm_g.reshape(1, HW).astype(F32))


COND_ROWS = 16


def _adaln_kernel(cond_ref, w_ref, b_ref, o_ref):
    cnd = cond_ref[...]
    a = (cnd * _sigmoid(cnd)).astype(BF16)
    o_ref[0] = _dot(a, w_ref[0].astype(BF16)) + b_ref[0]


def _adaln_call(cond, w_mod, b_mod, tn=1024):
    depth, D, N = w_mod.shape
    return pl.pallas_call(
        _adaln_kernel,
        grid=(depth, N // tn),
        in_specs=[pl.BlockSpec((COND_ROWS, D), lambda l, j: (0, 0)),
                  pl.BlockSpec((1, D, tn), lambda l, j: (l, 0, j)),
                  pl.BlockSpec((1, 1, tn), lambda l, j: (l, 0, j))],
        out_specs=pl.BlockSpec((1, COND_ROWS, tn), lambda l, j: (l, 0, j)),
        out_shape=jax.ShapeDtypeStruct((depth, COND_ROWS, N), F32),
        compiler_params=_params(("parallel", "parallel")),
        name="adaln",
    )(cond, w_mod, b_mod.reshape(depth, 1, N))


def _cond_row(i, tm, n_lat_rows, L, B):
    return jnp.where(i * tm < n_lat_rows, (i * tm) // L, B)


def _layer_norm_rows(x):
    mu = jnp.mean(x, -1, keepdims=True)
    xc = x - mu
    var = jnp.mean(xc * xc, -1, keepdims=True)
    return xc * lax.rsqrt(var + LN_EPS)


def _ln_mod_kernel(x_ref, mod_ref, o_ref, *, tm, n_lat_rows, L, B, D):
    r = _cond_row(pl.program_id(0), tm, n_lat_rows, L, B)
    shift = mod_ref[pl.ds(r, 1), 0:D]
    scale = mod_ref[pl.ds(r, 1), D:2 * D]
    o_ref[...] = (_layer_norm_rows(x_ref[...]) * (1.0 + scale) + shift).astype(o_ref.dtype)


def _ln_mod_call(h, mod_l, B, L, tm=256):
    T, D = h.shape
    kern = functools.partial(_ln_mod_kernel, tm=tm, n_lat_rows=B * L, L=L, B=B, D=D)
    return pl.pallas_call(
        kern,
        grid=(T // tm,),
        in_specs=[pl.BlockSpec((tm, D), lambda i: (i, 0)), pl.BlockSpec(mod_l.shape, lambda i: (0, 0))],
        out_specs=pl.BlockSpec((tm, D), lambda i: (i, 0)),
        out_shape=jax.ShapeDtypeStruct((T, D), BF16),
        compiler_params=_params(("parallel",)),
        name="ln_modulate",
    )(h, mod_l)


def _wstat_matmul_kernel(*refs, widths):
    n = len(widths)
    a_refs, w_ref, o_ref, wbf_ref = refs[:n], refs[n], refs[n + 1], refs[n + 2]

    @pl.when(pl.program_id(1) == 0)
    def _():
        wbf_ref[...] = w_ref[0].astype(BF16)

    acc = None
    k0 = 0
    for a_ref, kw in zip(a_refs, widths):
        part = _dot(a_ref[...], wbf_ref[k0:k0 + kw, :])
        acc = part if acc is None else acc + part
        k0 += kw
    o_ref[...] = acc.astype(o_ref.dtype)


def _wstat_matmul_call(a_list, w_all, layer, n_rows, tm, tn, name):
    widths = tuple(a.shape[1] for a in a_list)
    _, K, N = w_all.shape
    assert sum(widths) == K and n_rows % tm == 0 and N % tn == 0
    return pl.pallas_call(
        functools.partial(_wstat_matmul_kernel, widths=widths),
        grid=(N // tn, n_rows // tm),
        in_specs=[pl.BlockSpec((tm, kw), lambda j, i: (i, 0)) for kw in widths]
        + [pl.BlockSpec((1, K, tn), lambda j, i: (layer, 0, j))],
        out_specs=pl.BlockSpec((tm, tn), lambda j, i: (i, j)),
        out_shape=jax.ShapeDtypeStruct((n_rows, N), BF16),
        scratch_shapes=[pltpu.VMEM((K, tn), BF16)],
        compiler_params=_params(("parallel", "arbitrary")),
        name=name,
    )(*a_list, w_all)


def _split_bf16(x):
    hi = x.astype(BF16)
    return hi, (x - hi.astype(F32)).astype(BF16)


def _post_mix_kernel(y_ref, h_ref, mod_ref, g_ref, b_ref, rw_ref, rb_ref,
                     h1_ref, fin_ref, gates_ref, topi_ref, *, tm, n_lat_rows, L, B, D, alpha):
    r = _cond_row(pl.program_id(0), tm, n_lat_rows, L, B)
    mrow = lambda slot: mod_ref[pl.ds(r, 1), slot * D:(slot + 1) * D]
    u = alpha * h_ref[...] + mrow(2) * y_ref[...].astype(F32)
    h1 = _layer_norm_rows(u) * g_ref[...] + b_ref[...]
    h1_ref[...] = h1
    fin = _layer_norm_rows(h1) * (1.0 + mrow(4)) + mrow(3)
    fin_ref[...] = fin.astype(BF16)
    f_hi, f_lo = _split_bf16(fin)
    w_hi, w_lo = _split_bf16(rw_ref[...])
    logits = _dot(f_hi, w_hi) + _dot(f_hi, w_lo) + _dot(f_lo, w_hi) + rb_ref[...]
    lane = lax.broadcasted_iota(jnp.int32, logits.shape, 1)
    ninf = jnp.float32(-jnp.inf)
    xs = jnp.where(lane < N_EXPERTS, logits, ninf)
    sel_v, sel_i = [], []
    for _ in range(TOP_K):
        m = jnp.max(xs, -1, keepdims=True)
        idx = jnp.min(jnp.where(xs == m, lane, logits.shape[1]), -1, keepdims=True)
        sel_v.append(m)
        sel_i.append(idx)
        xs = jnp.where(lane == idx, ninf, xs)
    ex = [jnp.exp(v - sel_v[0]) for v in sel_v]
    den = ex[0]
    for e in ex[1:]:
        den = den + e
    gates = jnp.zeros(logits.shape, F32)
    topi = jnp.zeros(logits.shape, jnp.int32)
    for j in range(TOP_K):
        gates = jnp.where(lane == sel_i[j], ex[j] / den, gates)
        topi = jnp.where(lane == j, sel_i[j], topi)
    gates_ref[...] = gates
    topi_ref[...] = topi


def _post_mix_call(y, h, mod_l, ln_g, ln_b, rw_pad, rb_pad, B, L, alpha, tm=256):
    n_out, D = y.shape
    NE = rw_pad.shape[1]
    kern = functools.partial(_post_mix_kernel, tm=tm, n_lat_rows=B * L, L=L, B=B, D=D, alpha=alpha)
    row = lambda i: (i, 0)
    fixed = lambda i: (0, 0)
    return pl.pallas_call(
        kern,
        grid=(n_out // tm,),
        in_specs=[pl.BlockSpec((tm, D), row), pl.BlockSpec((tm, D), row),
                  pl.BlockSpec(mod_l.shape, fixed),
                  pl.BlockSpec((1, D), fixed), pl.BlockSpec((1, D), fixed),
                  pl.BlockSpec((D, NE), fixed), pl.BlockSpec((1, NE), fixed)],
        out_specs=[pl.BlockSpec((tm, D), row), pl.BlockSpec((tm, D), row),
                   pl.BlockSpec((tm, NE), row), pl.BlockSpec((tm, NE), row)],
        out_shape=[jax.ShapeDtypeStruct((n_out, D), F32), jax.ShapeDtypeStruct((n_out, D), BF16),
                   jax.ShapeDtypeStruct((n_out, NE), F32), jax.ShapeDtypeStruct((n_out, NE), jnp.int32)],
        compiler_params=_params(("parallel",)),
        name="post_mix_router",
    )(y, h, mod_l, ln_g.reshape(1, D), ln_b.reshape(1, D), rw_pad, rb_pad)


MOE_TILE = 512


def _visit_flags(v, vt_ref, ve_ref):
    p = jnp.maximum(v - 1, 0)
    return (v == 0) | (vt_ref[p] != vt_ref[v]), (v == 0) | (ve_ref[p] != ve_ref[v])


def _visit_row_mask(v, vt_ref, ve_ref, offs_ref, shape):
    e = ve_ref[v]
    row = lax.broadcasted_iota(jnp.int32, shape, 0) + vt_ref[v] * shape[0]
    return jnp.logical_and(row >= offs_ref[e], row < offs_ref[e + 1])


def _masked_tile_store(o_ref, val, mask, first_tile):
    @pl.when(first_tile)
    def _():
        o_ref[...] = jnp.where(mask, val, 0.0).astype(o_ref.dtype)

    @pl.when(jnp.logical_not(first_tile))
    def _():
        o_ref[...] = jnp.where(mask, val, o_ref[...].astype(F32)).astype(o_ref.dtype)


def _moe_up_kernel(vt_ref, ve_ref, vv_ref, offs_ref, x_ref, w1_ref, b1_ref, o_ref, wbf_ref, *, DE):
    v = pl.program_id(0)
    first_tile, first_expert = _visit_flags(v, vt_ref, ve_ref)

    @pl.when(first_expert)
    def _():
        wbf_ref[...] = w1_ref[0, 0].astype(BF16)

    @pl.when(vv_ref[v] > 0)
    def _():
        hcat = _dot(x_ref[...], wbf_ref[...]) + b1_ref[0, 0]
        gate = jnp.minimum(hcat[:, :DE], SWIGLU_LIMIT)
        up = jnp.clip(hcat[:, DE:], -SWIGLU_LIMIT, SWIGLU_LIMIT)
        hdn = (up + 1.0) * (gate * _sigmoid(SWIGLU_ALPHA * gate))
        mask = _visit_row_mask(v, vt_ref, ve_ref, offs_ref, hdn.shape)
        _masked_tile_store(o_ref, hdn, mask, first_tile)


def _moe_down_kernel(vt_ref, ve_ref, vv_ref, offs_ref, h_ref, rw_ref, w2_ref, b2_ref, o_ref, wbf_ref):
    v = pl.program_id(0)
    first_tile, first_expert = _visit_flags(v, vt_ref, ve_ref)

    @pl.when(first_expert)
    def _():
        wbf_ref[...] = w2_ref[0, 0].astype(BF16)

    @pl.when(vv_ref[v] > 0)
    def _():
        y = (_dot(h_ref[...], wbf_ref[...]) + b2_ref[0, 0]) * rw_ref[...]
        mask = _visit_row_mask(v, vt_ref, ve_ref, offs_ref, y.shape)
        _masked_tile_store(o_ref, y, mask, first_tile)


def _moe_call(xs, row_w, visits, w1_all, b1_all, w2_all, b2_all, layer):
    n, D = xs.shape
    _, E, _, DE2 = w1_all.shape
    DE = DE2 // 2
    tm = MOE_TILE
    nv = visits[0].shape[0]
    tile = lambda v, vt, ve, vv, offs: (vt[v], 0)
    expert = lambda v, vt, ve, vv, offs: (layer, ve[v], 0, 0)
    hdn = pl.pallas_call(
        functools.partial(_moe_up_kernel, DE=DE),
        grid_spec=pltpu.PrefetchScalarGridSpec(
            num_scalar_prefetch=4, grid=(nv,),
            in_specs=[pl.BlockSpec((tm, D), tile),
                      pl.BlockSpec((1, 1, D, DE2), expert),
                      pl.BlockSpec((1, 1, 1, DE2), expert)],
            out_specs=pl.BlockSpec((tm, DE), tile),
            scratch_shapes=[pltpu.VMEM((D, DE2), BF16)]),
        out_shape=jax.ShapeDtypeStruct((n, DE), BF16),
        compiler_params=_params(("arbitrary",)),
        name="moe_up",
    )(*visits, xs, w1_all, b1_all.reshape(b1_all.shape[0], E, 1, DE2))
    return pl.pallas_call(
        _moe_down_kernel,
        grid_spec=pltpu.PrefetchScalarGridSpec(
            num_scalar_prefetch=4, grid=(nv,),
            in_specs=[pl.BlockSpec((tm, DE), tile),
                      pl.BlockSpec((tm, 1), tile),
                      pl.BlockSpec((1, 1, DE, D), expert),
                      pl.BlockSpec((1, 1, 1, D), expert)],
            out_specs=pl.BlockSpec((tm, D), tile),
            scratch_shapes=[pltpu.VMEM((DE, D), BF16)]),
        out_shape=jax.ShapeDtypeStruct((n, D), BF16),
        compiler_params=_params(("arbitrary",)),
        name="moe_down",
    )(*visits, hdn, row_w, w2_all, b2_all.reshape(b2_all.shape[0], E, 1, D))


def _moe_dispatch(topi, gates, tm):
    T = topi.shape[0]
    E = N_EXPERTS
    n = T * TOP_K
    n_tiles = n // tm
    flat_e = topi.reshape(-1)
    flat_w = jnp.take_along_axis(gates, topi, axis=1).reshape(-1)
    order = jnp.argsort(flat_e, stable=True).astype(jnp.int32)
    inv = jnp.argsort(order).astype(jnp.int32)
    sorted_tok = order // TOP_K
    sorted_w = flat_w[order].reshape(n, 1)
    pos_kmajor = inv.reshape(T, TOP_K).T.reshape(-1)
    counts = jnp.sum((flat_e[:, None] == jnp.arange(E, dtype=jnp.int32)[None, :]).astype(jnp.int32), axis=0)
    ends = jnp.cumsum(counts)
    offs = jnp.concatenate([jnp.zeros((1,), jnp.int32), ends]).astype(jnp.int32)
    first_tile = offs[:-1] // tm
    n_vis = jnp.where(counts > 0, (ends - 1) // tm - first_tile + 1, 0)
    v_end = jnp.cumsum(n_vis)
    v_start = v_end - n_vis
    total = v_end[-1]
    nv = n_tiles + E - 1
    v = jnp.arange(nv, dtype=jnp.int32)
    ve = jnp.minimum(jnp.sum((v[:, None] >= v_end[None, :]).astype(jnp.int32), axis=1), E - 1)
    vt = first_tile[ve] + v - v_start[ve]
    valid = v < total
    ve = jnp.where(valid, ve, ve[total - 1]).astype(jnp.int32)
    vt = jnp.where(valid, vt, n_tiles - 1).astype(jnp.int32)
    return sorted_tok, sorted_w, pos_kmajor, (vt, ve, valid.astype(jnp.int32), offs)


def _final_kernel(h_ref, *refs, tm, n_lat_rows, L, B, D, alpha, with_next):
    y_refs, rest = refs[:TOP_K], refs[TOP_K:]
    r = _cond_row(pl.program_id(0), tm, n_lat_rows, L, B)
    gate = rest[0][pl.ds(r, 1), 5 * D:6 * D]
    f = y_refs[0][...].astype(F32)
    for y_ref in y_refs[1:]:
        f = f + y_ref[...].astype(F32)
    u = alpha * h_ref[...] + gate * f
    h2 = _layer_norm_rows(u) * rest[1][...] + rest[2][...]
    if with_next:
        modn_ref, o_ref, a_ref = rest[3:]
        a_ref[...] = (_layer_norm_rows(h2) * (1.0 + modn_ref[pl.ds(r, 1), D:2 * D])
                      + modn_ref[pl.ds(r, 1), 0:D]).astype(a_ref.dtype)
    else:
        o_ref = rest[3]
    o_ref[...] = h2


def _final_call(h1, y_assign, mod_l, ln_g, ln_b, B, L, alpha, mod_next=None, tm=256):
    n, D = h1.shape
    nt = n // tm
    with_next = mod_next is not None
    kern = functools.partial(_final_kernel, tm=tm, n_lat_rows=B * L, L=L, B=B, D=D, alpha=alpha,
                             with_next=with_next)
    row = lambda i: (i, 0)
    fixed = lambda i: (0, 0)
    y_specs = [pl.BlockSpec((tm, D), functools.partial(lambda i, k: (k * nt + i, 0), k=k)) for k in range(TOP_K)]
    mod_spec = pl.BlockSpec(mod_l.shape, fixed)
    row_spec = pl.BlockSpec((tm, D), row)
    return pl.pallas_call(
        kern,
        grid=(nt,),
        in_specs=[row_spec] + y_specs + [mod_spec, pl.BlockSpec((1, D), fixed), pl.BlockSpec((1, D), fixed)]
        + ([mod_spec] if with_next else []),
        out_specs=[row_spec, row_spec] if with_next else row_spec,
        out_shape=([jax.ShapeDtypeStruct((n, D), F32), jax.ShapeDtypeStruct((n, D), BF16)] if with_next
                   else jax.ShapeDtypeStruct((n, D), F32)),
        compiler_params=_params(("parallel",)),
        name="final_ln",
    )(h1, *([y_assign] * TOP_K), mod_l, ln_g.reshape(1, D), ln_b.reshape(1, D),
      *([mod_next] if with_next else []))


def kernel(x, c, ctx, c_ctx, w_mod, b_mod, w_in, pool_w, pool_scale, na_rpb, hg_lb, hg_norm_g, w_out, ln1_g, ln1_b, ln2_g, ln2_b, router_w, router_b, exp_w1, exp_b1, exp_w2, exp_b2):
    B, L, D = x.shape
    Lc = ctx.shape[1]
    depth = w_in.shape[0]
    alpha = (2 * depth) ** 0.25
    pool_width = pool_w.shape[1] * pool_w.shape[2]
    na_width = na_rpb.shape[1] * HEAD_DIM
    hg_width = hg_norm_g.shape[1]
    na_heads, hg_heads = na_width // HEAD_DIM, hg_width // HEAD_DIM
    c_naq = pool_width // HEAD_DIM
    c_nak, c_nav = c_naq + na_heads, c_naq + 2 * na_heads
    c_hq = c_naq + 3 * na_heads
    c_hff, c_hfb, c_hi, c_hg = (c_hq + k * hg_heads for k in range(1, 5))
    assert B + 1 <= COND_ROWS

    hall = jnp.concatenate([x.reshape(B * L, D), ctx.reshape(B * Lc, D)], axis=0)
    cond = jnp.concatenate([c, c_ctx[None], jnp.zeros((COND_ROWS - B - 1, D), F32)], axis=0)
    mod = _adaln_call(cond, w_mod, b_mod)
    lb_soft = jax.nn.softmax(hg_lb.astype(F32), axis=1)
    lower = jnp.cumsum(lb_soft, axis=1) - lb_soft[:, :1]
    rw_pad = jnp.pad(router_w, ((0, 0), (0, 0), (0, HEAD_DIM - N_EXPERTS)))
    rb_pad = jnp.pad(router_b, ((0, 0), (0, HEAD_DIM - N_EXPERTS)))

    T = hall.shape[0]
    n_row_tiles = 8
    a_in = _ln_mod_call(hall, mod[0], B, L)
    for l in range(depth):
        last = l == depth - 1
        px =_wstat_matmul_call([a_in], w_in, l, T, T // n_row_tiles, 512, "in_proj")
        a_mix = _pool_call(px, pool_w[l].astype(BF16), pool_scale[l], B, L, Lc)
        b_x, b_c = _na_call(px, na_rpb[l], B, L, Lc, c_naq, c_nak, c_nav, na_heads)
        b_mix = jnp.concatenate([b_x, b_c], axis=0)
        o_f, o_b = _hgrn_call(px, lower[0, l].reshape(hg_heads, 1, HEAD_DIM),
                              lower[1, l].reshape(hg_heads, 1, HEAD_DIM), B, L, Lc,
                              c_hq, c_hff, c_hfb, c_hi, hg_heads)
        c_mix = _readout_call(o_f, o_b, px, hg_norm_g[l], c_hg, 512)
        n_out = B * L if last else T
        y = _wstat_matmul_call([a_mix, b_mix, c_mix], w_out, l, n_out, n_out // n_row_tiles, 512, "out_proj")
        h1, fin, gates, topi = _post_mix_call(y, hall, mod[l], ln1_g[l], ln1_b[l], rw_pad[l],
                                              rb_pad[l].reshape(1, -1), B, L, alpha)
        sorted_tok, sorted_w, pos_kmajor, visits = _moe_dispatch(topi[:, :TOP_K], gates, MOE_TILE)
        xs = fin.at[sorted_tok].get(mode="promise_in_bounds")
        ys = _moe_call(xs, sorted_w, visits, exp_w1, exp_b1, exp_w2, exp_b2, l)
        y_assign = ys.at[pos_kmajor].get(mode="promise_in_bounds")
        if last:
            hall = _final_call(h1, y_assign, mod[l], ln2_g[l], ln2_b[l], B, L, alpha)
        else:
            hall, a_in = _final_call(h1, y_assign, mod[l], ln2_g[l], ln2_b[l], B, L, alpha, mod_next=mod[l + 1])
    return hall.reshape(B, L, D)
```

```python
import functools

import jax
import jax.numpy as jnp
import numpy as np
from jax import lax
from jax.experimental import pallas as pl
from jax.experimental.pallas import tpu as pltpu

F32 = jnp.float32
BF16 = jnp.bfloat16

HEAD_DIM = 128
GRID_W = 64
POOL_WINDOWS = (2, 4, 8, 16)
NA_ROWS = 8
NA_COLS = 16
FORGET_EPS = 1e-20
N_EXPERTS = 32
TOP_K = 4
SWIGLU_LIMIT = 7.0
SWIGLU_ALPHA = 1.702
ROPE_BASE = 10000.0
LN_EPS = 1e-6

VMEM_LIMIT_BYTES = 56 * 1024 * 1024
HG_CHUNK = 256
NEG_BIG = -1e30


def _params(sem):
    return pltpu.CompilerParams(dimension_semantics=sem, vmem_limit_bytes=VMEM_LIMIT_BYTES)


def _dot(a, b):
    return jnp.dot(a, b, preferred_element_type=F32)


def _dot_nt(a, b):
    return lax.dot_general(a, b, (((1,), (1,)), ((), ())), preferred_element_type=F32)


def _sigmoid(x):
    return 1.0 / (1.0 + jnp.exp(-x))


LOG2E = 1.4426950408889634
HG_HALF = HG_CHUNK // 2
HG_HEADS_PER_STEP = 2


def _neg_abs(x):
    bits = lax.bitcast_convert_type(x, jnp.uint32) | jnp.uint32(0x80000000)
    return lax.bitcast_convert_type(bits, F32)


def _hgrn_half_scores(q, k, b2, lev, reverse):
    n = q.shape[0]
    row = lax.broadcasted_iota(jnp.int32, (n, HEAD_DIM), 0)
    a = jnp.where(lev == -1, _dot_nt(q.astype(BF16), k.astype(BF16)), 0.0)
    h, lvl = 1, 0
    while h < n:
        if h < 8:
            upper = (row & h) != 0
            t_role = jnp.logical_not(upper) if reverse else upper
            qk = jnp.where(t_role, q, k)
            if h == 1:
                ref = jnp.where(t_role, pltpu.roll(b2, n - 1 if reverse else 1, 0), b2)
            elif h == 2:
                r4 = row & 3
                if reverse:
                    ref = jnp.where(r4 == 0, pltpu.roll(b2, n - 2, 0),
                                    jnp.where(r4 == 1, pltpu.roll(b2, n - 1, 0),
                                              jnp.where(r4 == 2, b2, pltpu.roll(b2, 1, 0))))
                else:
                    ref = jnp.where(r4 == 0, pltpu.roll(b2, n - 1, 0),
                                    jnp.where(r4 == 1, b2,
                                              jnp.where(r4 == 2, pltpu.roll(b2, 1, 0), pltpu.roll(b2, 2, 0))))
            else:
                m0 = h if reverse else h - 1
                b3 = b2.reshape(n // (2 * h), 2 * h, HEAD_DIM)
                ref = jnp.broadcast_to(b3[:, m0:m0 + 1, :], b3.shape).reshape(n, HEAD_DIM)
            x = (qk * jnp.exp2(_neg_abs(b2 - ref))).astype(BF16)
            a = jnp.where(lev == lvl, _dot_nt(x, x), a)
        else:
            nb = n // (2 * h)
            q4 = q.reshape(nb, 2, h, HEAD_DIM)
            k4 = k.reshape(nb, 2, h, HEAD_DIM)
            b4 = b2.reshape(nb, 2, h, HEAD_DIM)
            s_half, t_half, m_row = (1, 0, 0) if reverse else (0, 1, h - 1)
            ref = b4[:, s_half:s_half + 1, m_row:m_row + 1, :]
            xt = q4[:, t_half] * jnp.exp2(b4[:, t_half] - ref[:, 0])
            xs = k4[:, s_half] * jnp.exp2(ref[:, 0] - b4[:, s_half])
            halves = [xt, xs] if reverse else [xs, xt]
            x = jnp.stack(halves, axis=1).reshape(n, HEAD_DIM).astype(BF16)
            g = _dot_nt(xt.reshape(n // 2, HEAD_DIM).astype(BF16), x)
            a4 = a.reshape(nb, 2, h, n)
            lev_t = lev.reshape(nb, 2, h, n)[:, t_half]
            a_t = jnp.where(lev_t == lvl, g.reshape(nb, h, n), a4[:, t_half])
            halves = [a_t, a4[:, s_half]] if reverse else [a4[:, s_half], a_t]
            a = jnp.stack(halves, axis=1).reshape(n, n)
        h *= 2
        lvl += 1
    return a


def _hgrn_direction(q, z, v_bf, lb, st_ref, lev, tri_bf, reverse):
    C, H = HG_CHUNK, HG_HALF
    q = q * _sigmoid(q)
    sig = _sigmoid(z)
    f = lb + (1.0 - lb) * sig
    g2 = jnp.log(jnp.maximum(f, FORGET_EPS)) * LOG2E
    k = (1.0 - lb) * (1.0 - sig)

    g_hi = g2.astype(BF16)
    r1 = g2 - g_hi.astype(F32)
    g_mid = r1.astype(BF16)
    g_lo = (r1 - g_mid.astype(F32)).astype(BF16)
    b2 = _dot(tri_bf, g_hi) + _dot(tri_bf, g_mid) + _dot(tri_bf, g_lo)

    lo, hi = slice(0, H), slice(H, C)
    a_lo = _hgrn_half_scores(q[lo], k[lo], b2[lo], lev, reverse).astype(BF16)
    a_hi = _hgrn_half_scores(q[hi], k[hi], b2[hi], lev, reverse).astype(BF16)
    t_sl, s_sl, m = (lo, hi, H) if reverse else (hi, lo, H - 1)
    ref = b2[m:m + 1, :]
    qt = (q[t_sl] * jnp.exp2(b2[t_sl] - ref)).astype(BF16)
    ks = (k[s_sl] * jnp.exp2(ref - b2[s_sl])).astype(BF16)
    cross = _dot(_dot_nt(qt, ks).astype(BF16), v_bf[s_sl])
    o_lo = _dot(a_lo, v_bf[lo])
    o_hi = _dot(a_hi, v_bf[hi])
    o_intra = jnp.concatenate([o_lo + cross, o_hi] if reverse else [o_lo, o_hi + cross], axis=0)

    st = st_ref[...]
    tot = b2[0:1, :] if reverse else b2[C - 1:C, :]
    qdec = (q * jnp.exp2(b2)).astype(BF16)
    o_inter = _dot_nt(qdec, st.astype(BF16))
    kdec = (k * jnp.exp2(tot - b2)).astype(BF16)
    vt = v_bf.astype(F32).T.astype(BF16)
    st_ref[...] = jnp.exp2(tot) * st + _dot(vt, kdec)
    return o_intra + o_inter


def _hgrn_kernel(qf_ref, zf_ref, vf_ref, qb_ref, zb_ref, vb_ref, lbf_ref, lbb_ref, levf_ref, levb_ref,
                 trif_ref, trib_ref, of_ref, ob_ref, sf_ref, sb_ref):
    @pl.when(pl.program_id(2) == 0)
    def _():
        sf_ref[...] = jnp.zeros_like(sf_ref)
        sb_ref[...] = jnp.zeros_like(sb_ref)

    for hh in range(HG_HEADS_PER_STEP):
        sl = slice(hh * HEAD_DIM, (hh + 1) * HEAD_DIM)
        of_ref[:, sl] = _hgrn_direction(qf_ref[:, sl].astype(F32), zf_ref[:, sl].astype(F32), vf_ref[:, sl],
                                        lbf_ref[0][:, sl], sf_ref.at[hh], levf_ref[...], trif_ref[...],
                                        False).astype(of_ref.dtype)
        ob_ref[:, sl] = _hgrn_direction(qb_ref[:, sl].astype(F32), zb_ref[:, sl].astype(F32), vb_ref[:, sl],
                                        lbb_ref[0][:, sl], sb_ref.at[hh], levb_ref[...], trib_ref[...],
                                        True).astype(ob_ref.dtype)


def _hgrn_call(px, lb_f, lb_b, B, L, Lc, col_q, col_ff, col_fb, col_i, n_heads):
    C, H = HG_CHUNK, HG_HALF
    T = px.shape[0]
    nlc, ncc = L // C, Lc // C
    ns = nlc + ncc
    ctx0 = B * nlc

    def fwd_row(b, s):
        return jnp.where(s < ncc, ctx0 + b * ncc + s, b * nlc + s - ncc)

    def bwd_row(b, s):
        return jnp.where(s < ncc, ctx0 + b * ncc + (ncc - 1 - s), b * nlc + (nlc - 1 - (s - ncc)))

    G = HG_HEADS_PER_STEP
    GW = G * HEAD_DIM
    assert n_heads % G == 0 and all(c % G == 0 for c in (col_q, col_ff, col_fb, col_i))

    def spec(row_fn, col):
        return pl.BlockSpec((C, GW), lambda b, h, s: (row_fn(b, s), col // G + h))

    idx = np.arange(H)
    xr = idx[:, None] ^ idx[None, :]
    level = np.floor(np.log2(np.maximum(xr, 1))).astype(np.int32)
    lev_f = np.where(xr == 0, -1, np.where(idx[:, None] > idx[None, :], level, -2)).astype(np.int32)
    lev_b = np.ascontiguousarray(lev_f.T)
    idc = np.arange(C)
    tri_f = (idc[:, None] >= idc[None, :]).astype(np.float32)
    tri_b = (idc[:, None] <= idc[None, :]).astype(np.float32)
    const = lambda shape: pl.BlockSpec(shape, lambda b, h, s: (0,) * len(shape))
    lb_spec = pl.BlockSpec((1, 1, GW), lambda b, h, s: (h, 0, 0))
    out_sds = jax.ShapeDtypeStruct((T, n_heads * HEAD_DIM), BF16)
    state = pltpu.VMEM((G, HEAD_DIM, HEAD_DIM), F32)
    return pl.pallas_call(
        _hgrn_kernel,
        grid=(B, n_heads // G, ns),
        in_specs=[spec(fwd_row, col_q), spec(fwd_row, col_ff), spec(fwd_row, col_i),
                  spec(bwd_row, col_q), spec(bwd_row, col_fb), spec(bwd_row, col_i),
                  lb_spec, lb_spec, const((H, H)), const((H, H)), const((C, C)), const((C, C))],
        out_specs=[pl.BlockSpec((C, GW), lambda b, h, s: (fwd_row(b, s), h)),
                   pl.BlockSpec((C, GW), lambda b, h, s: (bwd_row(b, s), h))],
        out_shape=[out_sds, out_sds],
        scratch_shapes=[state, state],
        compiler_params=_params(("parallel", "parallel", "arbitrary")),
        name="hgrn_scan",
    )(px, px, px, px, px, px, lb_f.reshape(n_heads // G, 1, GW), lb_b.reshape(n_heads // G, 1, GW),
      jnp.asarray(lev_f), jnp.asarray(lev_b), jnp.asarray(tri_f, BF16), jnp.asarray(tri_b, BF16))


NA_QROWS = 4
NA_KROWS = NA_QROWS + NA_ROWS


def _na_block_cases(rows):
    assert rows % NA_QROWS == 0 and rows >= 2 * NA_ROWS
    nb = rows // NA_QROWS
    cases = []
    for i in (0, 1, nb - 1):
        r0 = NA_QROWS * i
        ks = min(max(r0 - NA_ROWS // 2, 0), rows - NA_KROWS)
        per_q = []
        for qr in range(NA_QROWS):
            r = r0 + qr
            rs = min(max(r - NA_ROWS // 2, 0), rows - NA_ROWS)
            per_q.append((rs - ks, ks - r + NA_ROWS - 1))
        cases.append(per_q)
    return cases


def _na_kernel(rpb_ref, q_ref, k_ref, v_ref, qc_ref, kc_ref, vc_ref, cos_ref, sin_ref,
               ox_ref, oc_ref, qr_ref, kr_ref, tb_ref, bias_ref, *, rows):
    h = pl.program_id(0)
    L = q_ref.shape[0]
    W = GRID_W
    scale = HEAD_DIM ** -0.5
    n_dr = 2 * NA_ROWS - 1
    n_dc = 2 * NA_COLS - 1

    @pl.when(pl.program_id(1) == 0)
    def _():
        c_i = lax.broadcasted_iota(jnp.int32, (W, W), 0)
        kc_i = lax.broadcasted_iota(jnp.int32, (W, W), 1)
        cs = jnp.clip(c_i - NA_COLS // 2, 0, W - NA_COLS)
        in_win = (kc_i >= cs) & (kc_i < cs + NA_COLS)
        c_off = kc_i - c_i + (NA_COLS - 1)
        for dr in range(n_dr):
            acc = jnp.zeros((W, W), F32)
            for j in range(n_dc):
                acc = jnp.where(c_off == j, rpb_ref[h, dr, j], acc)
            tb_ref[dr] = jnp.where(in_win, acc, NEG_BIG)
        neg = jnp.full((W, W), NEG_BIG, F32)
        for ci, per_q in enumerate(_na_block_cases(rows)):
            for qr, (j0, dr0) in enumerate(per_q):
                for j in range(NA_KROWS):
                    valid = j0 <= j < j0 + NA_ROWS
                    blk = tb_ref[dr0 + j] if valid else neg
                    bias_ref[ci, qr * W:(qr + 1) * W, j * W:(j + 1) * W] = blk

    RC = 256
    lane = lax.broadcasted_iota(jnp.int32, (RC, HEAD_DIM), 1)
    first = (lane & (HEAD_DIM // 2 - 1)) < HEAD_DIM // 4

    def rope_chunk(i, carry):
        sl = pl.ds(pl.multiple_of(i * RC, RC), RC)
        cos = cos_ref[sl, :]
        sin = sin_ref[sl, :]
        for src, dst, mul in ((q_ref, qr_ref, scale), (k_ref, kr_ref, 1.0)):
            t = src[sl, :].astype(F32)
            partner = jnp.where(first, pltpu.roll(t, HEAD_DIM - HEAD_DIM // 4, 1), pltpu.roll(t, HEAD_DIM // 4, 1))
            dst[sl, :] = ((t * cos + partner * sin) * mul).astype(BF16)
        return carry

    lax.fori_loop(0, L // RC, rope_chunk, 0)

    kc = kc_ref[...]
    vc = vc_ref[...]
    nb = rows // NA_QROWS
    QB = NA_QROWS * W
    KB = NA_KROWS * W

    def block(i, carry):
        ks = jnp.clip(i * NA_QROWS - NA_ROWS // 2, 0, rows - NA_KROWS)
        case = jnp.where(i == 0, 0, jnp.where(i == nb - 1, 2, 1))
        qs = pl.ds(pl.multiple_of(i * QB, QB), QB)
        kslice = pl.ds(pl.multiple_of(ks * W, W), KB)
        qb = qr_ref[qs, :]
        s_loc = _dot_nt(qb, kr_ref[kslice, :]) + bias_ref[case]
        s_ctx = _dot_nt(qb, kc)
        m = jnp.maximum(jnp.max(s_loc, -1, keepdims=True), jnp.max(s_ctx, -1, keepdims=True))
        p_loc = jnp.exp(s_loc - m)
        p_ctx = jnp.exp(s_ctx - m)
        den = jnp.sum(p_loc, -1, keepdims=True) + jnp.sum(p_ctx, -1, keepdims=True)
        o = _dot(p_loc.astype(BF16), v_ref[kslice, :]) + _dot(p_ctx.astype(BF16), vc)
        ox_ref[qs, :] = (o / den).astype(ox_ref.dtype)
        return carry

    lax.fori_loop(0, nb, block, 0, unroll=2)

    qc = (qc_ref[...].astype(F32) * scale).astype(BF16)
    s = _dot_nt(qc, kc)
    p = jnp.exp(s - jnp.max(s, -1, keepdims=True))
    oc = _dot(p.astype(BF16), vc) / jnp.sum(p, -1, keepdims=True)
    oc_ref[...] = oc.astype(oc_ref.dtype)


def _rope_tables(L):
    pos = jnp.arange(L)
    half = HEAD_DIM // 2
    inv = ROPE_BASE ** (-jnp.arange(0, half, 2, dtype=F32) / half)

    def tab(p):
        ang = p.astype(F32)[:, None] * inv[None]
        cos, sin = jnp.cos(ang), jnp.sin(ang)
        return jnp.concatenate([cos, cos], -1), jnp.concatenate([-sin, sin], -1)

    c_r, s_r = tab(pos // GRID_W)
    c_c, s_c = tab(pos % GRID_W)
    return jnp.concatenate([c_r, c_c], -1), jnp.concatenate([s_r, s_c], -1)


def _na_call(px, rpb, B, L, Lc, col_q, col_k, col_v, n_heads):
    rows = L // GRID_W
    cos, sin = _rope_tables(L)
    ctx0 = B * L // Lc

    def lat(col):
        return pl.BlockSpec((L, HEAD_DIM), lambda h, b: (b, col + h))

    def cx(col):
        return pl.BlockSpec((Lc, HEAD_DIM), lambda h, b: (ctx0 + b, col + h))

    tab = pl.BlockSpec((L, HEAD_DIM), lambda h, b: (0, 0))
    W = GRID_W
    return pl.pallas_call(
        functools.partial(_na_kernel, rows=rows),
        grid=(n_heads, B),
        in_specs=[pl.BlockSpec(memory_space=pltpu.SMEM), lat(col_q), lat(col_k), lat(col_v),
                  cx(col_q), cx(col_k), cx(col_v), tab, tab],
        out_specs=[pl.BlockSpec((L, HEAD_DIM), lambda h, b: (b, h)),
                   pl.BlockSpec((Lc, HEAD_DIM), lambda h, b: (b, h))],
        out_shape=[jax.ShapeDtypeStruct((B * L, n_heads * HEAD_DIM), BF16),
                   jax.ShapeDtypeStruct((B * Lc, n_heads * HEAD_DIM), BF16)],
        scratch_shapes=[pltpu.VMEM((L, HEAD_DIM), BF16), pltpu.VMEM((L, HEAD_DIM), BF16),
                        pltpu.VMEM((2 * NA_ROWS - 1, W, W), F32),
                        pltpu.VMEM((3, NA_QROWS * W, NA_KROWS * W), F32)],
        compiler_params=_params(("parallel", "arbitrary")),
        name="neighborhood_attention",
    )(rpb, px, px, px, px, px, px, cos, sin)


POOL_TILE = 256
POOL_HALO = 16


def _pool_kernel(prev_ref, cur_ref, next_ref, w_ref, scale_ref, o_ref, *, n_lat_tiles, nl, nc):
    i = pl.program_id(0)
    is_lat = i < n_lat_tiles
    j = jnp.where(is_lat, i % nl, (i - n_lat_tiles) % nc)
    n = jnp.where(is_lat, nl, nc)
    TQ, HL = POOL_TILE, POOL_HALO
    lo_min = jnp.where(j > 0, -HL, 0)
    hi_max = jnp.where(j < n - 1, TQ + HL, TQ)
    src = jnp.concatenate([prev_ref[...], cur_ref[...], next_ref[...]], axis=0)
    t = lax.broadcasted_iota(jnp.int32, (TQ, TQ + 2 * HL), 0)
    s = lax.broadcasted_iota(jnp.int32, (TQ, TQ + 2 * HL), 1) - HL
    Dg = w_ref.shape[1]
    t1 = lax.broadcasted_iota(jnp.int32, (TQ, Dg), 0)
    for g, w in enumerate(POOL_WINDOWS):
        lo = jnp.maximum(t - w // 2, lo_min)
        hi = jnp.minimum(t + (w - w // 2), hi_max)
        band = jnp.where(s >= lo, jnp.where(s < hi, 1.0, 0.0), 0.0).astype(BF16)
        cnt = (jnp.minimum(t1 + (w - w // 2), hi_max) - jnp.maximum(t1 - w // 2, lo_min)).astype(F32)
        cols = slice(g * Dg, (g + 1) * Dg)
        win = _dot(band, src[:, cols])
        d = win / cnt - cur_ref[:, cols].astype(F32)
        y = _dot(d.astype(BF16), w_ref[g]) * scale_ref[:, cols]
        o_ref[:, cols] = y.astype(o_ref.dtype)


def _pool_call(px, pool_w_bf, pool_scale, B, L, Lc):
    T = px.shape[0]
    G, Dg, _ = pool_w_bf.shape
    PW = G * Dg
    TQ, HL = POOL_TILE, POOL_HALO
    r = TQ // HL
    n_tiles = T // TQ
    n_halo = T // HL
    kern = functools.partial(_pool_kernel, n_lat_tiles=B * L // TQ, nl=L // TQ, nc=Lc // TQ)
    return pl.pallas_call(
        kern,
        grid=(n_tiles,),
        in_specs=[pl.BlockSpec((HL, PW), lambda i: (jnp.maximum(i * r - 1, 0), 0)),
                  pl.BlockSpec((TQ, PW), lambda i: (i, 0)),
                  pl.BlockSpec((HL, PW), lambda i: (jnp.minimum((i + 1) * r, n_halo - 1), 0)),
                  pl.BlockSpec((G, Dg, Dg), lambda i: (0, 0, 0)),
                  pl.BlockSpec((1, PW), lambda i: (0, 0))],
        out_specs=pl.BlockSpec((TQ, PW), lambda i: (i, 0)),
        out_shape=jax.ShapeDtypeStruct((T, PW), BF16),
        compiler_params=_params(("parallel",)),
        name="multiscale_pool",
    )(px, px, px, pool_w_bf, pool_scale.reshape(1, PW).astype(F32))


READOUT_HEADS = 4


def _readout_kernel(of_ref, ob_ref, gate_ref, ng_ref, o_ref):
    for hh in range(READOUT_HEADS):
        sl = slice(hh * HEAD_DIM, (hh + 1) * HEAD_DIM)
        o = of_ref[:, sl].astype(F32) + ob_ref[:, sl].astype(F32)
        o = o * lax.rsqrt(jnp.mean(o * o, -1, keepdims=True) + LN_EPS)
        g = gate_ref[:, sl].astype(F32)
        o_ref[:, sl] = (o * ng_ref[:, sl] * (g * _sigmoid(g))).astype(o_ref.dtype)


def _readout_call(o_f, o_b, px, norm_g, col_gate, tm):
    T, HW = o_f.shape
    GW = READOUT_HEADS * HEAD_DIM
    assert HW % GW == 0 and col_gate % READOUT_HEADS == 0
    blk = lambda col: pl.BlockSpec((tm, GW), lambda i, h: (i, col // READOUT_HEADS + h))
    return pl.pallas_call(
        _readout_kernel,
        grid=(T // tm, HW // GW),
        in_specs=[blk(0), blk(0), blk(col_gate), pl.BlockSpec((1, GW), lambda i, h: (0, h))],
        out_specs=blk(0),
        out_shape=jax.ShapeDtypeStruct((T, HW), BF16),
        compiler_params=_params(("parallel", "parallel")),
        name="hgrn_readout",
    )(o_f, o_b, px, norm_g.reshape(1, HW).astype(F32))


COND_ROWS = 16


def _adaln_kernel(cond_ref, w_ref, b_ref, o_ref):
    cnd = cond_ref[...]
    a = (cnd * _sigmoid(cnd)).astype(BF16)
    o_ref[0] = _dot(a, w_ref[0].astype(BF16)) + b_ref[0]


def _adaln_call(cond, w_mod, b_mod, tn=1024):
    depth, D, N = w_mod.shape
    return pl.pallas_call(
        _adaln_kernel,
        grid=(depth, N // tn),
        in_specs=[pl.BlockSpec((COND_ROWS, D), lambda l, j: (0, 0)),
                  pl.BlockSpec((1, D, tn), lambda l, j: (l, 0, j)),
                  pl.BlockSpec((1, 1, tn), lambda l, j: (l, 0, j))],
        out_specs=pl.BlockSpec((1, COND_ROWS, tn), lambda l, j: (l, 0, j)),
        out_shape=jax.ShapeDtypeStruct((depth, COND_ROWS, N), F32),
        compiler_params=_params(("parallel", "parallel")),
        name="adaln",
    )(cond, w_mod, b_mod.reshape(depth, 1, N))


def _cond_row(i, tm, n_lat_rows, L, B):
    return jnp.where(i * tm < n_lat_rows, (i * tm) // L, B)


def _layer_norm_rows(x):
    mu = jnp.mean(x, -1, keepdims=True)
    xc = x - mu
    var = jnp.mean(xc * xc, -1, keepdims=True)
    return xc * lax.rsqrt(var + LN_EPS)


def _ln_mod_kernel(x_ref, mod_ref, o_ref, *, tm, n_lat_rows, L, B, D):
    r = _cond_row(pl.program_id(0), tm, n_lat_rows, L, B)
    shift = mod_ref[pl.ds(r, 1), 0:D]
    scale = mod_ref[pl.ds(r, 1), D:2 * D]
    o_ref[...] = (_layer_norm_rows(x_ref[...]) * (1.0 + scale) + shift).astype(o_ref.dtype)


def _ln_mod_call(h, mod_l, B, L, tm=256):
    T, D = h.shape
    kern = functools.partial(_ln_mod_kernel, tm=tm, n_lat_rows=B * L, L=L, B=B, D=D)
    return pl.pallas_call(
        kern,
        grid=(T // tm,),
        in_specs=[pl.BlockSpec((tm, D), lambda i: (i, 0)), pl.BlockSpec(mod_l.shape, lambda i: (0, 0))],
        out_specs=pl.BlockSpec((tm, D), lambda i: (i, 0)),
        out_shape=jax.ShapeDtypeStruct((T, D), BF16),
        compiler_params=_params(("parallel",)),
        name="ln_modulate",
    )(h, mod_l)


def _wstat_matmul_kernel(*refs, widths):
    n = len(widths)
    a_refs, w_ref, o_ref, wbf_ref = refs[:n], refs[n], refs[n + 1], refs[n + 2]

    @pl.when(pl.program_id(1) == 0)
    def _():
        wbf_ref[...] = w_ref[0].astype(BF16)

    acc = None
    k0 = 0
    for a_ref, kw in zip(a_refs, widths):
        part = _dot(a_ref[...], wbf_ref[k0:k0 + kw, :])
        acc = part if acc is None else acc + part
        k0 += kw
    o_ref[...] = acc.astype(o_ref.dtype)


def _wstat_ksplit_kernel(a_ref, w_ref, o_ref, wbf_ref, acc_ref):
    i, k = pl.program_id(1), pl.program_id(2)
    nk = pl.num_programs(2)

    @pl.when(i == 0)
    def _():
        wbf_ref[k] = w_ref[0].astype(BF16)

    part = _dot(a_ref[...], wbf_ref[k])

    @pl.when(k == 0)
    def _():
        acc_ref[...] = part

    @pl.when(jnp.logical_and(k > 0, k < nk - 1))
    def _():
        acc_ref[...] += part

    @pl.when(k == nk - 1)
    def _():
        o_ref[...] = (acc_ref[...] + part).astype(o_ref.dtype)


def _wstat_ksplit_call(a, w_all, layer, tm, tn, nk, name):
    M, K = a.shape
    N = w_all.shape[2]
    tk = K // nk
    assert nk >= 2 and M % tm == 0 and N % tn == 0 and K % nk == 0
    return pl.pallas_call(
        _wstat_ksplit_kernel,
        grid=(N // tn, M // tm, nk),
        in_specs=[pl.BlockSpec((tm, tk), lambda j, i, k: (i, k)),
                  pl.BlockSpec((1, tk, tn), lambda j, i, k: (layer, jnp.where(i == 0, k, nk - 1), j))],
        out_specs=pl.BlockSpec((tm, tn), lambda j, i, k: (i, j)),
        out_shape=jax.ShapeDtypeStruct((M, N), BF16),
        scratch_shapes=[pltpu.VMEM((nk, tk, tn), BF16), pltpu.VMEM((tm, tn), F32)],
        compiler_params=_params(("parallel", "arbitrary", "arbitrary")),
        name=name,
    )(a, w_all)


def _wstat_matmul_call(a_list, w_all, layer, n_rows, tm, tn, name):
    widths = tuple(a.shape[1] for a in a_list)
    _, K, N = w_all.shape
    assert sum(widths) == K and n_rows % tm == 0 and N % tn == 0
    return pl.pallas_call(
        functools.partial(_wstat_matmul_kernel, widths=widths),
        grid=(N // tn, n_rows // tm),
        in_specs=[pl.BlockSpec((tm, kw), lambda j, i: (i, 0)) for kw in widths]
        + [pl.BlockSpec((1, K, tn), lambda j, i: (layer, 0, j))],
        out_specs=pl.BlockSpec((tm, tn), lambda j, i: (i, j)),
        out_shape=jax.ShapeDtypeStruct((n_rows, N), BF16),
        scratch_shapes=[pltpu.VMEM((K, tn), BF16)],
        compiler_params=_params(("parallel", "arbitrary")),
        name=name,
    )(*a_list, w_all)


def _split_bf16(x):
    hi = x.astype(BF16)
    return hi, (x - hi.astype(F32)).astype(BF16)


def _post_mix_kernel(y_ref, h_ref, mod_ref, g_ref, b_ref, rw_ref, rb_ref,
                     h1_ref, fin_ref, gates_ref, topi_ref, *, tm, n_lat_rows, L, B, D, alpha):
    r = _cond_row(pl.program_id(0), tm, n_lat_rows, L, B)
    mrow = lambda slot: mod_ref[pl.ds(r, 1), slot * D:(slot + 1) * D]
    u = alpha * h_ref[...] + mrow(2) * y_ref[...].astype(F32)
    h1 = _layer_norm_rows(u) * g_ref[...] + b_ref[...]
    h1_ref[...] = h1
    fin = _layer_norm_rows(h1) * (1.0 + mrow(4)) + mrow(3)
    fin_ref[...] = fin.astype(BF16)
    f_hi, f_lo = _split_bf16(fin)
    w_hi, w_lo = _split_bf16(rw_ref[...])
    logits = _dot(f_hi, w_hi) + _dot(f_hi, w_lo) + _dot(f_lo, w_hi) + rb_ref[...]
    lane = lax.broadcasted_iota(jnp.int32, logits.shape, 1)
    ninf = jnp.float32(-jnp.inf)
    xs = jnp.where(lane < N_EXPERTS, logits, ninf)
    sel_v, sel_i = [], []
    for _ in range(TOP_K):
        m = jnp.max(xs, -1, keepdims=True)
        idx = jnp.min(jnp.where(xs == m, lane, logits.shape[1]), -1, keepdims=True)
        sel_v.append(m)
        sel_i.append(idx)
        xs = jnp.where(lane == idx, ninf, xs)
    ex = [jnp.exp(v - sel_v[0]) for v in sel_v]
    den = ex[0]
    for e in ex[1:]:
        den = den + e
    gates = jnp.zeros(logits.shape, F32)
    topi = jnp.zeros(logits.shape, jnp.int32)
    for j in range(TOP_K):
        gates = jnp.where(lane == sel_i[j], ex[j] / den, gates)
        topi = jnp.where(lane == j, sel_i[j], topi)
    gates_ref[...] = gates
    topi_ref[...] = topi


def _post_mix_call(y, h, mod_l, ln_g, ln_b, rw_pad, rb_pad, B, L, alpha, tm=256):
    n_out, D = y.shape
    NE = rw_pad.shape[1]
    kern = functools.partial(_post_mix_kernel, tm=tm, n_lat_rows=B * L, L=L, B=B, D=D, alpha=alpha)
    row = lambda i: (i, 0)
    fixed = lambda i: (0, 0)
    return pl.pallas_call(
        kern,
        grid=(n_out // tm,),
        in_specs=[pl.BlockSpec((tm, D), row), pl.BlockSpec((tm, D), row),
                  pl.BlockSpec(mod_l.shape, fixed),
                  pl.BlockSpec((1, D), fixed), pl.BlockSpec((1, D), fixed),
                  pl.BlockSpec((D, NE), fixed), pl.BlockSpec((1, NE), fixed)],
        out_specs=[pl.BlockSpec((tm, D), row), pl.BlockSpec((tm, D), row),
                   pl.BlockSpec((tm, NE), row), pl.BlockSpec((tm, NE), row)],
        out_shape=[jax.ShapeDtypeStruct((n_out, D), F32), jax.ShapeDtypeStruct((n_out, D), BF16),
                   jax.ShapeDtypeStruct((n_out, NE), F32), jax.ShapeDtypeStruct((n_out, NE), jnp.int32)],
        compiler_params=_params(("parallel",)),
        name="post_mix_router",
    )(y, h, mod_l, ln_g.reshape(1, D), ln_b.reshape(1, D), rw_pad, rb_pad)


MOE_TILE = 512


def _visit_flags(v, vt_ref, ve_ref):
    p = jnp.maximum(v - 1, 0)
    return (v == 0) | (vt_ref[p] != vt_ref[v]), (v == 0) | (ve_ref[p] != ve_ref[v])


def _visit_row_mask(v, vt_ref, ve_ref, offs_ref, shape):
    e = ve_ref[v]
    row = lax.broadcasted_iota(jnp.int32, shape, 0) + vt_ref[v] * shape[0]
    return jnp.logical_and(row >= offs_ref[e], row < offs_ref[e + 1])


def _masked_tile_store(o_ref, val, mask, first_tile):
    @pl.when(first_tile)
    def _():
        o_ref[...] = jnp.where(mask, val, 0.0).astype(o_ref.dtype)

    @pl.when(jnp.logical_not(first_tile))
    def _():
        o_ref[...] = jnp.where(mask, val, o_ref[...].astype(F32)).astype(o_ref.dtype)


def _moe_up_kernel(vt_ref, ve_ref, vv_ref, offs_ref, x_ref, w1_ref, b1_ref, o_ref, wbf_ref, *, DE):
    v = pl.program_id(0)
    first_tile, first_expert = _visit_flags(v, vt_ref, ve_ref)

    @pl.when(first_expert)
    def _():
        wbf_ref[...] = w1_ref[0, 0].astype(BF16)

    @pl.when(vv_ref[v] > 0)
    def _():
        hcat = _dot(x_ref[...], wbf_ref[...]) + b1_ref[0, 0]
        gate = jnp.minimum(hcat[:, :DE], SWIGLU_LIMIT)
        up = jnp.clip(hcat[:, DE:], -SWIGLU_LIMIT, SWIGLU_LIMIT)
        hdn = (up + 1.0) * (gate * _sigmoid(SWIGLU_ALPHA * gate))
        mask = _visit_row_mask(v, vt_ref, ve_ref, offs_ref, hdn.shape)
        _masked_tile_store(o_ref, hdn, mask, first_tile)


def _moe_down_kernel(vt_ref, ve_ref, vv_ref, offs_ref, h_ref, rw_ref, w2_ref, b2_ref, o_ref, wbf_ref):
    v = pl.program_id(0)
    first_tile, first_expert = _visit_flags(v, vt_ref, ve_ref)

    @pl.when(first_expert)
    def _():
        wbf_ref[...] = w2_ref[0, 0].astype(BF16)

    @pl.when(vv_ref[v] > 0)
    def _():
        y = (_dot(h_ref[...], wbf_ref[...]) + b2_ref[0, 0]) * rw_ref[...]
        mask = _visit_row_mask(v, vt_ref, ve_ref, offs_ref, y.shape)
        _masked_tile_store(o_ref, y, mask, first_tile)


def _moe_call(xs, row_w, visits, w1_all, b1_all, w2_all, b2_all, layer):
    n, D = xs.shape
    _, E, _, DE2 = w1_all.shape
    DE = DE2 // 2
    tm = MOE_TILE
    nv = visits[0].shape[0]
    tile = lambda v, vt, ve, vv, offs: (vt[v], 0)
    expert = lambda v, vt, ve, vv, offs: (layer, ve[v], 0, 0)
    hdn = pl.pallas_call(
        functools.partial(_moe_up_kernel, DE=DE),
        grid_spec=pltpu.PrefetchScalarGridSpec(
            num_scalar_prefetch=4, grid=(nv,),
            in_specs=[pl.BlockSpec((tm, D), tile),
                      pl.BlockSpec((1, 1, D, DE2), expert),
                      pl.BlockSpec((1, 1, 1, DE2), expert)],
            out_specs=pl.BlockSpec((tm, DE), tile),
            scratch_shapes=[pltpu.VMEM((D, DE2), BF16)]),
        out_shape=jax.ShapeDtypeStruct((n, DE), BF16),
        compiler_params=_params(("arbitrary",)),
        name="moe_up",
    )(*visits, xs, w1_all, b1_all.reshape(b1_all.shape[0], E, 1, DE2))
    return pl.pallas_call(
        _moe_down_kernel,
        grid_spec=pltpu.PrefetchScalarGridSpec(
            num_scalar_prefetch=4, grid=(nv,),
            in_specs=[pl.BlockSpec((tm, DE), tile),
                      pl.BlockSpec((tm, 1), tile),
                      pl.BlockSpec((1, 1, DE, D), expert),
                      pl.BlockSpec((1, 1, 1, D), expert)],
            out_specs=pl.BlockSpec((tm, D), tile),
            scratch_shapes=[pltpu.VMEM((DE, D), BF16)]),
        out_shape=jax.ShapeDtypeStruct((n, D), BF16),
        compiler_params=_params(("arbitrary",)),
        name="moe_down",
    )(*visits, hdn, row_w, w2_all, b2_all.reshape(b2_all.shape[0], E, 1, D))


def _moe_dispatch(topi, gates, tm):
    T = topi.shape[0]
    E = N_EXPERTS
    n = T * TOP_K
    n_tiles = n // tm
    flat_e = topi.reshape(-1)
    flat_w = jnp.take_along_axis(gates, topi, axis=1).reshape(-1)
    order = jnp.argsort(flat_e, stable=True).astype(jnp.int32)
    inv = jnp.argsort(order).astype(jnp.int32)
    sorted_tok = order // TOP_K
    sorted_w = flat_w[order].reshape(n, 1)
    pos_kmajor = inv.reshape(T, TOP_K).T.reshape(-1)
    counts = jnp.sum((flat_e[:, None] == jnp.arange(E, dtype=jnp.int32)[None, :]).astype(jnp.int32), axis=0)
    ends = jnp.cumsum(counts)
    offs = jnp.concatenate([jnp.zeros((1,), jnp.int32), ends]).astype(jnp.int32)
    first_tile = offs[:-1] // tm
    n_vis = jnp.where(counts > 0, (ends - 1) // tm - first_tile + 1, 0)
    v_end = jnp.cumsum(n_vis)
    v_start = v_end - n_vis
    total = v_end[-1]
    nv = n_tiles + E - 1
    v = jnp.arange(nv, dtype=jnp.int32)
    ve = jnp.minimum(jnp.sum((v[:, None] >= v_end[None, :]).astype(jnp.int32), axis=1), E - 1)
    vt = first_tile[ve] + v - v_start[ve]
    valid = v < total
    ve = jnp.where(valid, ve, ve[total - 1]).astype(jnp.int32)
    vt = jnp.where(valid, vt, n_tiles - 1).astype(jnp.int32)
    return sorted_tok, sorted_w, pos_kmajor, (vt, ve, valid.astype(jnp.int32), offs)


def _final_kernel(h_ref, *refs, tm, n_lat_rows, L, B, D, alpha, with_next):
    y_refs, rest = refs[:TOP_K], refs[TOP_K:]
    r = _cond_row(pl.program_id(0), tm, n_lat_rows, L, B)
    gate = rest[0][pl.ds(r, 1), 5 * D:6 * D]
    f = y_refs[0][...].astype(F32)
    for y_ref in y_refs[1:]:
        f = f + y_ref[...].astype(F32)
    u = alpha * h_ref[...] + gate * f
    h2 = _layer_norm_rows(u) * rest[1][...] + rest[2][...]
    if with_next:
        modn_ref, o_ref, a_ref = rest[3:]
        a_ref[...] = (_layer_norm_rows(h2) * (1.0 + modn_ref[pl.ds(r, 1), D:2 * D])
                      + modn_ref[pl.ds(r, 1), 0:D]).astype(a_ref.dtype)
    else:
        o_ref = rest[3]
    o_ref[...] = h2


def _final_call(h1, y_assign, mod_l, ln_g, ln_b, B, L, alpha, mod_next=None, tm=256):
    n, D = h1.shape
    nt = n // tm
    with_next = mod_next is not None
    kern = functools.partial(_final_kernel, tm=tm, n_lat_rows=B * L, L=L, B=B, D=D, alpha=alpha,
                             with_next=with_next)
    row = lambda i: (i, 0)
    fixed = lambda i: (0, 0)
    y_specs = [pl.BlockSpec((tm, D), functools.partial(lambda i, k: (k * nt + i, 0), k=k)) for k in range(TOP_K)]
    mod_spec = pl.BlockSpec(mod_l.shape, fixed)
    row_spec = pl.BlockSpec((tm, D), row)
    return pl.pallas_call(
        kern,
        grid=(nt,),
        in_specs=[row_spec] + y_specs + [mod_spec, pl.BlockSpec((1, D), fixed), pl.BlockSpec((1, D), fixed)]
        + ([mod_spec] if with_next else []),
        out_specs=[row_spec, row_spec] if with_next else row_spec,
        out_shape=([jax.ShapeDtypeStruct((n, D), F32), jax.ShapeDtypeStruct((n, D), BF16)] if with_next
                   else jax.ShapeDtypeStruct((n, D), F32)),
        compiler_params=_params(("parallel",)),
        name="final_ln",
    )(h1, *([y_assign] * TOP_K), mod_l, ln_g.reshape(1, D), ln_b.reshape(1, D),
      *([mod_next] if with_next else []))


def kernel(x, c, ctx, c_ctx, w_mod, b_mod, w_in, pool_w, pool_scale, na_rpb, hg_lb, hg_norm_g, w_out, ln1_g, ln1_b, ln2_g, ln2_b, router_w, router_b, exp_w1, exp_b1, exp_w2, exp_b2):
    B, L, D = x.shape
    Lc = ctx.shape[1]
    depth = w_in.shape[0]
    alpha = (2 * depth) ** 0.25
    pool_width = pool_w.shape[1] * pool_w.shape[2]
    na_width = na_rpb.shape[1] * HEAD_DIM
    hg_width = hg_norm_g.shape[1]
    na_heads, hg_heads = na_width // HEAD_DIM, hg_width // HEAD_DIM
    c_naq = pool_width // HEAD_DIM
    c_nak, c_nav = c_naq + na_heads, c_naq + 2 * na_heads
    c_hq = c_naq + 3 * na_heads
    c_hff, c_hfb, c_hi, c_hg = (c_hq + k * hg_heads for k in range(1, 5))
    assert B + 1 <= COND_ROWS

    hall = jnp.concatenate([x.reshape(B * L, D), ctx.reshape(B * Lc, D)], axis=0)
    cond = jnp.concatenate([c, c_ctx[None], jnp.zeros((COND_ROWS - B - 1, D), F32)], axis=0)
    mod = _adaln_call(cond, w_mod, b_mod)
    lb_soft = jax.nn.softmax(hg_lb.astype(F32), axis=1)
    lower = jnp.cumsum(lb_soft, axis=1) - lb_soft[:, :1]
    rw_pad = jnp.pad(router_w, ((0, 0), (0, 0), (0, HEAD_DIM - N_EXPERTS)))
    rb_pad = jnp.pad(router_b, ((0, 0), (0, HEAD_DIM - N_EXPERTS)))

    T = hall.shape[0]
    n_row_tiles = 8
    a_in = _ln_mod_call(hall, mod[0], B, L)
    for l in range(depth):
        last = l == depth - 1
        px = _wstat_ksplit_call(a_in, w_in, l, T // 4, 512, 2, "in_proj")
        a_mix = _pool_call(px, pool_w[l].astype(BF16), pool_scale[l], B, L, Lc)
        b_x, b_c = _na_call(px, na_rpb[l], B, L, Lc, c_naq, c_nak, c_nav, na_heads)
        b_mix = jnp.concatenate([b_x, b_c], axis=0)
        o_f, o_b = _hgrn_call(px, lower[0, l].reshape(hg_heads, 1, HEAD_DIM),
                              lower[1, l].reshape(hg_heads, 1, HEAD_DIM), B, L, Lc,
                              c_hq, c_hff, c_hfb, c_hi, hg_heads)
        c_mix = _readout_call(o_f, o_b, px, hg_norm_g[l], c_hg, T // n_row_tiles)
        n_out = B * L if last else T
        y = _wstat_matmul_call([a_mix, b_mix, c_mix], w_out, l, n_out, n_out // n_row_tiles, 512, "out_proj")
        h1, fin, gates, topi = _post_mix_call(y, hall, mod[l], ln1_g[l], ln1_b[l], rw_pad[l],
                                              rb_pad[l].reshape(1, -1), B, L, alpha)
        sorted_tok, sorted_w, pos_kmajor, visits = _moe_dispatch(topi[:, :TOP_K], gates, MOE_TILE)
        xs = fin.at[sorted_tok].get(mode="promise_in_bounds")
        ys = _moe_call(xs, sorted_w, visits, exp_w1, exp_b1, exp_w2, exp_b2, l)
        y_assign = ys.at[pos_kmajor].get(mode="promise_in_bounds")
        if last:
            hall = _final_call(h1, y_assign, mod[l], ln2_g[l], ln2_b[l], B, L, alpha)
        else:
            hall, a_in = _final_call(h1, y_assign, mod[l], ln2_g[l], ln2_b[l], B, L, alpha, mod_next=mod[l + 1])
    return hall.reshape(B, L, D)
```

```python
import functools

import jax
import jax.numpy as jnp
import numpy as np
from jax import lax
from jax.experimental import pallas as pl
from jax.experimental.pallas import tpu as pltpu

F32 = jnp.float32
BF16 = jnp.bfloat16

HEAD_DIM = 128
GRID_W = 64
POOL_WINDOWS = (2, 4, 8, 16)
NA_ROWS = 8
NA_COLS = 16
FORGET_EPS = 1e-20
N_EXPERTS = 32
TOP_K = 4
SWIGLU_LIMIT = 7.0
SWIGLU_ALPHA = 1.702
ROPE_BASE = 10000.0
LN_EPS = 1e-6

VMEM_LIMIT_BYTES = 56 * 1024 * 1024
HG_CHUNK = 256
NEG_BIG = -1e30


def _params(sem):
    return pltpu.CompilerParams(dimension_semantics=sem, vmem_limit_bytes=VMEM_LIMIT_BYTES)


def _dot(a, b):
    return jnp.dot(a, b, preferred_element_type=F32)


def _dot_nt(a, b):
    return lax.dot_general(a, b, (((1,), (1,)), ((), ())), preferred_element_type=F32)


def _sigmoid(x):
    return 1.0 / (1.0 + jnp.exp(-x))


LOG2E = 1.4426950408889634
HG_HALF = HG_CHUNK // 2
HG_HEADS_PER_STEP = 2


def _neg_abs(x):
    bits = lax.bitcast_convert_type(x, jnp.uint32) | jnp.uint32(0x80000000)
    return lax.bitcast_convert_type(bits, F32)


def _hgrn_half_scores(q, k, b2, g2, fc, lev, reverse):
    n = q.shape[0]
    row = lax.broadcasted_iota(jnp.int32, (n, HEAD_DIM), 0)
    a = jnp.where(lev == -1, _dot_nt(q.astype(BF16), k.astype(BF16)), 0.0)
    h, lvl = 1, 0
    while h < n:
        if h < 8:
            upper = (row & h) != 0
            t_role = jnp.logical_not(upper) if reverse else upper
            if h == 1:
                x = jnp.where(t_role, q * fc, k)
            elif h == 2:
                r4 = row & 3
                g_prev, g_next = pltpu.roll(g2, 1, 0), pltpu.roll(g2, n - 1, 0)
                if reverse:
                    e = jnp.where(r4 == 0, g2 + g_next, jnp.where(r4 == 1, g2, jnp.where(r4 == 2, 0.0, g_prev)))
                else:
                    e = jnp.where(r4 == 0, g_next, jnp.where(r4 == 1, 0.0, jnp.where(r4 == 2, g2, g2 + g_prev)))
                x = jnp.where(t_role, q, k) * jnp.exp2(e)
            else:
                m0 = h if reverse else h - 1
                b3 = b2.reshape(n // (2 * h), 2 * h, HEAD_DIM)
                ref = jnp.broadcast_to(b3[:, m0:m0 + 1, :], b3.shape).reshape(n, HEAD_DIM)
                x = jnp.where(t_role, q, k) * jnp.exp2(_neg_abs(b2 - ref))
            x = x.astype(BF16)
            a = jnp.where(lev == lvl, _dot_nt(x, x), a)
        else:
            nb = n // (2 * h)
            q4 = q.reshape(nb, 2, h, HEAD_DIM)
            k4 = k.reshape(nb, 2, h, HEAD_DIM)
            b4 = b2.reshape(nb, 2, h, HEAD_DIM)
            s_half, t_half, m_row = (1, 0, 0) if reverse else (0, 1, h - 1)
            ref = b4[:, s_half:s_half + 1, m_row:m_row + 1, :]
            xt = q4[:, t_half] * jnp.exp2(b4[:, t_half] - ref[:, 0])
            xs = k4[:, s_half] * jnp.exp2(ref[:, 0] - b4[:, s_half])
            halves = [xt, xs] if reverse else [xs, xt]
            x = jnp.stack(halves, axis=1).reshape(n, HEAD_DIM).astype(BF16)
            g = _dot_nt(xt.reshape(n // 2, HEAD_DIM).astype(BF16), x)
            a4 = a.reshape(nb, 2, h, n)
            lev_t = lev.reshape(nb, 2, h, n)[:, t_half]
            a_t = jnp.where(lev_t == lvl, g.reshape(nb, h, n), a4[:, t_half])
            halves = [a_t, a4[:, s_half]] if reverse else [a4[:, s_half], a_t]
            a = jnp.stack(halves, axis=1).reshape(n, n)
        h *= 2
        lvl += 1
    return a


def _hgrn_direction(q, z, v_bf, lb, st_ref, lev, tri_bf, reverse):
    C, H = HG_CHUNK, HG_HALF
    q = q * _sigmoid(q)
    sig = _sigmoid(z)
    fc = jnp.maximum(lb + (1.0 - lb) * sig, FORGET_EPS)
    g2 = jnp.log(fc) * LOG2E
    k = (1.0 - lb) * (1.0 - sig)

    g_hi = g2.astype(BF16)
    r1 = g2 - g_hi.astype(F32)
    g_mid = r1.astype(BF16)
    g_lo = (r1 - g_mid.astype(F32)).astype(BF16)
    b2 = _dot(tri_bf, g_hi) + _dot(tri_bf, g_mid) + _dot(tri_bf, g_lo)

    lo, hi = slice(0, H), slice(H, C)
    a_lo = _hgrn_half_scores(q[lo], k[lo], b2[lo], g2[lo], fc[lo], lev, reverse).astype(BF16)
    a_hi = _hgrn_half_scores(q[hi], k[hi], b2[hi], g2[hi], fc[hi], lev, reverse).astype(BF16)
    t_sl, s_sl, m = (lo, hi, H) if reverse else (hi, lo, H - 1)
    ref = b2[m:m + 1, :]
    qt = (q[t_sl] * jnp.exp2(b2[t_sl] - ref)).astype(BF16)
    ks = (k[s_sl] * jnp.exp2(ref - b2[s_sl])).astype(BF16)
    cross = _dot(_dot_nt(qt, ks).astype(BF16), v_bf[s_sl])
    o_lo = _dot(a_lo, v_bf[lo])
    o_hi = _dot(a_hi, v_bf[hi])
    o_intra = jnp.concatenate([o_lo + cross, o_hi] if reverse else [o_lo, o_hi + cross], axis=0)

    st = st_ref[...]
    tot = b2[0:1, :] if reverse else b2[C - 1:C, :]
    qdec = (q * jnp.exp2(b2)).astype(BF16)
    o_inter = _dot_nt(qdec, st.astype(BF16))
    kdec = (k * jnp.exp2(tot - b2)).astype(BF16)
    vt = v_bf.astype(F32).T.astype(BF16)
    st_ref[...] = jnp.exp2(tot) * st + _dot(vt, kdec)
    return o_intra + o_inter


def _hgrn_kernel(qf_ref, zf_ref, vf_ref, qb_ref, zb_ref, vb_ref, lbf_ref, lbb_ref, levf_ref, levb_ref,
                 trif_ref, trib_ref, of_ref, ob_ref, sf_ref, sb_ref):
    @pl.when(pl.program_id(2) == 0)
    def _():
        sf_ref[...] = jnp.zeros_like(sf_ref)
        sb_ref[...] = jnp.zeros_like(sb_ref)

    for hh in range(HG_HEADS_PER_STEP):
        sl = slice(hh * HEAD_DIM, (hh + 1) * HEAD_DIM)
        of_ref[:, sl] = _hgrn_direction(qf_ref[:, sl].astype(F32), zf_ref[:, sl].astype(F32), vf_ref[:, sl],
                                        lbf_ref[0][:, sl], sf_ref.at[hh], levf_ref[...], trif_ref[...],
                                        False).astype(of_ref.dtype)
        ob_ref[:, sl] = _hgrn_direction(qb_ref[:, sl].astype(F32), zb_ref[:, sl].astype(F32), vb_ref[:, sl],
                                        lbb_ref[0][:, sl], sb_ref.at[hh], levb_ref[...], trib_ref[...],
                                        True).astype(ob_ref.dtype)


def _hgrn_call(px, lb_f, lb_b, B, L, Lc, col_q, col_ff, col_fb, col_i, n_heads):
    C, H = HG_CHUNK, HG_HALF
    T = px.shape[0]
    nlc, ncc = L // C, Lc // C
    ns = nlc + ncc
    ctx0 = B * nlc

    def fwd_row(b, s):
        return jnp.where(s < ncc, ctx0 + b * ncc + s, b * nlc + s - ncc)

    def bwd_row(b, s):
        return jnp.where(s < ncc, ctx0 + b * ncc + (ncc - 1 - s), b * nlc + (nlc - 1 - (s - ncc)))

    G = HG_HEADS_PER_STEP
    GW = G * HEAD_DIM
    assert n_heads % G == 0 and all(c % G == 0 for c in (col_q, col_ff, col_fb, col_i))

    def spec(row_fn, col):
        return pl.BlockSpec((C, GW), lambda b, h, s: (row_fn(b, s), col // G + h))

    idx = np.arange(H)
    xr = idx[:, None] ^ idx[None, :]
    level = np.floor(np.log2(np.maximum(xr, 1))).astype(np.int32)
    lev_f = np.where(xr == 0, -1, np.where(idx[:, None] > idx[None, :], level, -2)).astype(np.int32)
    lev_b = np.ascontiguousarray(lev_f.T)
    idc = np.arange(C)
    tri_f = (idc[:, None] >= idc[None, :]).astype(np.float32)
    tri_b = (idc[:, None] <= idc[None, :]).astype(np.float32)
    const = lambda shape: pl.BlockSpec(shape, lambda b, h, s: (0,) * len(shape))
    lb_spec = pl.BlockSpec((1, 1, GW), lambda b, h, s: (h, 0, 0))
    out_sds = jax.ShapeDtypeStruct((T, n_heads * HEAD_DIM), BF16)
    state = pltpu.VMEM((G, HEAD_DIM, HEAD_DIM), F32)
    return pl.pallas_call(
        _hgrn_kernel,
        grid=(B, n_heads // G, ns),
        in_specs=[spec(fwd_row, col_q), spec(fwd_row, col_ff), spec(fwd_row, col_i),
                  spec(bwd_row, col_q), spec(bwd_row, col_fb), spec(bwd_row, col_i),
                  lb_spec, lb_spec, const((H, H)), const((H, H)), const((C, C)), const((C, C))],
        out_specs=[pl.BlockSpec((C, GW), lambda b, h, s: (fwd_row(b, s), h)),
                   pl.BlockSpec((C, GW), lambda b, h, s: (bwd_row(b, s), h))],
        out_shape=[out_sds, out_sds],
        scratch_shapes=[state, state],
        compiler_params=_params(("parallel", "parallel", "arbitrary")),
        name="hgrn_scan",
    )(px, px, px, px, px, px, lb_f.reshape(n_heads // G, 1, GW), lb_b.reshape(n_heads // G, 1, GW),
      jnp.asarray(lev_f), jnp.asarray(lev_b), jnp.asarray(tri_f, BF16), jnp.asarray(tri_b, BF16))


NA_QROWS = 4
NA_KROWS = NA_QROWS + NA_ROWS


def _na_block_cases(rows):
    assert rows % NA_QROWS == 0 and rows >= 2 * NA_ROWS
    nb = rows // NA_QROWS
    cases = []
    for i in (0, 1, nb - 1):
        r0 = NA_QROWS * i
        ks = min(max(r0 - NA_ROWS // 2, 0), rows - NA_KROWS)
        per_q = []
        for qr in range(NA_QROWS):
            r = r0 + qr
            rs = min(max(r - NA_ROWS // 2, 0), rows - NA_ROWS)
            per_q.append((rs - ks, ks - r + NA_ROWS - 1))
        cases.append(per_q)
    return cases


def _na_kernel(rpb_ref, q_ref, k_ref, v_ref, qc_ref, kc_ref, vc_ref, cos_ref, sin_ref,
               ox_ref, oc_ref, qr_ref, kr_ref, tb_ref, bias_ref, *, rows):
    h = pl.program_id(0)
    L = q_ref.shape[0]
    W = GRID_W
    scale = HEAD_DIM ** -0.5
    n_dr = 2 * NA_ROWS - 1
    n_dc = 2 * NA_COLS - 1

    @pl.when(pl.program_id(1) == 0)
    def _():
        c_i = lax.broadcasted_iota(jnp.int32, (W, W), 0)
        kc_i = lax.broadcasted_iota(jnp.int32, (W, W), 1)
        cs = jnp.clip(c_i - NA_COLS // 2, 0, W - NA_COLS)
        in_win = (kc_i >= cs) & (kc_i < cs + NA_COLS)
        c_off = kc_i - c_i + (NA_COLS - 1)
        for dr in range(n_dr):
            acc = jnp.zeros((W, W), F32)
            for j in range(n_dc):
                acc = jnp.where(c_off == j, rpb_ref[h, dr, j], acc)
            tb_ref[dr] = jnp.where(in_win, acc, NEG_BIG)
        neg = jnp.full((W, W), NEG_BIG, F32)
        for ci, per_q in enumerate(_na_block_cases(rows)):
            for qr, (j0, dr0) in enumerate(per_q):
                for j in range(NA_KROWS):
                    valid = j0 <= j < j0 + NA_ROWS
                    blk = tb_ref[dr0 + j] if valid else neg
                    bias_ref[ci, qr * W:(qr + 1) * W, j * W:(j + 1) * W] = blk

    RC = 256
    lane = lax.broadcasted_iota(jnp.int32, (RC, HEAD_DIM), 1)
    first = (lane & (HEAD_DIM // 2 - 1)) < HEAD_DIM // 4

    def rope_chunk(i, carry):
        sl = pl.ds(pl.multiple_of(i * RC, RC), RC)
        cos = cos_ref[sl, :]
        sin = sin_ref[sl, :]
        for src, dst, mul in ((q_ref, qr_ref, scale), (k_ref, kr_ref, 1.0)):
            t = src[sl, :].astype(F32)
            partner = jnp.where(first, pltpu.roll(t, HEAD_DIM - HEAD_DIM // 4, 1), pltpu.roll(t, HEAD_DIM // 4, 1))
            dst[sl, :] = ((t * cos + partner * sin) * mul).astype(BF16)
        return carry

    lax.fori_loop(0, L // RC, rope_chunk, 0)

    kc = kc_ref[...]
    vc = vc_ref[...]
    nb = rows // NA_QROWS
    QB = NA_QROWS * W
    KB = NA_KROWS * W

    def block(i, carry):
        ks = jnp.clip(i * NA_QROWS - NA_ROWS // 2, 0, rows - NA_KROWS)
        case = jnp.where(i == 0, 0, jnp.where(i == nb - 1, 2, 1))
        qs = pl.ds(pl.multiple_of(i * QB, QB), QB)
        kslice = pl.ds(pl.multiple_of(ks * W, W), KB)
        qb = qr_ref[qs, :]
        s_loc = _dot_nt(qb, kr_ref[kslice, :]) + bias_ref[case]
        s_ctx = _dot_nt(qb, kc)
        m = jnp.maximum(jnp.max(s_loc, -1, keepdims=True), jnp.max(s_ctx, -1, keepdims=True))
        p_loc = jnp.exp(s_loc - m)
        p_ctx = jnp.exp(s_ctx - m)
        den = jnp.sum(p_loc, -1, keepdims=True) + jnp.sum(p_ctx, -1, keepdims=True)
        o = _dot(p_loc.astype(BF16), v_ref[kslice, :]) + _dot(p_ctx.astype(BF16), vc)
        ox_ref[qs, :] = (o / den).astype(ox_ref.dtype)
        return carry

    lax.fori_loop(0, nb, block, 0, unroll=2)

    qc = (qc_ref[...].astype(F32) * scale).astype(BF16)
    s = _dot_nt(qc, kc)
    p = jnp.exp(s - jnp.max(s, -1, keepdims=True))
    oc = _dot(p.astype(BF16), vc) / jnp.sum(p, -1, keepdims=True)
    oc_ref[...] = oc.astype(oc_ref.dtype)


def _rope_tables(L):
    pos = jnp.arange(L)
    half = HEAD_DIM // 2
    inv = ROPE_BASE ** (-jnp.arange(0, half, 2, dtype=F32) / half)

    def tab(p):
        ang = p.astype(F32)[:, None] * inv[None]
        cos, sin = jnp.cos(ang), jnp.sin(ang)
        return jnp.concatenate([cos, cos], -1), jnp.concatenate([-sin, sin], -1)

    c_r, s_r = tab(pos // GRID_W)
    c_c, s_c = tab(pos % GRID_W)
    return jnp.concatenate([c_r, c_c], -1), jnp.concatenate([s_r, s_c], -1)


def _na_call(px, rpb, B, L, Lc, col_q, col_k, col_v, n_heads):
    rows = L // GRID_W
    cos, sin = _rope_tables(L)
    ctx0 = B * L // Lc

    def lat(col):
        return pl.BlockSpec((L, HEAD_DIM), lambda h, b: (b, col + h))

    def cx(col):
        return pl.BlockSpec((Lc, HEAD_DIM), lambda h, b: (ctx0 + b, col + h))

    tab = pl.BlockSpec((L, HEAD_DIM), lambda h, b: (0, 0))
    W = GRID_W
    return pl.pallas_call(
        functools.partial(_na_kernel, rows=rows),
        grid=(n_heads, B),
        in_specs=[pl.BlockSpec(memory_space=pltpu.SMEM), lat(col_q), lat(col_k), lat(col_v),
                  cx(col_q), cx(col_k), cx(col_v), tab, tab],
        out_specs=[pl.BlockSpec((L, HEAD_DIM), lambda h, b: (b, h)),
                   pl.BlockSpec((Lc, HEAD_DIM), lambda h, b: (b, h))],
        out_shape=[jax.ShapeDtypeStruct((B * L, n_heads * HEAD_DIM), BF16),
                   jax.ShapeDtypeStruct((B * Lc, n_heads * HEAD_DIM), BF16)],
        scratch_shapes=[pltpu.VMEM((L, HEAD_DIM), BF16), pltpu.VMEM((L, HEAD_DIM), BF16),
                        pltpu.VMEM((2 * NA_ROWS - 1, W, W), F32),
                        pltpu.VMEM((3, NA_QROWS * W, NA_KROWS * W), F32)],
        compiler_params=_params(("parallel", "arbitrary")),
        name="neighborhood_attention",
    )(rpb, px, px, px, px, px, px, cos, sin)


POOL_TILE = 256
POOL_HALO = 16


def _pool_kernel(prev_ref, cur_ref, next_ref, w_ref, scale_ref, o_ref, *, n_lat_tiles, nl, nc):
    i = pl.program_id(0)
    is_lat = i < n_lat_tiles
    j = jnp.where(is_lat, i % nl, (i - n_lat_tiles) % nc)
    n = jnp.where(is_lat, nl, nc)
    TQ, HL = POOL_TILE, POOL_HALO
    lo_min = jnp.where(j > 0, -HL, 0)
    hi_max = jnp.where(j < n - 1, TQ + HL, TQ)
    src = jnp.concatenate([prev_ref[...], cur_ref[...], next_ref[...]], axis=0)
    t = lax.broadcasted_iota(jnp.int32, (TQ, TQ + 2 * HL), 0)
    s = lax.broadcasted_iota(jnp.int32, (TQ, TQ + 2 * HL), 1) - HL
    Dg = w_ref.shape[1]
    t1 = lax.broadcasted_iota(jnp.int32, (TQ, Dg), 0)
    for g, w in enumerate(POOL_WINDOWS):
        lo = jnp.maximum(t - w // 2, lo_min)
        hi = jnp.minimum(t + (w - w // 2), hi_max)
        band = jnp.where(s >= lo, jnp.where(s < hi, 1.0, 0.0), 0.0).astype(BF16)
        cnt = (jnp.minimum(t1 + (w - w // 2), hi_max) - jnp.maximum(t1 - w // 2, lo_min)).astype(F32)
        cols = slice(g * Dg, (g + 1) * Dg)
        win = _dot(band, src[:, cols])
        d = win / cnt - cur_ref[:, cols].astype(F32)
        y = _dot(d.astype(BF16), w_ref[g]) * scale_ref[:, cols]
        o_ref[:, cols] = y.astype(o_ref.dtype)


def _pool_call(px, pool_w_bf, pool_scale, B, L, Lc):
    T = px.shape[0]
    G, Dg, _ = pool_w_bf.shape
    PW = G * Dg
    TQ, HL = POOL_TILE, POOL_HALO
    r = TQ // HL
    n_tiles = T // TQ
    n_halo = T // HL
    kern = functools.partial(_pool_kernel, n_lat_tiles=B * L // TQ, nl=L // TQ, nc=Lc // TQ)
    return pl.pallas_call(
        kern,
        grid=(n_tiles,),
        in_specs=[pl.BlockSpec((HL, PW), lambda i: (jnp.maximum(i * r - 1, 0), 0)),
                  pl.BlockSpec((TQ, PW), lambda i: (i, 0)),
                  pl.BlockSpec((HL, PW), lambda i: (jnp.minimum((i + 1) * r, n_halo - 1), 0)),
                  pl.BlockSpec((G, Dg, Dg), lambda i: (0, 0, 0)),
                  pl.BlockSpec((1, PW), lambda i: (0, 0))],
        out_specs=pl.BlockSpec((TQ, PW), lambda i: (i, 0)),
        out_shape=jax.ShapeDtypeStruct((T, PW), BF16),
        compiler_params=_params(("parallel",)),
        name="multiscale_pool",
    )(px, px, px, pool_w_bf, pool_scale.reshape(1, PW).astype(F32))


READOUT_HEADS = 4


def _readout_kernel(of_ref, ob_ref, gate_ref, ng_ref, o_ref):
    for hh in range(READOUT_HEADS):
        sl = slice(hh * HEAD_DIM, (hh + 1) * HEAD_DIM)
        o = of_ref[:, sl].astype(F32) + ob_ref[:, sl].astype(F32)
        o = o * lax.rsqrt(jnp.mean(o * o, -1, keepdims=True) + LN_EPS)
        g = gate_ref[:, sl].astype(F32)
        o_ref[:, sl] = (o * ng_ref[:, sl] * (g * _sigmoid(g))).astype(o_ref.dtype)


def _readout_call(o_f, o_b, px, norm_g, col_gate, tm):
    T, HW = o_f.shape
    GW = READOUT_HEADS * HEAD_DIM
    assert HW % GW == 0 and col_gate % READOUT_HEADS == 0
    blk = lambda col: pl.BlockSpec((tm, GW), lambda i, h: (i, col // READOUT_HEADS + h))
    return pl.pallas_call(
        _readout_kernel,
        grid=(T // tm, HW // GW),
        in_specs=[blk(0), blk(0), blk(col_gate), pl.BlockSpec((1, GW), lambda i, h: (0, h))],
        out_specs=blk(0),
        out_shape=jax.ShapeDtypeStruct((T, HW), BF16),
        compiler_params=_params(("parallel", "parallel")),
        name="hgrn_readout",
    )(o_f, o_b, px, norm_g.reshape(1, HW).astype(F32))


COND_ROWS = 16


def _adaln_kernel(cond_ref, w_ref, b_ref, o_ref):
    cnd = cond_ref[...]
    a = (cnd * _sigmoid(cnd)).astype(BF16)
    o_ref[0] = _dot(a, w_ref[0].astype(BF16)) + b_ref[0]


def _adaln_call(cond, w_mod, b_mod, tn=1024):
    depth, D, N = w_mod.shape
    return pl.pallas_call(
        _adaln_kernel,
        grid=(depth, N // tn),
        in_specs=[pl.BlockSpec((COND_ROWS, D), lambda l, j: (0, 0)),
                  pl.BlockSpec((1, D, tn), lambda l, j: (l, 0, j)),
                  pl.BlockSpec((1, 1, tn), lambda l, j: (l, 0, j))],
        out_specs=pl.BlockSpec((1, COND_ROWS, tn), lambda l, j: (l, 0, j)),
        out_shape=jax.ShapeDtypeStruct((depth, COND_ROWS, N), F32),
        compiler_params=_params(("parallel", "parallel")),
        name="adaln",
    )(cond, w_mod, b_mod.reshape(depth, 1, N))


def _cond_row(i, tm, n_lat_rows, L, B):
    return jnp.where(i * tm < n_lat_rows, (i * tm) // L, B)


def _layer_norm_rows(x):
    mu = jnp.mean(x, -1, keepdims=True)
    xc = x - mu
    var = jnp.mean(xc * xc, -1, keepdims=True)
    return xc * lax.rsqrt(var + LN_EPS)


def _ln_mod_kernel(x_ref, mod_ref, o_ref, *, tm, n_lat_rows, L, B, D):
    r = _cond_row(pl.program_id(0), tm, n_lat_rows, L, B)
    shift = mod_ref[pl.ds(r, 1), 0:D]
    scale = mod_ref[pl.ds(r, 1), D:2 * D]
    o_ref[...] = (_layer_norm_rows(x_ref[...]) * (1.0 + scale) + shift).astype(o_ref.dtype)


def _ln_mod_call(h, mod_l, B, L, tm=256):
    T, D = h.shape
    kern = functools.partial(_ln_mod_kernel, tm=tm, n_lat_rows=B * L, L=L, B=B, D=D)
    return pl.pallas_call(
        kern,
        grid=(T // tm,),
        in_specs=[pl.BlockSpec((tm, D), lambda i: (i, 0)), pl.BlockSpec(mod_l.shape, lambda i: (0, 0))],
        out_specs=pl.BlockSpec((tm, D), lambda i: (i, 0)),
        out_shape=jax.ShapeDtypeStruct((T, D), BF16),
        compiler_params=_params(("parallel",)),
        name="ln_modulate",
    )(h, mod_l)


def _wstat_matmul_kernel(*refs, widths):
    n = len(widths)
    a_refs, w_ref, o_ref, wbf_ref = refs[:n], refs[n], refs[n + 1], refs[n + 2]

    @pl.when(pl.program_id(1) == 0)
    def _():
        wbf_ref[...] = w_ref[0].astype(BF16)

    acc = None
    k0 = 0
    for a_ref, kw in zip(a_refs, widths):
        part = _dot(a_ref[...], wbf_ref[k0:k0 + kw, :])
        acc = part if acc is None else acc + part
        k0 += kw
    o_ref[...] = acc.astype(o_ref.dtype)


def _wstat_matmul_call(a_list, w_all, layer, n_rows, tm, tn, name):
    widths = tuple(a.shape[1] for a in a_list)
    _, K, N = w_all.shape
    assert sum(widths) == K and n_rows % tm == 0 and N % tn == 0
    return pl.pallas_call(
        functools.partial(_wstat_matmul_kernel, widths=widths),
        grid=(N // tn, n_rows // tm),
        in_specs=[pl.BlockSpec((tm, kw), lambda j, i: (i, 0)) for kw in widths]
        + [pl.BlockSpec((1, K, tn), lambda j, i: (layer, 0, j))],
        out_specs=pl.BlockSpec((tm, tn), lambda j, i: (i, j)),
        out_shape=jax.ShapeDtypeStruct((n_rows, N), BF16),
        scratch_shapes=[pltpu.VMEM((K, tn), BF16)],
        compiler_params=_params(("parallel", "arbitrary")),
        name=name,
    )(*a_list, w_all)


def _split_bf16(x):
    hi = x.astype(BF16)
    return hi, (x - hi.astype(F32)).astype(BF16)


def _post_mix_kernel(y_ref, h_ref, mod_ref, g_ref, b_ref, rw_ref, rb_ref,
                     h1_ref, fin_ref, gates_ref, topi_ref, *, tm, n_lat_rows, L, B, D, alpha):
    r = _cond_row(pl.program_id(0), tm, n_lat_rows, L, B)
    mrow = lambda slot: mod_ref[pl.ds(r, 1), slot * D:(slot + 1) * D]
    u = alpha * h_ref[...] + mrow(2) * y_ref[...].astype(F32)
    h1 = _layer_norm_rows(u) * g_ref[...] + b_ref[...]
    h1_ref[...] = h1
    fin = _layer_norm_rows(h1) * (1.0 + mrow(4)) + mrow(3)
    fin_ref[...] = fin.astype(BF16)
    f_hi, f_lo = _split_bf16(fin)
    w_hi, w_lo = _split_bf16(rw_ref[...])
    logits = _dot(f_hi, w_hi) + _dot(f_hi, w_lo) + _dot(f_lo, w_hi) + rb_ref[...]
    lane = lax.broadcasted_iota(jnp.int32, logits.shape, 1)
    ninf = jnp.float32(-jnp.inf)
    xs = jnp.where(lane < N_EXPERTS, logits, ninf)
    sel_v, sel_i = [], []
    for _ in range(TOP_K):
        m = jnp.max(xs, -1, keepdims=True)
        idx = jnp.min(jnp.where(xs == m, lane, logits.shape[1]), -1, keepdims=True)
        sel_v.append(m)
        sel_i.append(idx)
        xs = jnp.where(lane == idx, ninf, xs)
    ex = [jnp.exp(v - sel_v[0]) for v in sel_v]
    den = ex[0]
    for e in ex[1:]:
        den = den + e
    gates = jnp.zeros(logits.shape, F32)
    topi = jnp.zeros(logits.shape, jnp.int32)
    for j in range(TOP_K):
        gates = jnp.where(lane == sel_i[j], ex[j] / den, gates)
        topi = jnp.where(lane == j, sel_i[j], topi)
    gates_ref[...] = gates
    topi_ref[...] = topi


def _post_mix_call(y, h, mod_l, ln_g, ln_b, rw_pad, rb_pad, B, L, alpha, tm=256):
    n_out, D = y.shape
    NE = rw_pad.shape[1]
    kern = functools.partial(_post_mix_kernel, tm=tm, n_lat_rows=B * L, L=L, B=B, D=D, alpha=alpha)
    row = lambda i: (i, 0)
    fixed = lambda i: (0, 0)
    return pl.pallas_call(
        kern,
        grid=(n_out // tm,),
        in_specs=[pl.BlockSpec((tm, D), row), pl.BlockSpec((tm, D), row),
                  pl.BlockSpec(mod_l.shape, fixed),
                  pl.BlockSpec((1, D), fixed), pl.BlockSpec((1, D), fixed),
                  pl.BlockSpec((D, NE), fixed), pl.BlockSpec((1, NE), fixed)],
        out_specs=[pl.BlockSpec((tm, D), row), pl.BlockSpec((tm, D), row),
                   pl.BlockSpec((tm, NE), row), pl.BlockSpec((tm, NE), row)],
        out_shape=[jax.ShapeDtypeStruct((n_out, D), F32), jax.ShapeDtypeStruct((n_out, D), BF16),
                   jax.ShapeDtypeStruct((n_out, NE), F32), jax.ShapeDtypeStruct((n_out, NE), jnp.int32)],
        compiler_params=_params(("parallel",)),
        name="post_mix_router",
    )(y, h, mod_l, ln_g.reshape(1, D), ln_b.reshape(1, D), rw_pad, rb_pad)


MOE_TILE = 512
MOE_ALIGN = 16


def _first_visit_of_expert(v, ve_ref):
    return (v == 0) | (ve_ref[jnp.maximum(v - 1, 0)] != ve_ref[v])


def _moe_up_kernel(start_ref, ve_ref, vout_ref, vv_ref, x_ref, w1_ref, b1_ref, o_ref, wbf_ref, *, DE):
    v = pl.program_id(0)

    @pl.when(_first_visit_of_expert(v, ve_ref))
    def _():
        wbf_ref[...] = w1_ref[0, 0].astype(BF16)

    @pl.when(vv_ref[v] > 0)
    def _():
        hcat = _dot(x_ref[...], wbf_ref[...]) + b1_ref[0, 0]
        gate = jnp.minimum(hcat[:, :DE], SWIGLU_LIMIT)
        up = jnp.clip(hcat[:, DE:], -SWIGLU_LIMIT, SWIGLU_LIMIT)
        o_ref[...] = ((up + 1.0) * (gate * _sigmoid(SWIGLU_ALPHA * gate))).astype(o_ref.dtype)


def _moe_down_kernel(start_ref, ve_ref, vout_ref, vv_ref, h_ref, rw_ref, w2_ref, b2_ref, o_ref, wbf_ref):
    v = pl.program_id(0)

    @pl.when(_first_visit_of_expert(v, ve_ref))
    def _():
        wbf_ref[...] = w2_ref[0, 0].astype(BF16)

    @pl.when(vv_ref[v] > 0)
    def _():
        y = (_dot(h_ref[...], wbf_ref[...]) + b2_ref[0, 0]) * rw_ref[...]
        o_ref[...] = y.astype(o_ref.dtype)


def _moe_call(xs, row_w, visits, w1_all, b1_all, w2_all, b2_all, layer):
    n_in, D = xs.shape
    _, E, _, DE2 = w1_all.shape
    DE = DE2 // 2
    tm = MOE_TILE
    nv = visits[0].shape[0]
    window = lambda v, st, ve, vo, vv: (pl.multiple_of(st[v], MOE_ALIGN), 0)
    out_tile = lambda v, st, ve, vo, vv: (vo[v], 0)
    expert = lambda v, st, ve, vo, vv: (layer, ve[v], 0, 0)
    hdn = pl.pallas_call(
        functools.partial(_moe_up_kernel, DE=DE),
        grid_spec=pltpu.PrefetchScalarGridSpec(
            num_scalar_prefetch=4, grid=(nv,),
            in_specs=[pl.BlockSpec((pl.Element(tm), pl.Element(D)), window),
                      pl.BlockSpec((1, 1, D, DE2), expert),
                      pl.BlockSpec((1, 1, 1, DE2), expert)],
            out_specs=pl.BlockSpec((tm, DE), out_tile),
            scratch_shapes=[pltpu.VMEM((D, DE2), BF16)]),
        out_shape=jax.ShapeDtypeStruct((nv * tm, DE), BF16),
        compiler_params=_params(("arbitrary",)),
        name="moe_up",
    )(*visits, xs, w1_all, b1_all.reshape(b1_all.shape[0], E, 1, DE2))
    return pl.pallas_call(
        _moe_down_kernel,
        grid_spec=pltpu.PrefetchScalarGridSpec(
            num_scalar_prefetch=4, grid=(nv,),
            in_specs=[pl.BlockSpec((tm, DE), out_tile),
                      pl.BlockSpec((pl.Element(tm), pl.Element(1)), window),
                      pl.BlockSpec((1, 1, DE, D), expert),
                      pl.BlockSpec((1, 1, 1, D), expert)],
            out_specs=pl.BlockSpec((tm, D), out_tile),
            scratch_shapes=[pltpu.VMEM((DE, D), BF16)]),
        out_shape=jax.ShapeDtypeStruct((nv * tm, D), BF16),
        compiler_params=_params(("arbitrary",)),
        name="moe_down",
    )(*visits, hdn, row_w, w2_all, b2_all.reshape(b2_all.shape[0], E, 1, D))


def _moe_dispatch(topi, gates, tm):
    T = topi.shape[0]
    E = N_EXPERTS
    n = T * TOP_K
    al = MOE_ALIGN
    flat_e = topi.reshape(-1)
    flat_w = jnp.take_along_axis(gates, topi, axis=1).reshape(-1)
    order = jnp.argsort(flat_e, stable=True).astype(jnp.int32)
    inv = jnp.argsort(order).astype(jnp.int32)
    experts = jnp.arange(E, dtype=jnp.int32)
    counts = jnp.sum((flat_e[:, None] == experts[None, :]).astype(jnp.int32), axis=0)
    offs = jnp.cumsum(counts) - counts
    seg_len = ((counts + al - 1) // al) * al
    seg_end = jnp.cumsum(seg_len)
    seg_start = seg_end - seg_len
    n_in = n + E * al + tm
    p = jnp.arange(n_in, dtype=jnp.int32)
    e_of_p = jnp.minimum(jnp.sum((p[:, None] >= seg_end[None, :]).astype(jnp.int32), axis=1), E - 1)
    i_of_p = p - seg_start[e_of_p]
    real = i_of_p < counts[e_of_p]
    src = order[jnp.where(real, offs[e_of_p] + i_of_p, 0)]
    row_tok = jnp.where(real, src // TOP_K, 0)
    row_w = jnp.where(real, flat_w[src], 0.0).reshape(n_in, 1)
    n_tiles_e = (counts + tm - 1) // tm
    tile_end = jnp.cumsum(n_tiles_e)
    tile_start = tile_end - n_tiles_e
    total = tile_end[-1]
    nv = n // tm + E
    v = jnp.arange(nv, dtype=jnp.int32)
    ve = jnp.minimum(jnp.sum((v[:, None] >= tile_end[None, :]).astype(jnp.int32), axis=1), E - 1)
    valid = v < total
    last = total - 1
    ve = jnp.where(valid, ve, ve[last]).astype(jnp.int32)
    vout = jnp.where(valid, v, last).astype(jnp.int32)
    start = (seg_start[ve] + (vout - tile_start[ve]) * tm).astype(jnp.int32)
    pos = tile_start[flat_e] * tm + (inv - offs[flat_e])
    pos_kmajor = pos.reshape(T, TOP_K).T.reshape(-1)
    return row_tok, row_w, pos_kmajor, (start, ve, vout, valid.astype(jnp.int32))


def _final_kernel(h_ref, *refs, tm, n_lat_rows, L, B, D, alpha, with_next):
    y_refs, rest = refs[:TOP_K], refs[TOP_K:]
    r = _cond_row(pl.program_id(0), tm, n_lat_rows, L, B)
    gate = rest[0][pl.ds(r, 1), 5 * D:6 * D]
    f = y_refs[0][...].astype(F32)
    for y_ref in y_refs[1:]:
        f = f + y_ref[...].astype(F32)
    u = alpha * h_ref[...] + gate * f
    h2 = _layer_norm_rows(u) * rest[1][...] + rest[2][...]
    if with_next:
        modn_ref, o_ref, a_ref = rest[3:]
        a_ref[...] = (_layer_norm_rows(h2) * (1.0 + modn_ref[pl.ds(r, 1), D:2 * D])
                      + modn_ref[pl.ds(r, 1), 0:D]).astype(a_ref.dtype)
    else:
        o_ref = rest[3]
    o_ref[...] = h2


def _final_call(h1, y_assign, mod_l, ln_g, ln_b, B, L, alpha, mod_next=None, tm=256):
    n, D = h1.shape
    nt = n // tm
    with_next = mod_next is not None
    kern = functools.partial(_final_kernel, tm=tm, n_lat_rows=B * L, L=L, B=B, D=D, alpha=alpha,
                             with_next=with_next)
    row = lambda i: (i, 0)
    fixed = lambda i: (0, 0)
    y_specs = [pl.BlockSpec((tm, D), functools.partial(lambda i, k: (k * nt + i, 0), k=k)) for k in range(TOP_K)]
    mod_spec = pl.BlockSpec(mod_l.shape, fixed)
    row_spec = pl.BlockSpec((tm, D), row)
    return pl.pallas_call(
        kern,
        grid=(nt,),
        in_specs=[row_spec] + y_specs + [mod_spec, pl.BlockSpec((1, D), fixed), pl.BlockSpec((1, D), fixed)]
        + ([mod_spec] if with_next else []),
        out_specs=[row_spec, row_spec] if with_next else row_spec,
        out_shape=([jax.ShapeDtypeStruct((n, D), F32), jax.ShapeDtypeStruct((n, D), BF16)] if with_next
                   else jax.ShapeDtypeStruct((n, D), F32)),
        compiler_params=_params(("parallel",)),
        name="final_ln",
    )(h1, *([y_assign] * TOP_K), mod_l, ln_g.reshape(1, D), ln_b.reshape(1, D),
      *([mod_next] if with_next else []))


def kernel(x, c, ctx, c_ctx, w_mod, b_mod, w_in, pool_w, pool_scale, na_rpb, hg_lb, hg_norm_g, w_out, ln1_g, ln1_b, ln2_g, ln2_b, router_w, router_b, exp_w1, exp_b1, exp_w2, exp_b2):
    B, L, D = x.shape
    Lc = ctx.shape[1]
    depth = w_in.shape[0]
    alpha = (2 * depth) ** 0.25
    pool_width = pool_w.shape[1] * pool_w.shape[2]
    na_width = na_rpb.shape[1] * HEAD_DIM
    hg_width = hg_norm_g.shape[1]
    na_heads, hg_heads = na_width // HEAD_DIM, hg_width // HEAD_DIM
    c_naq = pool_width // HEAD_DIM
    c_nak, c_nav = c_naq + na_heads, c_naq + 2 * na_heads
    c_hq = c_naq + 3 * na_heads
    c_hff, c_hfb, c_hi, c_hg = (c_hq + k * hg_heads for k in range(1, 5))
    assert B + 1 <= COND_ROWS

    hall = jnp.concatenate([x.reshape(B * L, D), ctx.reshape(B * Lc, D)], axis=0)
    cond = jnp.concatenate([c, c_ctx[None], jnp.zeros((COND_ROWS - B - 1, D), F32)], axis=0)
    mod = _adaln_call(cond, w_mod, b_mod)
    lb_soft = jax.nn.softmax(hg_lb.astype(F32), axis=1)
    lower = jnp.cumsum(lb_soft, axis=1) - lb_soft[:, :1]
    rw_pad = jnp.pad(router_w, ((0, 0), (0, 0), (0, HEAD_DIM - N_EXPERTS)))
    rb_pad = jnp.pad(router_b, ((0, 0), (0, HEAD_DIM - N_EXPERTS)))

    T = hall.shape[0]
    n_row_tiles = 8
    a_in = _ln_mod_call(hall, mod[0], B, L)
    for l in range(depth):
        last = l == depth - 1
        px = _wstat_matmul_call([a_in], w_in, l, T, T // n_row_tiles, 512, "in_proj")
        a_mix = _pool_call(px, pool_w[l].astype(BF16), pool_scale[l], B, L, Lc)
        b_x, b_c = _na_call(px, na_rpb[l], B, L, Lc, c_naq, c_nak, c_nav, na_heads)
        b_mix = jnp.concatenate([b_x, b_c], axis=0)
        o_f, o_b = _hgrn_call(px, lower[0, l].reshape(hg_heads, 1, HEAD_DIM),
                              lower[1, l].reshape(hg_heads, 1, HEAD_DIM), B, L, Lc,
                              c_hq, c_hff, c_hfb, c_hi, hg_heads)
        c_mix = _readout_call(o_f, o_b, px, hg_norm_g[l], c_hg, T // n_row_tiles)
        n_out = B * L if last else T
        y = _wstat_matmul_call([a_mix, b_mix, c_mix], w_out, l, n_out, n_out // n_row_tiles, 512, "out_proj")
        h1, fin, gates, topi = _post_mix_call(y, hall, mod[l], ln1_g[l], ln1_b[l], rw_pad[l],
                                              rb_pad[l].reshape(1, -1), B, L, alpha)
        sorted_tok, sorted_w, pos_kmajor, visits = _moe_dispatch(topi[:, :TOP_K], gates, MOE_TILE)
        xs = fin.at[sorted_tok].get(mode="promise_in_bounds")
        ys = _moe_call(xs, sorted_w, visits, exp_w1, exp_b1, exp_w2, exp_b2, l)
        y_assign = ys.at[pos_kmajor].get(mode="promise_in_bounds")
        if last:
            hall = _final_call(h1, y_assign, mod[l], ln2_g[l], ln2_b[l], B, L, alpha)
        else:
            hall, a_in = _final_call(h1, y_assign, mod[l], ln2_g[l], ln2_b[l], B, L, alpha, mod_next=mod[l + 1])
    return hall.reshape(B, L, D)
```

```python
import functools

import jax
import jax.numpy as jnp
import numpy as np
from jax import lax
from jax.experimental import pallas as pl
from jax.experimental.pallas import tpu as pltpu

F32 = jnp.float32
BF16 = jnp.bfloat16

HEAD_DIM = 128
GRID_W = 64
POOL_WINDOWS = (2, 4, 8, 16)
NA_ROWS = 8
NA_COLS = 16
FORGET_EPS = 1e-20
N_EXPERTS = 32
TOP_K = 4
SWIGLU_LIMIT = 7.0
SWIGLU_ALPHA = 1.702
ROPE_BASE = 10000.0
LN_EPS = 1e-6

VMEM_LIMIT_BYTES = 56 * 1024 * 1024
HG_CHUNK = 256
NEG_BIG = -1e30


def _params(sem):
    return pltpu.CompilerParams(dimension_semantics=sem, vmem_limit_bytes=VMEM_LIMIT_BYTES)


def _dot(a, b):
    return jnp.dot(a, b, preferred_element_type=F32)


def _dot_nt(a, b):
    return lax.dot_general(a, b, (((1,), (1,)), ((), ())), preferred_element_type=F32)


def _sigmoid(x):
    return 1.0 / (1.0 + jnp.exp(-x))


LOG2E = 1.4426950408889634
HG_HALF = HG_CHUNK // 2
HG_HEADS_PER_STEP = 2


def _neg_abs(x):
    bits = lax.bitcast_convert_type(x, jnp.uint32) | jnp.uint32(0x80000000)
    return lax.bitcast_convert_type(bits, F32)


def _hgrn_half_scores(q, k, b2, g2, fc, lev, reverse):
    n = q.shape[0]
    row = lax.broadcasted_iota(jnp.int32, (n, HEAD_DIM), 0)
    a = jnp.where(lev == -1, _dot_nt(q.astype(BF16), k.astype(BF16)), 0.0)
    h, lvl = 1, 0
    while h < n:
        if h < 8:
            upper = (row & h) != 0
            t_role = jnp.logical_not(upper) if reverse else upper
            if h == 1:
                x = jnp.where(t_role, q * fc, k)
            elif h == 2:
                r4 = row & 3
                g_prev, g_next = pltpu.roll(g2, 1, 0), pltpu.roll(g2, n - 1, 0)
                if reverse:
                    e = jnp.where(r4 == 0, g2 + g_next, jnp.where(r4 == 1, g2, jnp.where(r4 == 2, 0.0, g_prev)))
                else:
                    e = jnp.where(r4 == 0, g_next, jnp.where(r4 == 1, 0.0, jnp.where(r4 == 2, g2, g2 + g_prev)))
                x = jnp.where(t_role, q, k) * jnp.exp2(e)
            else:
                m0 = h if reverse else h - 1
                b3 = b2.reshape(n // (2 * h), 2 * h, HEAD_DIM)
                ref = jnp.broadcast_to(b3[:, m0:m0 + 1, :], b3.shape).reshape(n, HEAD_DIM)
                x = jnp.where(t_role, q, k) * jnp.exp2(_neg_abs(b2 - ref))
            x = x.astype(BF16)
            a = jnp.where(lev == lvl, _dot_nt(x, x), a)
        else:
            nb = n // (2 * h)
            q4 = q.reshape(nb, 2, h, HEAD_DIM)
            k4 = k.reshape(nb, 2, h, HEAD_DIM)
            b4 = b2.reshape(nb, 2, h, HEAD_DIM)
            s_half, t_half, m_row = (1, 0, 0) if reverse else (0, 1, h - 1)
            ref = b4[:, s_half:s_half + 1, m_row:m_row + 1, :]
            xt = q4[:, t_half] * jnp.exp2(b4[:, t_half] - ref[:, 0])
            xs = k4[:, s_half] * jnp.exp2(ref[:, 0] - b4[:, s_half])
            halves = [xt, xs] if reverse else [xs, xt]
            x = jnp.stack(halves, axis=1).reshape(n, HEAD_DIM).astype(BF16)
            g = _dot_nt(xt.reshape(n // 2, HEAD_DIM).astype(BF16), x)
            a4 = a.reshape(nb, 2, h, n)
            lev_t = lev.reshape(nb, 2, h, n)[:, t_half]
            a_t = jnp.where(lev_t == lvl, g.reshape(nb, h, n), a4[:, t_half])
            halves = [a_t, a4[:, s_half]] if reverse else [a4[:, s_half], a_t]
            a = jnp.stack(halves, axis=1).reshape(n, n)
        h *= 2
        lvl += 1
    return a


def _hgrn_direction(q, z, v_bf, lb, st_ref, lev, tri_bf, reverse):
    C, H = HG_CHUNK, HG_HALF
    q = q * _sigmoid(q)
    sig = _sigmoid(z)
    fc = jnp.maximum(lb + (1.0 - lb) * sig, FORGET_EPS)
    g2 = jnp.log(fc) * LOG2E
    k = (1.0 - lb) * (1.0 - sig)

    g_hi = g2.astype(BF16)
    r1 = g2 - g_hi.astype(F32)
    g_mid = r1.astype(BF16)
    g_lo = (r1 - g_mid.astype(F32)).astype(BF16)
    b2 = _dot(tri_bf, g_hi) + _dot(tri_bf, g_mid) + _dot(tri_bf, g_lo)

    lo, hi = slice(0, H), slice(H, C)
    a_lo = _hgrn_half_scores(q[lo], k[lo], b2[lo], g2[lo], fc[lo], lev, reverse).astype(BF16)
    a_hi = _hgrn_half_scores(q[hi], k[hi], b2[hi], g2[hi], fc[hi], lev, reverse).astype(BF16)
    t_sl, s_sl, m = (lo, hi, H) if reverse else (hi, lo, H - 1)
    ref = b2[m:m + 1, :]
    qt = (q[t_sl] * jnp.exp2(b2[t_sl] - ref)).astype(BF16)
    ks = (k[s_sl] * jnp.exp2(ref - b2[s_sl])).astype(BF16)
    cross = _dot(_dot_nt(qt, ks).astype(BF16), v_bf[s_sl])
    o_lo = _dot(a_lo, v_bf[lo])
    o_hi = _dot(a_hi, v_bf[hi])
    o_intra = jnp.concatenate([o_lo + cross, o_hi] if reverse else [o_lo, o_hi + cross], axis=0)

    st = st_ref[...]
    tot = b2[0:1, :] if reverse else b2[C - 1:C, :]
    qdec = (q * jnp.exp2(b2)).astype(BF16)
    o_inter = _dot_nt(qdec, st.astype(BF16))
    kdec = (k * jnp.exp2(tot - b2)).astype(BF16)
    vt = v_bf.astype(F32).T.astype(BF16)
    st_ref[...] = jnp.exp2(tot) * st + _dot(vt, kdec)
    return o_intra + o_inter


def _hgrn_kernel(qf_ref, zf_ref, vf_ref, qb_ref, zb_ref, vb_ref, lbf_ref, lbb_ref, levf_ref, levb_ref,
                 trif_ref, trib_ref, of_ref, ob_ref, sf_ref, sb_ref):
    @pl.when(pl.program_id(2) == 0)
    def _():
        sf_ref[...] = jnp.zeros_like(sf_ref)
        sb_ref[...] = jnp.zeros_like(sb_ref)

    for hh in range(HG_HEADS_PER_STEP):
        sl = slice(hh * HEAD_DIM, (hh + 1) * HEAD_DIM)
        of_ref[:, sl] = _hgrn_direction(qf_ref[:, sl].astype(F32), zf_ref[:, sl].astype(F32), vf_ref[:, sl],
                                        lbf_ref[0][:, sl], sf_ref.at[hh], levf_ref[...], trif_ref[...],
                                        False).astype(of_ref.dtype)
        ob_ref[:, sl] = _hgrn_direction(qb_ref[:, sl].astype(F32), zb_ref[:, sl].astype(F32), vb_ref[:, sl],
                                        lbb_ref[0][:, sl], sb_ref.at[hh], levb_ref[...], trib_ref[...],
                                        True).astype(ob_ref.dtype)


def _hgrn_call(px, lb_f, lb_b, B, L, Lc, col_q, col_ff, col_fb, col_i, n_heads):
    C, H = HG_CHUNK, HG_HALF
    T = px.shape[0]
    nlc, ncc = L // C, Lc // C
    ns = nlc + ncc
    ctx0 = B * nlc

    def fwd_row(b, s):
        return jnp.where(s < ncc, ctx0 + b * ncc + s, b * nlc + s - ncc)

    def bwd_row(b, s):
        return jnp.where(s < ncc, ctx0 + b * ncc + (ncc - 1 - s), b * nlc + (nlc - 1 - (s - ncc)))

    G = HG_HEADS_PER_STEP
    GW = G * HEAD_DIM
    assert n_heads % G == 0 and all(c % G == 0 for c in (col_q, col_ff, col_fb, col_i))

    def spec(row_fn, col):
        return pl.BlockSpec((C, GW), lambda b, h, s: (row_fn(b, s), col // G + h))

    idx = np.arange(H)
    xr = idx[:, None] ^ idx[None, :]
    level = np.floor(np.log2(np.maximum(xr, 1))).astype(np.int32)
    lev_f = np.where(xr == 0, -1, np.where(idx[:, None] > idx[None, :], level, -2)).astype(np.int32)
    lev_b = np.ascontiguousarray(lev_f.T)
    idc = np.arange(C)
    tri_f = (idc[:, None] >= idc[None, :]).astype(np.float32)
    tri_b = (idc[:, None] <= idc[None, :]).astype(np.float32)
    const = lambda shape: pl.BlockSpec(shape, lambda b, h, s: (0,) * len(shape))
    lb_spec = pl.BlockSpec((1, 1, GW), lambda b, h, s: (h, 0, 0))
    out_sds = jax.ShapeDtypeStruct((T, n_heads * HEAD_DIM), BF16)
    state = pltpu.VMEM((G, HEAD_DIM, HEAD_DIM), F32)
    return pl.pallas_call(
        _hgrn_kernel,
        grid=(B, n_heads // G, ns),
        in_specs=[spec(fwd_row, col_q), spec(fwd_row, col_ff), spec(fwd_row, col_i),
                  spec(bwd_row, col_q), spec(bwd_row, col_fb), spec(bwd_row, col_i),
                  lb_spec, lb_spec, const((H, H)), const((H, H)), const((C, C)), const((C, C))],
        out_specs=[pl.BlockSpec((C, GW), lambda b, h, s: (fwd_row(b, s), h)),
                   pl.BlockSpec((C, GW), lambda b, h, s: (bwd_row(b, s), h))],
        out_shape=[out_sds, out_sds],
        scratch_shapes=[state, state],
        compiler_params=_params(("parallel", "parallel", "arbitrary")),
        name="hgrn_scan",
    )(px, px, px, px, px, px, lb_f.reshape(n_heads // G, 1, GW), lb_b.reshape(n_heads // G, 1, GW),
      jnp.asarray(lev_f), jnp.asarray(lev_b), jnp.asarray(tri_f, BF16), jnp.asarray(tri_b, BF16))


NA_QROWS = 4
NA_KROWS = NA_QROWS + NA_ROWS


def _na_block_cases(rows):
    assert rows % NA_QROWS == 0 and rows >= 2 * NA_ROWS
    nb = rows // NA_QROWS
    cases = []
    for i in (0, 1, nb - 1):
        r0 = NA_QROWS * i
        ks = min(max(r0 - NA_ROWS // 2, 0), rows - NA_KROWS)
        per_q = []
        for qr in range(NA_QROWS):
            r = r0 + qr
            rs = min(max(r - NA_ROWS // 2, 0), rows - NA_ROWS)
            per_q.append((rs - ks, ks - r + NA_ROWS - 1))
        cases.append(per_q)
    return cases


def _na_kernel(rpb_ref, q_ref, k_ref, v_ref, qc_ref, kc_ref, vc_ref, cos_ref, sin_ref,
               ox_ref, oc_ref, qr_ref, kr_ref, tb_ref, bias_ref, *, rows):
    h = pl.program_id(0)
    L = q_ref.shape[0]
    W = GRID_W
    scale = HEAD_DIM ** -0.5
    n_dr = 2 * NA_ROWS - 1
    n_dc = 2 * NA_COLS - 1

    @pl.when(pl.program_id(1) == 0)
    def _():
        c_i = lax.broadcasted_iota(jnp.int32, (W, W), 0)
        kc_i = lax.broadcasted_iota(jnp.int32, (W, W), 1)
        cs = jnp.clip(c_i - NA_COLS // 2, 0, W - NA_COLS)
        in_win = (kc_i >= cs) & (kc_i < cs + NA_COLS)
        c_off = kc_i - c_i + (NA_COLS - 1)
        for dr in range(n_dr):
            acc = jnp.zeros((W, W), F32)
            for j in range(n_dc):
                acc = jnp.where(c_off == j, rpb_ref[h, dr, j], acc)
            tb_ref[dr] = jnp.where(in_win, acc, NEG_BIG)
        neg = jnp.full((W, W), NEG_BIG, F32)
        for ci, per_q in enumerate(_na_block_cases(rows)):
            for qr, (j0, dr0) in enumerate(per_q):
                for j in range(NA_KROWS):
                    valid = j0 <= j < j0 + NA_ROWS
                    blk = tb_ref[dr0 + j] if valid else neg
                    bias_ref[ci, qr * W:(qr + 1) * W, j * W:(j + 1) * W] = blk

    RC = 256
    lane = lax.broadcasted_iota(jnp.int32, (RC, HEAD_DIM), 1)
    first = (lane & (HEAD_DIM // 2 - 1)) < HEAD_DIM // 4

    def rope_chunk(i, carry):
        sl = pl.ds(pl.multiple_of(i * RC, RC), RC)
        cos = cos_ref[sl, :]
        sin = sin_ref[sl, :]
        for src, dst, mul in ((q_ref, qr_ref, scale), (k_ref, kr_ref, 1.0)):
            t = src[sl, :].astype(F32)
            partner = jnp.where(first, pltpu.roll(t, HEAD_DIM - HEAD_DIM // 4, 1), pltpu.roll(t, HEAD_DIM // 4, 1))
            dst[sl, :] = ((t * cos + partner * sin) * mul).astype(BF16)
        return carry

    lax.fori_loop(0, L // RC, rope_chunk, 0)

    kc = kc_ref[...]
    vc = vc_ref[...]
    nb = rows // NA_QROWS
    QB = NA_QROWS * W
    KB = NA_KROWS * W

    def block(i, carry):
        ks = jnp.clip(i * NA_QROWS - NA_ROWS // 2, 0, rows - NA_KROWS)
        case = jnp.where(i == 0, 0, jnp.where(i == nb - 1, 2, 1))
        qs = pl.ds(pl.multiple_of(i * QB, QB), QB)
        kslice = pl.ds(pl.multiple_of(ks * W, W), KB)
        qb = qr_ref[qs, :]
        s_loc = _dot_nt(qb, kr_ref[kslice, :]) + bias_ref[case]
        s_ctx = _dot_nt(qb, kc)
        m = jnp.maximum(jnp.max(s_loc, -1, keepdims=True), jnp.max(s_ctx, -1, keepdims=True))
        p_loc = jnp.exp(s_loc - m)
        p_ctx = jnp.exp(s_ctx - m)
        den = jnp.sum(p_loc, -1, keepdims=True) + jnp.sum(p_ctx, -1, keepdims=True)
        o = _dot(p_loc.astype(BF16), v_ref[kslice, :]) + _dot(p_ctx.astype(BF16), vc)
        ox_ref[qs, :] = (o / den).astype(ox_ref.dtype)
        return carry

    lax.fori_loop(0, nb, block, 0, unroll=2)

    qc = (qc_ref[...].astype(F32) * scale).astype(BF16)
    s = _dot_nt(qc, kc)
    p = jnp.exp(s - jnp.max(s, -1, keepdims=True))
    oc = _dot(p.astype(BF16), vc) / jnp.sum(p, -1, keepdims=True)
    oc_ref[...] = oc.astype(oc_ref.dtype)


def _rope_tables(L):
    pos = jnp.arange(L)
    half = HEAD_DIM // 2
    inv = ROPE_BASE ** (-jnp.arange(0, half, 2, dtype=F32) / half)

    def tab(p):
        ang = p.astype(F32)[:, None] * inv[None]
        cos, sin = jnp.cos(ang), jnp.sin(ang)
        return jnp.concatenate([cos, cos], -1), jnp.concatenate([-sin, sin], -1)

    c_r, s_r = tab(pos // GRID_W)
    c_c, s_c = tab(pos % GRID_W)
    return jnp.concatenate([c_r, c_c], -1), jnp.concatenate([s_r, s_c], -1)


def _na_call(px, rpb, B, L, Lc, col_q, col_k, col_v, n_heads):
    rows = L // GRID_W
    cos, sin = _rope_tables(L)
    ctx0 = B * L // Lc

    def lat(col):
        return pl.BlockSpec((L, HEAD_DIM), lambda h, b: (b, col + h))

    def cx(col):
        return pl.BlockSpec((Lc, HEAD_DIM), lambda h, b: (ctx0 + b, col + h))

    tab = pl.BlockSpec((L, HEAD_DIM), lambda h, b: (0, 0))
    W = GRID_W
    return pl.pallas_call(
        functools.partial(_na_kernel, rows=rows),
        grid=(n_heads, B),
        in_specs=[pl.BlockSpec(memory_space=pltpu.SMEM), lat(col_q), lat(col_k), lat(col_v),
                  cx(col_q), cx(col_k), cx(col_v), tab, tab],
        out_specs=[pl.BlockSpec((L, HEAD_DIM), lambda h, b: (b, h)),
                   pl.BlockSpec((Lc, HEAD_DIM), lambda h, b: (b, h))],
        out_shape=[jax.ShapeDtypeStruct((B * L, n_heads * HEAD_DIM), BF16),
                   jax.ShapeDtypeStruct((B * Lc, n_heads * HEAD_DIM), BF16)],
        scratch_shapes=[pltpu.VMEM((L, HEAD_DIM), BF16), pltpu.VMEM((L, HEAD_DIM), BF16),
                        pltpu.VMEM((2 * NA_ROWS - 1, W, W), F32),
                        pltpu.VMEM((3, NA_QROWS * W, NA_KROWS * W), F32)],
        compiler_params=_params(("parallel", "arbitrary")),
        name="neighborhood_attention",
    )(rpb, px, px, px, px, px, px, cos, sin)


POOL_TILE = 256
POOL_HALO = 16


def _pool_kernel(prev_ref, cur_ref, next_ref, w_ref, scale_ref, o_ref, *, n_lat_tiles, nl, nc):
    i = pl.program_id(0)
    is_lat = i < n_lat_tiles
    j = jnp.where(is_lat, i % nl, (i - n_lat_tiles) % nc)
    n = jnp.where(is_lat, nl, nc)
    TQ, HL = POOL_TILE, POOL_HALO
    lo_min = jnp.where(j > 0, -HL, 0)
    hi_max = jnp.where(j < n - 1, TQ + HL, TQ)
    src = jnp.concatenate([prev_ref[...], cur_ref[...], next_ref[...]], axis=0)
    t = lax.broadcasted_iota(jnp.int32, (TQ, TQ + 2 * HL), 0)
    s = lax.broadcasted_iota(jnp.int32, (TQ, TQ + 2 * HL), 1) - HL
    Dg = w_ref.shape[1]
    t1 = lax.broadcasted_iota(jnp.int32, (TQ, Dg), 0)
    for g, w in enumerate(POOL_WINDOWS):
        lo = jnp.maximum(t - w // 2, lo_min)
        hi = jnp.minimum(t + (w - w // 2), hi_max)
        band = jnp.where(s >= lo, jnp.where(s < hi, 1.0, 0.0), 0.0).astype(BF16)
        cnt = (jnp.minimum(t1 + (w - w // 2), hi_max) - jnp.maximum(t1 - w // 2, lo_min)).astype(F32)
        cols = slice(g * Dg, (g + 1) * Dg)
        win = _dot(band, src[:, cols])
        d = win / cnt - cur_ref[:, cols].astype(F32)
        y = _dot(d.astype(BF16), w_ref[g]) * scale_ref[:, cols]
        o_ref[:, cols] = y.astype(o_ref.dtype)


def _pool_call(px, pool_w_bf, pool_scale, B, L, Lc):
    T = px.shape[0]
    G, Dg, _ = pool_w_bf.shape
    PW = G * Dg
    TQ, HL = POOL_TILE, POOL_HALO
    r = TQ // HL
    n_tiles = T // TQ
    n_halo = T // HL
    kern = functools.partial(_pool_kernel, n_lat_tiles=B * L // TQ, nl=L // TQ, nc=Lc // TQ)
    return pl.pallas_call(
        kern,
        grid=(n_tiles,),
        in_specs=[pl.BlockSpec((HL, PW), lambda i: (jnp.maximum(i * r - 1, 0), 0)),
                  pl.BlockSpec((TQ, PW), lambda i: (i, 0)),
                  pl.BlockSpec((HL, PW), lambda i: (jnp.minimum((i + 1) * r, n_halo - 1), 0)),
                  pl.BlockSpec((G, Dg, Dg), lambda i: (0, 0, 0)),
                  pl.BlockSpec((1, PW), lambda i: (0, 0))],
        out_specs=pl.BlockSpec((TQ, PW), lambda i: (i, 0)),
        out_shape=jax.ShapeDtypeStruct((T, PW), BF16),
        compiler_params=_params(("parallel",)),
        name="multiscale_pool",
    )(px, px, px, pool_w_bf, pool_scale.reshape(1, PW).astype(F32))


READOUT_HEADS = 4


def _readout_kernel(of_ref, ob_ref, gate_ref, ng_ref, o_ref):
    for hh in range(READOUT_HEADS):
        sl = slice(hh * HEAD_DIM, (hh + 1) * HEAD_DIM)
        o = of_ref[:, sl].astype(F32) + ob_ref[:, sl].astype(F32)
        o = o * lax.rsqrt(jnp.mean(o * o, -1, keepdims=True) + LN_EPS)
        g = gate_ref[:, sl].astype(F32)
        o_ref[:, sl] = (o * ng_ref[:, sl] * (g * _sigmoid(g))).astype(o_ref.dtype)


def _readout_call(o_f, o_b, px, norm_g, col_gate, tm):
    T, HW = o_f.shape
    GW = READOUT_HEADS * HEAD_DIM
    assert HW % GW == 0 and col_gate % READOUT_HEADS == 0
    blk = lambda col: pl.BlockSpec((tm, GW), lambda i, h: (i, col // READOUT_HEADS + h))
    return pl.pallas_call(
        _readout_kernel,
        grid=(T // tm, HW // GW),
        in_specs=[blk(0), blk(0), blk(col_gate), pl.BlockSpec((1, GW), lambda i, h: (0, h))],
        out_specs=blk(0),
        out_shape=jax.ShapeDtypeStruct((T, HW), BF16),
        compiler_params=_params(("parallel", "parallel")),
        name="hgrn_readout",
    )(o_f, o_b, px, norm_g.reshape(1, HW).astype(F32))


COND_ROWS = 16


def _adaln_kernel(cond_ref, w_ref, b_ref, o_ref):
    cnd = cond_ref[...]
    a = (cnd * _sigmoid(cnd)).astype(BF16)
    o_ref[0] = _dot(a, w_ref[0].astype(BF16)) + b_ref[0]


def _adaln_call(cond, w_mod, b_mod, tn=1024):
    depth, D, N = w_mod.shape
    return pl.pallas_call(
        _adaln_kernel,
        grid=(depth, N // tn),
        in_specs=[pl.BlockSpec((COND_ROWS, D), lambda l, j: (0, 0)),
                  pl.BlockSpec((1, D, tn), lambda l, j: (l, 0, j)),
                  pl.BlockSpec((1, 1, tn), lambda l, j: (l, 0, j))],
        out_specs=pl.BlockSpec((1, COND_ROWS, tn), lambda l, j: (l, 0, j)),
        out_shape=jax.ShapeDtypeStruct((depth, COND_ROWS, N), F32),
        compiler_params=_params(("parallel", "parallel")),
        name="adaln",
    )(cond, w_mod, b_mod.reshape(depth, 1, N))


def _cond_row(i, tm, n_lat_rows, L, B):
    return jnp.where(i * tm < n_lat_rows, (i * tm) // L, B)


def _layer_norm_rows(x):
    mu = jnp.mean(x, -1, keepdims=True)
    xc = x - mu
    var = jnp.mean(xc * xc, -1, keepdims=True)
    return xc * lax.rsqrt(var + LN_EPS)


def _ln_mod_kernel(x_ref, mod_ref, o_ref, *, tm, n_lat_rows, L, B, D):
    r = _cond_row(pl.program_id(0), tm, n_lat_rows, L, B)
    shift = mod_ref[pl.ds(r, 1), 0:D]
    scale = mod_ref[pl.ds(r, 1), D:2 * D]
    o_ref[...] = (_layer_norm_rows(x_ref[...]) * (1.0 + scale) + shift).astype(o_ref.dtype)


def _ln_mod_call(h, mod_l, B, L, tm=256):
    T, D = h.shape
    kern = functools.partial(_ln_mod_kernel, tm=tm, n_lat_rows=B * L, L=L, B=B, D=D)
    return pl.pallas_call(
        kern,
        grid=(T // tm,),
        in_specs=[pl.BlockSpec((tm, D), lambda i: (i, 0)), pl.BlockSpec(mod_l.shape, lambda i: (0, 0))],
        out_specs=pl.BlockSpec((tm, D), lambda i: (i, 0)),
        out_shape=jax.ShapeDtypeStruct((T, D), BF16),
        compiler_params=_params(("parallel",)),
        name="ln_modulate",
    )(h, mod_l)


def _wstat_matmul_kernel(*refs, widths):
    n = len(widths)
    a_refs, w_ref, o_ref, wbf_ref = refs[:n], refs[n], refs[n + 1], refs[n + 2]

    @pl.when(pl.program_id(1) == 0)
    def _():
        wbf_ref[...] = w_ref[0].astype(BF16)

    acc = None
    k0 = 0
    for a_ref, kw in zip(a_refs, widths):
        part = _dot(a_ref[...], wbf_ref[k0:k0 + kw, :])
        acc = part if acc is None else acc + part
        k0 += kw
    o_ref[...] = acc.astype(o_ref.dtype)


def _wstat_matmul_call(a_list, w_all, layer, n_rows, tm, tn, name):
    widths = tuple(a.shape[1] for a in a_list)
    _, K, N = w_all.shape
    assert sum(widths) == K and n_rows % tm == 0 and N % tn == 0
    return pl.pallas_call(
        functools.partial(_wstat_matmul_kernel, widths=widths),
        grid=(N // tn, n_rows // tm),
        in_specs=[pl.BlockSpec((tm, kw), lambda j, i: (i, 0)) for kw in widths]
        + [pl.BlockSpec((1, K, tn), lambda j, i: (layer, 0, j))],
        out_specs=pl.BlockSpec((tm, tn), lambda j, i: (i, j)),
        out_shape=jax.ShapeDtypeStruct((n_rows, N), BF16),
        scratch_shapes=[pltpu.VMEM((K, tn), BF16)],
        compiler_params=_params(("parallel", "arbitrary")),
        name=name,
    )(*a_list, w_all)


def _split_bf16(x):
    hi = x.astype(BF16)
    return hi, (x - hi.astype(F32)).astype(BF16)


def _post_mix_kernel(y_ref, h_ref, mod_ref, g_ref, b_ref, rw_ref, rb_ref,
                     h1_ref, fin_ref, gates_ref, topi_ref, *, tm, n_lat_rows, L, B, D, alpha):
    r = _cond_row(pl.program_id(0), tm, n_lat_rows, L, B)
    mrow = lambda slot: mod_ref[pl.ds(r, 1), slot * D:(slot + 1) * D]
    u = alpha * h_ref[...] + mrow(2) * y_ref[...].astype(F32)
    h1 = _layer_norm_rows(u) * g_ref[...] + b_ref[...]
    h1_ref[...] = h1
    fin = _layer_norm_rows(h1) * (1.0 + mrow(4)) + mrow(3)
    fin_ref[...] = fin.astype(BF16)
    f_hi, f_lo = _split_bf16(fin)
    w_hi, w_lo = _split_bf16(rw_ref[...])
    logits = _dot(f_hi, w_hi) + _dot(f_hi, w_lo) + _dot(f_lo, w_hi) + rb_ref[...]
    lane = lax.broadcasted_iota(jnp.int32, logits.shape, 1)
    ninf = jnp.float32(-jnp.inf)
    xs = jnp.where(lane < N_EXPERTS, logits, ninf)
    sel_v, sel_i = [], []
    for _ in range(TOP_K):
        m = jnp.max(xs, -1, keepdims=True)
        idx = jnp.min(jnp.where(xs == m, lane, logits.shape[1]), -1, keepdims=True)
        sel_v.append(m)
        sel_i.append(idx)
        xs = jnp.where(lane == idx, ninf, xs)
    ex = [jnp.exp(v - sel_v[0]) for v in sel_v]
    den = ex[0]
    for e in ex[1:]:
        den = den + e
    gates = jnp.zeros(logits.shape, F32)
    topi = jnp.zeros(logits.shape, jnp.int32)
    for j in range(TOP_K):
        gates = jnp.where(lane == sel_i[j], ex[j] / den, gates)
        topi = jnp.where(lane == j, sel_i[j], topi)
    gates_ref[...] = gates
    topi_ref[...] = topi


def _post_mix_call(y, h, mod_l, ln_g, ln_b, rw_pad, rb_pad, B, L, alpha, tm=256):
    n_out, D = y.shape
    NE = rw_pad.shape[1]
    kern = functools.partial(_post_mix_kernel, tm=tm, n_lat_rows=B * L, L=L, B=B, D=D, alpha=alpha)
    row = lambda i: (i, 0)
    fixed = lambda i: (0, 0)
    return pl.pallas_call(
        kern,
        grid=(n_out // tm,),
        in_specs=[pl.BlockSpec((tm, D), row), pl.BlockSpec((tm, D), row),
                  pl.BlockSpec(mod_l.shape, fixed),
                  pl.BlockSpec((1, D), fixed), pl.BlockSpec((1, D), fixed),
                  pl.BlockSpec((D, NE), fixed), pl.BlockSpec((1, NE), fixed)],
        out_specs=[pl.BlockSpec((tm, D), row), pl.BlockSpec((tm, D), row),
                   pl.BlockSpec((tm, NE), row), pl.BlockSpec((tm, NE), row)],
        out_shape=[jax.ShapeDtypeStruct((n_out, D), F32), jax.ShapeDtypeStruct((n_out, D), BF16),
                   jax.ShapeDtypeStruct((n_out, NE), F32), jax.ShapeDtypeStruct((n_out, NE), jnp.int32)],
        compiler_params=_params(("parallel",)),
        name="post_mix_router",
    )(y, h, mod_l, ln_g.reshape(1, D), ln_b.reshape(1, D), rw_pad, rb_pad)


MOE_TILE = 512
MOE_ALIGN = 16


def _first_visit_of_expert(v, ve_ref):
    return (v == 0) | (ve_ref[jnp.maximum(v - 1, 0)] != ve_ref[v])


def _moe_up_kernel(start_ref, ve_ref, vout_ref, vv_ref, x_ref, w1_ref, b1_ref, o_ref, wbf_ref, *, DE):
    v = pl.program_id(0)

    @pl.when(_first_visit_of_expert(v, ve_ref))
    def _():
        wbf_ref[...] = w1_ref[0, 0].astype(BF16)

    @pl.when(vv_ref[v] > 0)
    def _():
        hcat = _dot(x_ref[...], wbf_ref[...]) + b1_ref[0, 0]
        gate = jnp.minimum(hcat[:, :DE], SWIGLU_LIMIT)
        up = jnp.clip(hcat[:, DE:], -SWIGLU_LIMIT, SWIGLU_LIMIT)
        o_ref[...] = ((up + 1.0) * (gate * _sigmoid(SWIGLU_ALPHA * gate))).astype(o_ref.dtype)


def _moe_down_kernel(start_ref, ve_ref, vout_ref, vv_ref, h_ref, rw_ref, w2_ref, b2_ref, o_ref, wbf_ref):
    v = pl.program_id(0)

    @pl.when(_first_visit_of_expert(v, ve_ref))
    def _():
        wbf_ref[...] = w2_ref[0, 0].astype(BF16)

    @pl.when(vv_ref[v] > 0)
    def _():
        y = (_dot(h_ref[...], wbf_ref[...]) + b2_ref[0, 0]) * rw_ref[...]
        o_ref[...] = y.astype(o_ref.dtype)


def _moe_call(xs, row_w, visits, w1_all, b1_all, w2_all, b2_all, layer):
    n_in, D = xs.shape
    _, E, _, DE2 = w1_all.shape
    DE = DE2 // 2
    tm = MOE_TILE
    nv = visits[0].shape[0]
    window = lambda v, st, ve, vo, vv: (pl.multiple_of(st[v], MOE_ALIGN), 0)
    out_tile = lambda v, st, ve, vo, vv: (vo[v], 0)
    expert = lambda v, st, ve, vo, vv: (layer, ve[v], 0, 0)
    hdn = pl.pallas_call(
        functools.partial(_moe_up_kernel, DE=DE),
        grid_spec=pltpu.PrefetchScalarGridSpec(
            num_scalar_prefetch=4, grid=(nv,),
            in_specs=[pl.BlockSpec((pl.Element(tm), pl.Element(D)), window),
                      pl.BlockSpec((1, 1, D, DE2), expert),
                      pl.BlockSpec((1, 1, 1, DE2), expert)],
            out_specs=pl.BlockSpec((tm, DE), out_tile),
            scratch_shapes=[pltpu.VMEM((D, DE2), BF16)]),
        out_shape=jax.ShapeDtypeStruct((nv * tm, DE), BF16),
        compiler_params=_params(("arbitrary",)),
        name="moe_up",
    )(*visits, xs, w1_all, b1_all.reshape(b1_all.shape[0], E, 1, DE2))
    return pl.pallas_call(
        _moe_down_kernel,
        grid_spec=pltpu.PrefetchScalarGridSpec(
            num_scalar_prefetch=4, grid=(nv,),
            in_specs=[pl.BlockSpec((tm, DE), out_tile),
                      pl.BlockSpec((pl.Element(tm), pl.Element(1)), window),
                      pl.BlockSpec((1, 1, DE, D), expert),
                      pl.BlockSpec((1, 1, 1, D), expert)],
            out_specs=pl.BlockSpec((tm, D), out_tile),
            scratch_shapes=[pltpu.VMEM((DE, D), BF16)]),
        out_shape=jax.ShapeDtypeStruct((nv * tm, D), BF16),
        compiler_params=_params(("arbitrary",)),
        name="moe_down",
    )(*visits, hdn, row_w, w2_all, b2_all.reshape(b2_all.shape[0], E, 1, D))


def _moe_dispatch(topi, gates, tm):
    T = topi.shape[0]
    E = N_EXPERTS
    n = T * TOP_K
    al = MOE_ALIGN
    flat_e = topi.reshape(-1)
    flat_w = jnp.take_along_axis(gates, topi, axis=1).reshape(-1)
    order = jnp.argsort(flat_e, stable=True).astype(jnp.int32)
    inv = jnp.argsort(order).astype(jnp.int32)
    e_col = jnp.arange(E, dtype=jnp.int32)[:, None]
    is_e = flat_e[None, :] == e_col
    counts = jnp.sum(is_e.astype(jnp.int32), axis=1)
    ends = jnp.cumsum(counts)
    offs = ends - counts
    seg_len = ((counts + al - 1) // al) * al
    seg_end = jnp.cumsum(seg_len)
    seg_start = seg_end - seg_len
    n_in = n + E * al + tm
    p = jnp.arange(n_in, dtype=jnp.int32)
    past = p[None, :] >= seg_end[:, None]
    rank = p - jnp.sum(jnp.where(past, (seg_len - counts)[:, None], 0), axis=0)
    end_of_p = ends[0] + jnp.sum(jnp.where(past[:-1], (ends[1:] - ends[:-1])[:, None], 0), axis=0)
    real = jnp.logical_and(rank < end_of_p, p < seg_end[-1])
    src = order[jnp.where(real, rank, 0)]
    row_tok = jnp.where(real, src // TOP_K, 0)
    row_w = jnp.where(real, flat_w[src], 0.0).reshape(n_in, 1)
    n_tiles_e = (counts + tm - 1) // tm
    tile_end = jnp.cumsum(n_tiles_e)
    tile_start = tile_end - n_tiles_e
    total = tile_end[-1]
    nv = n // tm + E
    v = jnp.arange(nv, dtype=jnp.int32)
    ve = jnp.minimum(jnp.sum((v[:, None] >= tile_end[None, :]).astype(jnp.int32), axis=1), E - 1)
    valid = v < total
    last = total - 1
    ve = jnp.where(valid, ve, ve[last]).astype(jnp.int32)
    vout = jnp.where(valid, v, last).astype(jnp.int32)
    start = (seg_start[ve] + (vout - tile_start[ve]) * tm).astype(jnp.int32)
    pos = inv + jnp.sum(jnp.where(is_e, (tile_start * tm - offs)[:, None], 0), axis=0)
    pos_kmajor = pos.reshape(T, TOP_K).T.reshape(-1)
    return row_tok, row_w, pos_kmajor, (start, ve, vout, valid.astype(jnp.int32))


def _final_kernel(h_ref, *refs, tm, n_lat_rows, L, B, D, alpha, with_next):
    y_refs, rest = refs[:TOP_K], refs[TOP_K:]
    r = _cond_row(pl.program_id(0), tm, n_lat_rows, L, B)
    gate = rest[0][pl.ds(r, 1), 5 * D:6 * D]
    f = y_refs[0][...].astype(F32)
    for y_ref in y_refs[1:]:
        f = f + y_ref[...].astype(F32)
    u = alpha * h_ref[...] + gate * f
    h2 = _layer_norm_rows(u) * rest[1][...] + rest[2][...]
    if with_next:
        modn_ref, o_ref, a_ref = rest[3:]
        a_ref[...] = (_layer_norm_rows(h2) * (1.0 + modn_ref[pl.ds(r, 1), D:2 * D])
                      + modn_ref[pl.ds(r, 1), 0:D]).astype(a_ref.dtype)
    else:
        o_ref = rest[3]
    o_ref[...] = h2


def _final_call(h1, y_assign, mod_l, ln_g, ln_b, B, L, alpha, mod_next=None, tm=256):
    n, D = h1.shape
    nt = n // tm
    with_next = mod_next is not None
    kern = functools.partial(_final_kernel, tm=tm, n_lat_rows=B * L, L=L, B=B, D=D, alpha=alpha,
                             with_next=with_next)
    row = lambda i: (i, 0)
    fixed = lambda i: (0, 0)
    y_specs = [pl.BlockSpec((tm, D), functools.partial(lambda i, k: (k * nt + i, 0), k=k)) for k in range(TOP_K)]
    mod_spec = pl.BlockSpec(mod_l.shape, fixed)
    row_spec = pl.BlockSpec((tm, D), row)
    return pl.pallas_call(
        kern,
        grid=(nt,),
        in_specs=[row_spec] + y_specs + [mod_spec, pl.BlockSpec((1, D), fixed), pl.BlockSpec((1, D), fixed)]
        + ([mod_spec] if with_next else []),
        out_specs=[row_spec, row_spec] if with_next else row_spec,
        out_shape=([jax.ShapeDtypeStruct((n, D), F32), jax.ShapeDtypeStruct((n, D), BF16)] if with_next
                   else jax.ShapeDtypeStruct((n, D), F32)),
        compiler_params=_params(("parallel",)),
        name="final_ln",
    )(h1, *([y_assign] * TOP_K), mod_l, ln_g.reshape(1, D), ln_b.reshape(1, D),
      *([mod_next] if with_next else []))


def kernel(x, c, ctx, c_ctx, w_mod, b_mod, w_in, pool_w, pool_scale, na_rpb, hg_lb, hg_norm_g, w_out, ln1_g, ln1_b, ln2_g, ln2_b, router_w, router_b, exp_w1, exp_b1, exp_w2, exp_b2):
    B, L, D = x.shape
    Lc = ctx.shape[1]
    depth = w_in.shape[0]
    alpha = (2 * depth) ** 0.25
    pool_width = pool_w.shape[1] * pool_w.shape[2]
    na_width = na_rpb.shape[1] * HEAD_DIM
    hg_width = hg_norm_g.shape[1]
    na_heads, hg_heads = na_width // HEAD_DIM, hg_width // HEAD_DIM
    c_naq = pool_width // HEAD_DIM
    c_nak, c_nav = c_naq + na_heads, c_naq + 2 * na_heads
    c_hq = c_naq + 3 * na_heads
    c_hff, c_hfb, c_hi, c_hg = (c_hq + k * hg_heads for k in range(1, 5))
    assert B + 1 <= COND_ROWS

    hall = jnp.concatenate([x.reshape(B * L, D), ctx.reshape(B * Lc, D)], axis=0)
    cond = jnp.concatenate([c, c_ctx[None], jnp.zeros((COND_ROWS - B - 1, D), F32)], axis=0)
    mod = _adaln_call(cond, w_mod, b_mod)
    lb_soft = jax.nn.softmax(hg_lb.astype(F32), axis=1)
    lower = jnp.cumsum(lb_soft, axis=1) - lb_soft[:, :1]
    rw_pad = jnp.pad(router_w, ((0, 0), (0, 0), (0, HEAD_DIM - N_EXPERTS)))
    rb_pad = jnp.pad(router_b, ((0, 0), (0, HEAD_DIM - N_EXPERTS)))

    T = hall.shape[0]
    n_row_tiles = 8
    a_in = _ln_mod_call(hall, mod[0], B, L)
    for l in range(depth):
        last = l == depth - 1
        px = _wstat_matmul_call([a_in], w_in, l, T, T // n_row_tiles, 512, "in_proj")
        a_mix = _pool_call(px, pool_w[l].astype(BF16), pool_scale[l], B, L, Lc)
        b_x, b_c = _na_call(px, na_rpb[l], B, L, Lc, c_naq, c_nak, c_nav, na_heads)
        b_mix = jnp.concatenate([b_x, b_c], axis=0)
        o_f, o_b = _hgrn_call(px, lower[0, l].reshape(hg_heads, 1, HEAD_DIM),
                              lower[1, l].reshape(hg_heads, 1, HEAD_DIM), B, L, Lc,
                              c_hq, c_hff, c_hfb, c_hi, hg_heads)
        c_mix = _readout_call(o_f, o_b, px, hg_norm_g[l], c_hg, T // n_row_tiles)
        n_out = B * L if last else T
        y = _wstat_matmul_call([a_mix, b_mix, c_mix], w_out, l, n_out, n_out // n_row_tiles, 512, "out_proj")
        h1, fin, gates, topi = _post_mix_call(y, hall, mod[l], ln1_g[l], ln1_b[l], rw_pad[l],
                                              rb_pad[l].reshape(1, -1), B, L, alpha)
        sorted_tok, sorted_w, pos_kmajor, visits = _moe_dispatch(topi[:, :TOP_K], gates, MOE_TILE)
        xs = fin.at[sorted_tok].get(mode="promise_in_bounds")
        ys = _moe_call(xs, sorted_w, visits, exp_w1, exp_b1, exp_w2, exp_b2, l)
        y_assign = ys.at[pos_kmajor].get(mode="promise_in_bounds")
        if last:
            hall = _final_call(h1, y_assign, mod[l], ln2_g[l], ln2_b[l], B, L, alpha)
        else:
            hall, a_in = _final_call(h1, y_assign, mod[l], ln2_g[l], ln2_b[l], B, L, alpha, mod_next=mod[l + 1])
    return hall.reshape(B, L, D)
```

```python
import functools

import jax
import jax.numpy as jnp
import numpy as np
from jax import lax
from jax.experimental import pallas as pl
from jax.experimental.pallas import tpu as pltpu

F32 = jnp.float32
BF16 = jnp.bfloat16

HEAD_DIM = 128
GRID_W = 64
POOL_WINDOWS = (2, 4, 8, 16)
NA_ROWS = 8
NA_COLS = 16
FORGET_EPS = 1e-20
N_EXPERTS = 32
TOP_K = 4
SWIGLU_LIMIT = 7.0
SWIGLU_ALPHA = 1.702
ROPE_BASE = 10000.0
LN_EPS = 1e-6

VMEM_LIMIT_BYTES = 56 * 1024 * 1024
HG_CHUNK = 256
NEG_BIG = -1e30


def _params(sem):
    return pltpu.CompilerParams(dimension_semantics=sem, vmem_limit_bytes=VMEM_LIMIT_BYTES)


def _dot(a, b):
    return jnp.dot(a, b, preferred_element_type=F32)


def _dot_nt(a, b):
    return lax.dot_general(a, b, (((1,), (1,)), ((), ())), preferred_element_type=F32)


def _sigmoid(x):
    return 1.0 / (1.0 + jnp.exp(-x))


LOG2E = 1.4426950408889634
HG_HALF = HG_CHUNK // 2
HG_HEADS_PER_STEP = 2


def _neg_abs(x):
    bits = lax.bitcast_convert_type(x, jnp.uint32) | jnp.uint32(0x80000000)
    return lax.bitcast_convert_type(bits, F32)


def _hgrn_half_scores(q, k, b2, g2, fc, lev, reverse):
    n = q.shape[0]
    row = lax.broadcasted_iota(jnp.int32, (n, HEAD_DIM), 0)
    a = jnp.where(lev == -1, _dot_nt(q.astype(BF16), k.astype(BF16)), 0.0)
    h, lvl = 1, 0
    while h < n:
        if h < 8:
            upper = (row & h) != 0
            t_role = jnp.logical_not(upper) if reverse else upper
            if h == 1:
                x = jnp.where(t_role, q * fc, k)
            elif h == 2:
                r4 = row & 3
                g_prev, g_next = pltpu.roll(g2, 1, 0), pltpu.roll(g2, n - 1, 0)
                if reverse:
                    e = jnp.where(r4 == 0, g2 + g_next, jnp.where(r4 == 1, g2, jnp.where(r4 == 2, 0.0, g_prev)))
                else:
                    e = jnp.where(r4 == 0, g_next, jnp.where(r4 == 1, 0.0, jnp.where(r4 == 2, g2, g2 + g_prev)))
                x = jnp.where(t_role, q, k) * jnp.exp2(e)
            else:
                m0 = h if reverse else h - 1
                b3 = b2.reshape(n // (2 * h), 2 * h, HEAD_DIM)
                ref = jnp.broadcast_to(b3[:, m0:m0 + 1, :], b3.shape).reshape(n, HEAD_DIM)
                x = jnp.where(t_role, q, k) * jnp.exp2(_neg_abs(b2 - ref))
            x = x.astype(BF16)
            a = jnp.where(lev == lvl, _dot_nt(x, x), a)
        else:
            nb = n // (2 * h)
            q4 = q.reshape(nb, 2, h, HEAD_DIM)
            k4 = k.reshape(nb, 2, h, HEAD_DIM)
            b4 = b2.reshape(nb, 2, h, HEAD_DIM)
            s_half, t_half, m_row = (1, 0, 0) if reverse else (0, 1, h - 1)
            ref = b4[:, s_half:s_half + 1, m_row:m_row + 1, :]
            xt = q4[:, t_half] * jnp.exp2(b4[:, t_half] - ref[:, 0])
            xs = k4[:, s_half] * jnp.exp2(ref[:, 0] - b4[:, s_half])
            halves = [xt, xs] if reverse else [xs, xt]
            x = jnp.stack(halves, axis=1).reshape(n, HEAD_DIM).astype(BF16)
            g = _dot_nt(xt.reshape(n // 2, HEAD_DIM).astype(BF16), x)
            a4 = a.reshape(nb, 2, h, n)
            lev_t = lev.reshape(nb, 2, h, n)[:, t_half]
            a_t = jnp.where(lev_t == lvl, g.reshape(nb, h, n), a4[:, t_half])
            halves = [a_t, a4[:, s_half]] if reverse else [a4[:, s_half], a_t]
            a = jnp.stack(halves, axis=1).reshape(n, n)
        h *= 2
        lvl += 1
    return a


def _hgrn_direction(q, z, v_bf, lb, st_ref, lev, tri_bf, reverse):
    C, H = HG_CHUNK, HG_HALF
    q = q * _sigmoid(q)
    sig = _sigmoid(z)
    fc = jnp.maximum(lb + (1.0 - lb) * sig, FORGET_EPS)
    g2 = jnp.log(fc) * LOG2E
    k = (1.0 - lb) * (1.0 - sig)

    g_hi = g2.astype(BF16)
    r1 = g2 - g_hi.astype(F32)
    g_mid = r1.astype(BF16)
    g_lo = (r1 - g_mid.astype(F32)).astype(BF16)
    b2 = _dot(tri_bf, g_hi) + _dot(tri_bf, g_mid) + _dot(tri_bf, g_lo)

    lo, hi = slice(0, H), slice(H, C)
    a_lo = _hgrn_half_scores(q[lo], k[lo], b2[lo], g2[lo], fc[lo], lev, reverse).astype(BF16)
    a_hi = _hgrn_half_scores(q[hi], k[hi], b2[hi], g2[hi], fc[hi], lev, reverse).astype(BF16)
    t_sl, s_sl, m = (lo, hi, H) if reverse else (hi, lo, H - 1)
    ref = b2[m:m + 1, :]
    qt = (q[t_sl] * jnp.exp2(b2[t_sl] - ref)).astype(BF16)
    ks = (k[s_sl] * jnp.exp2(ref - b2[s_sl])).astype(BF16)
    cross = _dot(_dot_nt(qt, ks).astype(BF16), v_bf[s_sl])
    o_lo = _dot(a_lo, v_bf[lo])
    o_hi = _dot(a_hi, v_bf[hi])
    o_intra = jnp.concatenate([o_lo + cross, o_hi] if reverse else [o_lo, o_hi + cross], axis=0)

    st = st_ref[...]
    tot = b2[0:1, :] if reverse else b2[C - 1:C, :]
    qdec = (q * jnp.exp2(b2)).astype(BF16)
    o_inter = _dot_nt(qdec, st.astype(BF16))
    kdec = (k * jnp.exp2(tot - b2)).astype(BF16)
    vt = v_bf.astype(F32).T.astype(BF16)
    st_ref[...] = jnp.exp2(tot) * st + _dot(vt, kdec)
    return o_intra + o_inter


def _hgrn_kernel(qf_ref, zf_ref, vf_ref, qb_ref, zb_ref, vb_ref, lbf_ref, lbb_ref, levf_ref, levb_ref,
                 trif_ref, trib_ref, of_ref, ob_ref, sf_ref, sb_ref):
    @pl.when(pl.program_id(2) == 0)
    def _():
        sf_ref[...] = jnp.zeros_like(sf_ref)
        sb_ref[...] = jnp.zeros_like(sb_ref)

    for hh in range(HG_HEADS_PER_STEP):
        sl = slice(hh * HEAD_DIM, (hh + 1) * HEAD_DIM)
        of_ref[:, sl] = _hgrn_direction(qf_ref[:, sl].astype(F32), zf_ref[:, sl].astype(F32), vf_ref[:, sl],
                                        lbf_ref[0][:, sl], sf_ref.at[hh], levf_ref[...], trif_ref[...],
                                        False).astype(of_ref.dtype)
        ob_ref[:, sl] = _hgrn_direction(qb_ref[:, sl].astype(F32), zb_ref[:, sl].astype(F32), vb_ref[:, sl],
                                        lbb_ref[0][:, sl], sb_ref.at[hh], levb_ref[...], trib_ref[...],
                                        True).astype(ob_ref.dtype)


def _hgrn_call(px, lb_f, lb_b, B, L, Lc, col_q, col_ff, col_fb, col_i, n_heads):
    C, H = HG_CHUNK, HG_HALF
    T = px.shape[0]
    nlc, ncc = L // C, Lc // C
    ns = nlc + ncc
    ctx0 = B * nlc

    def fwd_row(b, s):
        return jnp.where(s < ncc, ctx0 + b * ncc + s, b * nlc + s - ncc)

    def bwd_row(b, s):
        return jnp.where(s < ncc, ctx0 + b * ncc + (ncc - 1 - s), b * nlc + (nlc - 1 - (s - ncc)))

    G = HG_HEADS_PER_STEP
    GW = G * HEAD_DIM
    assert n_heads % G == 0 and all(c % G == 0 for c in (col_q, col_ff, col_fb, col_i))

    def spec(row_fn, col):
        return pl.BlockSpec((C, GW), lambda b, h, s: (row_fn(b, s), col // G + h))

    idx = np.arange(H)
    xr = idx[:, None] ^ idx[None, :]
    level = np.floor(np.log2(np.maximum(xr, 1))).astype(np.int32)
    lev_f = np.where(xr == 0, -1, np.where(idx[:, None] > idx[None, :], level, -2)).astype(np.int32)
    lev_b = np.ascontiguousarray(lev_f.T)
    idc = np.arange(C)
    tri_f = (idc[:, None] >= idc[None, :]).astype(np.float32)
    tri_b = (idc[:, None] <= idc[None, :]).astype(np.float32)
    const = lambda shape: pl.BlockSpec(shape, lambda b, h, s: (0,) * len(shape))
    lb_spec = pl.BlockSpec((1, 1, GW), lambda b, h, s: (h, 0, 0))
    out_sds = jax.ShapeDtypeStruct((T, n_heads * HEAD_DIM), BF16)
    state = pltpu.VMEM((G, HEAD_DIM, HEAD_DIM), F32)
    return pl.pallas_call(
        _hgrn_kernel,
        grid=(B, n_heads // G, ns),
        in_specs=[spec(fwd_row, col_q), spec(fwd_row, col_ff), spec(fwd_row, col_i),
                  spec(bwd_row, col_q), spec(bwd_row, col_fb), spec(bwd_row, col_i),
                  lb_spec, lb_spec, const((H, H)), const((H, H)), const((C, C)), const((C, C))],
        out_specs=[pl.BlockSpec((C, GW), lambda b, h, s: (fwd_row(b, s), h)),
                   pl.BlockSpec((C, GW), lambda b, h, s: (bwd_row(b, s), h))],
        out_shape=[out_sds, out_sds],
        scratch_shapes=[state, state],
        compiler_params=_params(("parallel", "parallel", "arbitrary")),
        name="hgrn_scan",
    )(px, px, px, px, px, px, lb_f.reshape(n_heads // G, 1, GW), lb_b.reshape(n_heads // G, 1, GW),
      jnp.asarray(lev_f), jnp.asarray(lev_b), jnp.asarray(tri_f, BF16), jnp.asarray(tri_b, BF16))


NA_QROWS = 4
NA_KROWS = NA_QROWS + NA_ROWS


def _na_block_cases(rows):
    assert rows % NA_QROWS == 0 and rows >= 2 * NA_ROWS
    nb = rows // NA_QROWS
    cases = []
    for i in (0, 1, nb - 1):
        r0 = NA_QROWS * i
        ks = min(max(r0 - NA_ROWS // 2, 0), rows - NA_KROWS)
        per_q = []
        for qr in range(NA_QROWS):
            r = r0 + qr
            rs = min(max(r - NA_ROWS // 2, 0), rows - NA_ROWS)
            per_q.append((rs - ks, ks - r + NA_ROWS - 1))
        cases.append(per_q)
    return cases


def _na_kernel(rpb_ref, q_ref, k_ref, v_ref, qc_ref, kc_ref, vc_ref, cos_ref, sin_ref,
               ox_ref, oc_ref, qr_ref, kr_ref, tb_ref, bias_ref, *, rows):
    h = pl.program_id(0)
    L = q_ref.shape[0]
    W = GRID_W
    scale = HEAD_DIM ** -0.5
    n_dr = 2 * NA_ROWS - 1
    n_dc = 2 * NA_COLS - 1

    @pl.when(pl.program_id(1) == 0)
    def _():
        c_i = lax.broadcasted_iota(jnp.int32, (W, W), 0)
        kc_i = lax.broadcasted_iota(jnp.int32, (W, W), 1)
        cs = jnp.clip(c_i - NA_COLS // 2, 0, W - NA_COLS)
        in_win = (kc_i >= cs) & (kc_i < cs + NA_COLS)
        c_off = kc_i - c_i + (NA_COLS - 1)
        for dr in range(n_dr):
            acc = jnp.zeros((W, W), F32)
            for j in range(n_dc):
                acc = jnp.where(c_off == j, rpb_ref[h, dr, j], acc)
            tb_ref[dr] = jnp.where(in_win, acc, NEG_BIG)
        neg = jnp.full((W, W), NEG_BIG, F32)
        for ci, per_q in enumerate(_na_block_cases(rows)):
            for qr, (j0, dr0) in enumerate(per_q):
                for j in range(NA_KROWS):
                    valid = j0 <= j < j0 + NA_ROWS
                    blk = tb_ref[dr0 + j] if valid else neg
                    bias_ref[ci, qr * W:(qr + 1) * W, j * W:(j + 1) * W] = blk

    RC = 256
    lane = lax.broadcasted_iota(jnp.int32, (RC, HEAD_DIM), 1)
    first = (lane & (HEAD_DIM // 2 - 1)) < HEAD_DIM // 4

    def rope_chunk(i, carry):
        sl = pl.ds(pl.multiple_of(i * RC, RC), RC)
        cos = cos_ref[sl, :]
        sin = sin_ref[sl, :]
        for src, dst, mul in ((q_ref, qr_ref, scale), (k_ref, kr_ref, 1.0)):
            t = src[sl, :].astype(F32)
            partner = jnp.where(first, pltpu.roll(t, HEAD_DIM - HEAD_DIM // 4, 1), pltpu.roll(t, HEAD_DIM // 4, 1))
            dst[sl, :] = ((t * cos + partner * sin) * mul).astype(BF16)
        return carry

    lax.fori_loop(0, L // RC, rope_chunk, 0)

    kc = kc_ref[...]
    vc = vc_ref[...]
    nb = rows // NA_QROWS
    QB = NA_QROWS * W
    KB = NA_KROWS * W

    def block(i, carry):
        ks = jnp.clip(i * NA_QROWS - NA_ROWS // 2, 0, rows - NA_KROWS)
        case = jnp.where(i == 0, 0, jnp.where(i == nb - 1, 2, 1))
        qs = pl.ds(pl.multiple_of(i * QB, QB), QB)
        kslice = pl.ds(pl.multiple_of(ks * W, W), KB)
        qb = qr_ref[qs, :]
        s_loc = _dot_nt(qb, kr_ref[kslice, :]) + bias_ref[case]
        s_ctx = _dot_nt(qb, kc)
        m = jnp.maximum(jnp.max(s_loc, -1, keepdims=True), jnp.max(s_ctx, -1, keepdims=True))
        p_loc = jnp.exp(s_loc - m)
        p_ctx = jnp.exp(s_ctx - m)
        den = jnp.sum(p_loc, -1, keepdims=True) + jnp.sum(p_ctx, -1, keepdims=True)
        o = _dot(p_loc.astype(BF16), v_ref[kslice, :]) + _dot(p_ctx.astype(BF16), vc)
        ox_ref[qs, :] = (o / den).astype(ox_ref.dtype)
        return carry

    lax.fori_loop(0, nb, block, 0, unroll=2)

    qc = (qc_ref[...].astype(F32) * scale).astype(BF16)
    s = _dot_nt(qc, kc)
    p = jnp.exp(s - jnp.max(s, -1, keepdims=True))
    oc = _dot(p.astype(BF16), vc) / jnp.sum(p, -1, keepdims=True)
    oc_ref[...] = oc.astype(oc_ref.dtype)


def _rope_tables(L):
    pos = jnp.arange(L)
    half = HEAD_DIM // 2
    inv = ROPE_BASE ** (-jnp.arange(0, half, 2, dtype=F32) / half)

    def tab(p):
        ang = p.astype(F32)[:, None] * inv[None]
        cos, sin = jnp.cos(ang), jnp.sin(ang)
        return jnp.concatenate([cos, cos], -1), jnp.concatenate([-sin, sin], -1)

    c_r, s_r = tab(pos // GRID_W)
    c_c, s_c = tab(pos % GRID_W)
    return jnp.concatenate([c_r, c_c], -1), jnp.concatenate([s_r, s_c], -1)


def _na_call(px, rpb, B, L, Lc, col_q, col_k, col_v, n_heads):
    rows = L // GRID_W
    cos, sin = _rope_tables(L)
    ctx0 = B * L // Lc

    def lat(col):
        return pl.BlockSpec((L, HEAD_DIM), lambda h, b: (b, col + h))

    def cx(col):
        return pl.BlockSpec((Lc, HEAD_DIM), lambda h, b: (ctx0 + b, col + h))

    tab = pl.BlockSpec((L, HEAD_DIM), lambda h, b: (0, 0))
    W = GRID_W
    return pl.pallas_call(
        functools.partial(_na_kernel, rows=rows),
        grid=(n_heads, B),
        in_specs=[pl.BlockSpec(memory_space=pltpu.SMEM), lat(col_q), lat(col_k), lat(col_v),
                  cx(col_q), cx(col_k), cx(col_v), tab, tab],
        out_specs=[pl.BlockSpec((L, HEAD_DIM), lambda h, b: (b, h)),
                   pl.BlockSpec((Lc, HEAD_DIM), lambda h, b: (b, h))],
        out_shape=[jax.ShapeDtypeStruct((B * L, n_heads * HEAD_DIM), BF16),
                   jax.ShapeDtypeStruct((B * Lc, n_heads * HEAD_DIM), BF16)],
        scratch_shapes=[pltpu.VMEM((L, HEAD_DIM), BF16), pltpu.VMEM((L, HEAD_DIM), BF16),
                        pltpu.VMEM((2 * NA_ROWS - 1, W, W), F32),
                        pltpu.VMEM((3, NA_QROWS * W, NA_KROWS * W), F32)],
        compiler_params=_params(("parallel", "arbitrary")),
        name="neighborhood_attention",
    )(rpb, px, px, px, px, px, px, cos, sin)


POOL_TILE = 256
POOL_HALO = 16


def _pool_kernel(prev_ref, cur_ref, next_ref, w_ref, scale_ref, o_ref, *, n_lat_tiles, nl, nc):
    i = pl.program_id(0)
    is_lat = i < n_lat_tiles
    j = jnp.where(is_lat, i % nl, (i - n_lat_tiles) % nc)
    n = jnp.where(is_lat, nl, nc)
    TQ, HL = POOL_TILE, POOL_HALO
    lo_min = jnp.where(j > 0, -HL, 0)
    hi_max = jnp.where(j < n - 1, TQ + HL, TQ)
    src = jnp.concatenate([prev_ref[...], cur_ref[...], next_ref[...]], axis=0)
    t = lax.broadcasted_iota(jnp.int32, (TQ, TQ + 2 * HL), 0)
    s = lax.broadcasted_iota(jnp.int32, (TQ, TQ + 2 * HL), 1) - HL
    Dg = w_ref.shape[1]
    t1 = lax.broadcasted_iota(jnp.int32, (TQ, Dg), 0)
    for g, w in enumerate(POOL_WINDOWS):
        lo = jnp.maximum(t - w // 2, lo_min)
        hi = jnp.minimum(t + (w - w // 2), hi_max)
        band = jnp.where(s >= lo, jnp.where(s < hi, 1.0, 0.0), 0.0).astype(BF16)
        cnt = (jnp.minimum(t1 + (w - w // 2), hi_max) - jnp.maximum(t1 - w // 2, lo_min)).astype(F32)
        cols = slice(g * Dg, (g + 1) * Dg)
        win = _dot(band, src[:, cols])
        d = win / cnt - cur_ref[:, cols].astype(F32)
        y = _dot(d.astype(BF16), w_ref[g]) * scale_ref[:, cols]
        o_ref[:, cols] = y.astype(o_ref.dtype)


def _pool_call(px, pool_w_bf, pool_scale, B, L, Lc):
    T = px.shape[0]
    G, Dg, _ = pool_w_bf.shape
    PW = G * Dg
    TQ, HL = POOL_TILE, POOL_HALO
    r = TQ // HL
    n_tiles = T // TQ
    n_halo = T // HL
    kern = functools.partial(_pool_kernel, n_lat_tiles=B * L // TQ, nl=L // TQ, nc=Lc // TQ)
    return pl.pallas_call(
        kern,
        grid=(n_tiles,),
        in_specs=[pl.BlockSpec((HL, PW), lambda i: (jnp.maximum(i * r - 1, 0), 0)),
                  pl.BlockSpec((TQ, PW), lambda i: (i, 0)),
                  pl.BlockSpec((HL, PW), lambda i: (jnp.minimum((i + 1) * r, n_halo - 1), 0)),
                  pl.BlockSpec((G, Dg, Dg), lambda i: (0, 0, 0)),
                  pl.BlockSpec((1, PW), lambda i: (0, 0))],
        out_specs=pl.BlockSpec((TQ, PW), lambda i: (i, 0)),
        out_shape=jax.ShapeDtypeStruct((T, PW), BF16),
        compiler_params=_params(("parallel",)),
        name="multiscale_pool",
    )(px, px, px, pool_w_bf, pool_scale.reshape(1, PW).astype(F32))


READOUT_HEADS = 4


def _readout_kernel(of_ref, ob_ref, gate_ref, ng_ref, o_ref):
    for hh in range(READOUT_HEADS):
        sl = slice(hh * HEAD_DIM, (hh + 1) * HEAD_DIM)
        o = of_ref[:, sl].astype(F32) + ob_ref[:, sl].astype(F32)
        o = o * lax.rsqrt(jnp.mean(o * o, -1, keepdims=True) + LN_EPS)
        g = gate_ref[:, sl].astype(F32)
        o_ref[:, sl] = (o * ng_ref[:, sl] * (g * _sigmoid(g))).astype(o_ref.dtype)


def _readout_call(o_f, o_b, px, norm_g, col_gate, tm):
    T, HW = o_f.shape
    GW = READOUT_HEADS * HEAD_DIM
    assert HW % GW == 0 and col_gate % READOUT_HEADS == 0
    blk = lambda col: pl.BlockSpec((tm, GW), lambda i, h: (i, col // READOUT_HEADS + h))
    return pl.pallas_call(
        _readout_kernel,
        grid=(T // tm, HW // GW),
        in_specs=[blk(0), blk(0), blk(col_gate), pl.BlockSpec((1, GW), lambda i, h: (0, h))],
        out_specs=blk(0),
        out_shape=jax.ShapeDtypeStruct((T, HW), BF16),
        compiler_params=_params(("parallel", "parallel")),
        name="hgrn_readout",
    )(o_f, o_b, px, norm_g.reshape(1, HW).astype(F32))


COND_ROWS = 16


def _adaln_kernel(cond_ref, w_ref, b_ref, o_ref):
    cnd = cond_ref[...]
    a = (cnd * _sigmoid(cnd)).astype(BF16)
    o_ref[0] = _dot(a, w_ref[0].astype(BF16)) + b_ref[0]


def _adaln_call(cond, w_mod, b_mod, tn=1024):
    depth, D, N = w_mod.shape
    return pl.pallas_call(
        _adaln_kernel,
        grid=(depth, N // tn),
        in_specs=[pl.BlockSpec((COND_ROWS, D), lambda l, j: (0, 0)),
                  pl.BlockSpec((1, D, tn), lambda l, j: (l, 0, j)),
                  pl.BlockSpec((1, 1, tn), lambda l, j: (l, 0, j))],
        out_specs=pl.BlockSpec((1, COND_ROWS, tn), lambda l, j: (l, 0, j)),
        out_shape=jax.ShapeDtypeStruct((depth, COND_ROWS, N), F32),
        compiler_params=_params(("parallel", "parallel")),
        name="adaln",
    )(cond, w_mod, b_mod.reshape(depth, 1, N))


def _cond_row(i, tm, n_lat_rows, L, B):
    return jnp.where(i * tm < n_lat_rows, (i * tm) // L, B)


def _layer_norm_rows(x):
    mu = jnp.mean(x, -1, keepdims=True)
    xc = x - mu
    var = jnp.mean(xc * xc, -1, keepdims=True)
    return xc * lax.rsqrt(var + LN_EPS)


def _row_part_specs(parts, tm):
    specs, firsts, lo = [], [], 0
    for part in parts:
        n_t = part.shape[0] // tm
        assert part.shape[0] % tm == 0
        specs.append(pl.BlockSpec((tm, part.shape[1]),
                                  functools.partial(lambda i, lo, n_t: (jnp.clip(i - lo, 0, n_t - 1), 0), lo=lo, n_t=n_t)))
        firsts.append(lo)
        lo += n_t
    return specs, tuple(firsts)


def _read_row_parts(refs, firsts, i):
    val = refs[0][...]
    for ref, lo in zip(refs[1:], firsts[1:]):
        val = jnp.where(i >= lo, ref[...], val)
    return val


def _ln_mod_kernel(*refs, firsts, tm, n_lat_rows, L, B, D):
    n = len(firsts)
    mod_ref, o_ref = refs[n], refs[n + 1]
    i = pl.program_id(0)
    r = _cond_row(i, tm, n_lat_rows, L, B)
    shift = mod_ref[pl.ds(r, 1), 0:D]
    scale = mod_ref[pl.ds(r, 1), D:2 * D]
    x = _read_row_parts(refs[:n], firsts, i)
    o_ref[...] = (_layer_norm_rows(x) * (1.0 + scale) + shift).astype(o_ref.dtype)


def _ln_mod_call(h_parts, mod_l, B, L, tm=256):
    T = sum(p.shape[0] for p in h_parts)
    D = h_parts[0].shape[1]
    specs, firsts = _row_part_specs(h_parts, tm)
    kern = functools.partial(_ln_mod_kernel, firsts=firsts, tm=tm, n_lat_rows=B * L, L=L, B=B, D=D)
    return pl.pallas_call(
        kern,
        grid=(T // tm,),
        in_specs=specs + [pl.BlockSpec(mod_l.shape, lambda i: (0, 0))],
        out_specs=pl.BlockSpec((tm, D), lambda i: (i, 0)),
        out_shape=jax.ShapeDtypeStruct((T, D), BF16),
        compiler_params=_params(("parallel",)),
        name="ln_modulate",
    )(*h_parts, mod_l)


def _wstat_matmul_kernel(*refs, widths):
    n = len(widths)
    a_refs, w_ref, o_ref, wbf_ref = refs[:n], refs[n], refs[n + 1], refs[n + 2]

    @pl.when(pl.program_id(1) == 0)
    def _():
        wbf_ref[...] = w_ref[0].astype(BF16)

    acc = None
    k0 = 0
    for a_ref, kw in zip(a_refs, widths):
        part = _dot(a_ref[...], wbf_ref[k0:k0 + kw, :])
        acc = part if acc is None else acc + part
        k0 += kw
    o_ref[...] = acc.astype(o_ref.dtype)


def _wstat_matmul_call(a_list, w_all, layer, n_rows, tm, tn, name):
    widths = tuple(a.shape[1] for a in a_list)
    _, K, N = w_all.shape
    assert sum(widths) == K and n_rows % tm == 0 and N % tn == 0
    return pl.pallas_call(
        functools.partial(_wstat_matmul_kernel, widths=widths),
        grid=(N // tn, n_rows // tm),
        in_specs=[pl.BlockSpec((tm, kw), lambda j, i: (i, 0)) for kw in widths]
        + [pl.BlockSpec((1, K, tn), lambda j, i: (layer, 0, j))],
        out_specs=pl.BlockSpec((tm, tn), lambda j, i: (i, j)),
        out_shape=jax.ShapeDtypeStruct((n_rows, N), BF16),
        scratch_shapes=[pltpu.VMEM((K, tn), BF16)],
        compiler_params=_params(("parallel", "arbitrary")),
        name=name,
    )(*a_list, w_all)


def _split_bf16(x):
    hi = x.astype(BF16)
    return hi, (x - hi.astype(F32)).astype(BF16)


def _post_mix_kernel(y_ref, *refs, firsts, tm, n_lat_rows, L, B, D, alpha):
    n = len(firsts)
    mod_ref, g_ref, b_ref, rw_ref, rb_ref, h1_ref, fin_ref, gates_ref, topi_ref = refs[n:]
    i = pl.program_id(0)
    r = _cond_row(i, tm, n_lat_rows, L, B)
    mrow = lambda slot: mod_ref[pl.ds(r, 1), slot * D:(slot + 1) * D]
    u = alpha * _read_row_parts(refs[:n], firsts, i) + mrow(2) * y_ref[...].astype(F32)
    h1 = _layer_norm_rows(u) * g_ref[...] + b_ref[...]
    h1_ref[...] = h1
    fin = _layer_norm_rows(h1) * (1.0 + mrow(4)) + mrow(3)
    fin_ref[...] = fin.astype(BF16)
    f_hi, f_lo = _split_bf16(fin)
    w_hi, w_lo = _split_bf16(rw_ref[...])
    logits = _dot(f_hi, w_hi) + _dot(f_hi, w_lo) + _dot(f_lo, w_hi) + rb_ref[...]
    lane = lax.broadcasted_iota(jnp.int32, logits.shape, 1)
    ninf = jnp.float32(-jnp.inf)
    xs = jnp.where(lane < N_EXPERTS, logits, ninf)
    sel_v, sel_i = [], []
    for _ in range(TOP_K):
        m = jnp.max(xs, -1, keepdims=True)
        idx = jnp.min(jnp.where(xs == m, lane, logits.shape[1]), -1, keepdims=True)
        sel_v.append(m)
        sel_i.append(idx)
        xs = jnp.where(lane == idx, ninf, xs)
    ex = [jnp.exp(v - sel_v[0]) for v in sel_v]
    den = ex[0]
    for e in ex[1:]:
        den = den + e
    gates = jnp.zeros(logits.shape, F32)
    topi = jnp.zeros(logits.shape, jnp.int32)
    for j in range(TOP_K):
        gates = jnp.where(lane == sel_i[j], ex[j] / den, gates)
        topi = jnp.where(lane == j, sel_i[j], topi)
    gates_ref[...] = gates
    topi_ref[...] = topi


def _post_mix_call(y, h_parts, mod_l, ln_g, ln_b, rw_pad, rb_pad, B, L, alpha, tm=256):
    n_out, D = y.shape
    NE = rw_pad.shape[1]
    h_specs, firsts = _row_part_specs(h_parts, tm)
    kern = functools.partial(_post_mix_kernel, firsts=firsts, tm=tm, n_lat_rows=B * L, L=L, B=B, D=D, alpha=alpha)
    row = lambda i: (i, 0)
    fixed = lambda i: (0, 0)
    return pl.pallas_call(
        kern,
        grid=(n_out // tm,),
        in_specs=[pl.BlockSpec((tm, D), row)] + h_specs
        + [pl.BlockSpec(mod_l.shape, fixed),
                  pl.BlockSpec((1, D), fixed), pl.BlockSpec((1, D), fixed),
                  pl.BlockSpec((D, NE), fixed), pl.BlockSpec((1, NE), fixed)],
        out_specs=[pl.BlockSpec((tm, D), row), pl.BlockSpec((tm, D), row),
                   pl.BlockSpec((tm, NE), row), pl.BlockSpec((tm, NE), row)],
        out_shape=[jax.ShapeDtypeStruct((n_out, D), F32), jax.ShapeDtypeStruct((n_out, D), BF16),
                   jax.ShapeDtypeStruct((n_out, NE), F32), jax.ShapeDtypeStruct((n_out, NE), jnp.int32)],
        compiler_params=_params(("parallel",)),
        name="post_mix_router",
    )(y, *h_parts, mod_l, ln_g.reshape(1, D), ln_b.reshape(1, D), rw_pad, rb_pad)


MOE_TILE = 512
MOE_ALIGN = 16


def _first_visit_of_expert(v, ve_ref):
    return (v == 0) | (ve_ref[jnp.maximum(v - 1, 0)] != ve_ref[v])


def _moe_up_kernel(start_ref, ve_ref, vout_ref, vv_ref, x_ref, w1_ref, b1_ref, o_ref, wbf_ref, *, DE):
    v = pl.program_id(0)

    @pl.when(_first_visit_of_expert(v, ve_ref))
    def _():
        wbf_ref[...] = w1_ref[0, 0].astype(BF16)

    @pl.when(vv_ref[v] > 0)
    def _():
        hcat = _dot(x_ref[...], wbf_ref[...]) + b1_ref[0, 0]
        gate = jnp.minimum(hcat[:, :DE], SWIGLU_LIMIT)
        up = jnp.clip(hcat[:, DE:], -SWIGLU_LIMIT, SWIGLU_LIMIT)
        o_ref[...] = ((up + 1.0) * (gate * _sigmoid(SWIGLU_ALPHA * gate))).astype(o_ref.dtype)


def _moe_down_kernel(start_ref, ve_ref, vout_ref, vv_ref, h_ref, rw_ref, w2_ref, b2_ref, o_ref, wbf_ref):
    v = pl.program_id(0)

    @pl.when(_first_visit_of_expert(v, ve_ref))
    def _():
        wbf_ref[...] = w2_ref[0, 0].astype(BF16)

    @pl.when(vv_ref[v] > 0)
    def _():
        y = (_dot(h_ref[...], wbf_ref[...]) + b2_ref[0, 0]) * rw_ref[...]
        o_ref[...] = y.astype(o_ref.dtype)


def _moe_call(xs, row_w, visits, w1_all, b1_all, w2_all, b2_all, layer):
    n_in, D = xs.shape
    _, E, _, DE2 = w1_all.shape
    DE = DE2 // 2
    tm = MOE_TILE
    nv = visits[0].shape[0]
    window = lambda v, st, ve, vo, vv: (pl.multiple_of(st[v], MOE_ALIGN), 0)
    out_tile = lambda v, st, ve, vo, vv: (vo[v], 0)
    expert = lambda v, st, ve, vo, vv: (layer, ve[v], 0, 0)
    hdn = pl.pallas_call(
        functools.partial(_moe_up_kernel, DE=DE),
        grid_spec=pltpu.PrefetchScalarGridSpec(
            num_scalar_prefetch=4, grid=(nv,),
            in_specs=[pl.BlockSpec((pl.Element(tm), pl.Element(D)), window),
                      pl.BlockSpec((1, 1, D, DE2), expert),
                      pl.BlockSpec((1, 1, 1, DE2), expert)],
            out_specs=pl.BlockSpec((tm, DE), out_tile),
            scratch_shapes=[pltpu.VMEM((D, DE2), BF16)]),
        out_shape=jax.ShapeDtypeStruct((nv * tm, DE), BF16),
        compiler_params=_params(("arbitrary",)),
        name="moe_up",
    )(*visits, xs, w1_all, b1_all.reshape(b1_all.shape[0], E, 1, DE2))
    return pl.pallas_call(
        _moe_down_kernel,
        grid_spec=pltpu.PrefetchScalarGridSpec(
            num_scalar_prefetch=4, grid=(nv,),
            in_specs=[pl.BlockSpec((tm, DE), out_tile),
                      pl.BlockSpec((pl.Element(tm), pl.Element(1)), window),
                      pl.BlockSpec((1, 1, DE, D), expert),
                      pl.BlockSpec((1, 1, 1, D), expert)],
            out_specs=pl.BlockSpec((tm, D), out_tile),
            scratch_shapes=[pltpu.VMEM((DE, D), BF16)]),
        out_shape=jax.ShapeDtypeStruct((nv * tm, D), BF16),
        compiler_params=_params(("arbitrary",)),
        name="moe_down",
    )(*visits, hdn, row_w, w2_all, b2_all.reshape(b2_all.shape[0], E, 1, D))


def _moe_dispatch(topi, gates, tm):
    T = topi.shape[0]
    E = N_EXPERTS
    n = T * TOP_K
    al = MOE_ALIGN
    flat_e = topi.reshape(-1)
    flat_w = jnp.take_along_axis(gates, topi, axis=1).reshape(-1)
    order = jnp.argsort(flat_e, stable=True).astype(jnp.int32)
    inv = jnp.argsort(order).astype(jnp.int32)
    e_col = jnp.arange(E, dtype=jnp.int32)[:, None]
    is_e = flat_e[None, :] == e_col
    counts = jnp.sum(is_e.astype(jnp.int32), axis=1)
    ends = jnp.cumsum(counts)
    offs = ends - counts
    seg_len = ((counts + al - 1) // al) * al
    seg_end = jnp.cumsum(seg_len)
    seg_start = seg_end - seg_len
    n_in = n + E * al + tm
    p = jnp.arange(n_in, dtype=jnp.int32)
    past = p[None, :] >= seg_end[:, None]
    rank = p - jnp.sum(jnp.where(past, (seg_len - counts)[:, None], 0), axis=0)
    end_of_p = ends[0] + jnp.sum(jnp.where(past[:-1], (ends[1:] - ends[:-1])[:, None], 0), axis=0)
    real = jnp.logical_and(rank < end_of_p, p < seg_end[-1])
    src = order[jnp.where(real, rank, 0)]
    row_tok = jnp.where(real, src // TOP_K, p % T)
    row_w = jnp.where(real, flat_w[src], 0.0).reshape(n_in, 1)
    n_tiles_e = (counts + tm - 1) // tm
    tile_end = jnp.cumsum(n_tiles_e)
    tile_start = tile_end - n_tiles_e
    total = tile_end[-1]
    nv = n // tm + E
    v = jnp.arange(nv, dtype=jnp.int32)
    ve = jnp.minimum(jnp.sum((v[:, None] >= tile_end[None, :]).astype(jnp.int32), axis=1), E - 1)
    valid = v < total
    last = total - 1
    ve = jnp.where(valid, ve, ve[last]).astype(jnp.int32)
    vout = jnp.where(valid, v, last).astype(jnp.int32)
    start = (seg_start[ve] + (vout - tile_start[ve]) * tm).astype(jnp.int32)
    pos = inv + jnp.sum(jnp.where(is_e, (tile_start * tm - offs)[:, None], 0), axis=0)
    pos_kmajor = pos.reshape(T, TOP_K).T.reshape(-1)
    return row_tok, row_w, pos_kmajor, (start, ve, vout, valid.astype(jnp.int32))


def _final_kernel(h_ref, *refs, tm, n_lat_rows, L, B, D, alpha, with_next):
    y_refs, rest = refs[:TOP_K], refs[TOP_K:]
    r = _cond_row(pl.program_id(0), tm, n_lat_rows, L, B)
    gate = rest[0][pl.ds(r, 1), 5 * D:6 * D]
    f = y_refs[0][...].astype(F32)
    for y_ref in y_refs[1:]:
        f = f + y_ref[...].astype(F32)
    u = alpha * h_ref[...] + gate * f
    h2 = _layer_norm_rows(u) * rest[1][...] + rest[2][...]
    if with_next:
        modn_ref, o_ref, a_ref = rest[3:]
        a_ref[...] = (_layer_norm_rows(h2) * (1.0 + modn_ref[pl.ds(r, 1), D:2 * D])
                      + modn_ref[pl.ds(r, 1), 0:D]).astype(a_ref.dtype)
    else:
        o_ref = rest[3]
    o_ref[...] = h2


def _final_call(h1, y_assign, mod_l, ln_g, ln_b, B, L, alpha, mod_next=None, tm=256):
    n, D = h1.shape
    nt = n // tm
    with_next = mod_next is not None
    kern = functools.partial(_final_kernel, tm=tm, n_lat_rows=B * L, L=L, B=B, D=D, alpha=alpha,
                             with_next=with_next)
    row = lambda i: (i, 0)
    fixed = lambda i: (0, 0)
    y_specs = [pl.BlockSpec((tm, D), functools.partial(lambda i, k: (k * nt + i, 0), k=k)) for k in range(TOP_K)]
    mod_spec = pl.BlockSpec(mod_l.shape, fixed)
    row_spec = pl.BlockSpec((tm, D), row)
    return pl.pallas_call(
        kern,
        grid=(nt,),
        in_specs=[row_spec] + y_specs + [mod_spec, pl.BlockSpec((1, D), fixed), pl.BlockSpec((1, D), fixed)]
        + ([mod_spec] if with_next else []),
        out_specs=[row_spec, row_spec] if with_next else row_spec,
        out_shape=([jax.ShapeDtypeStruct((n, D), F32), jax.ShapeDtypeStruct((n, D), BF16)] if with_next
                   else jax.ShapeDtypeStruct((n, D), F32)),
        compiler_params=_params(("parallel",)),
        name="final_ln",
    )(h1, *([y_assign] * TOP_K), mod_l, ln_g.reshape(1, D), ln_b.reshape(1, D),
      *([mod_next] if with_next else []))


def kernel(x, c, ctx, c_ctx, w_mod, b_mod, w_in, pool_w, pool_scale, na_rpb, hg_lb, hg_norm_g, w_out, ln1_g, ln1_b, ln2_g, ln2_b, router_w, router_b, exp_w1, exp_b1, exp_w2, exp_b2):
    B, L, D = x.shape
    Lc = ctx.shape[1]
    depth = w_in.shape[0]
    alpha = (2 * depth) ** 0.25
    pool_width = pool_w.shape[1] * pool_w.shape[2]
    na_width = na_rpb.shape[1] * HEAD_DIM
    hg_width = hg_norm_g.shape[1]
    na_heads, hg_heads = na_width // HEAD_DIM, hg_width // HEAD_DIM
    c_naq = pool_width // HEAD_DIM
    c_nak, c_nav = c_naq + na_heads, c_naq + 2 * na_heads
    c_hq = c_naq + 3 * na_heads
    c_hff, c_hfb, c_hi, c_hg = (c_hq + k * hg_heads for k in range(1, 5))
    assert B + 1 <= COND_ROWS

    h_parts = (x.reshape(B * L, D), ctx.reshape(B * Lc, D))
    cond =jnp.concatenate([c, c_ctx[None], jnp.zeros((COND_ROWS - B - 1, D), F32)], axis=0)
    mod = _adaln_call(cond, w_mod, b_mod)
    lb_soft = jax.nn.softmax(hg_lb.astype(F32), axis=1)
    lower = jnp.cumsum(lb_soft, axis=1) - lb_soft[:, :1]
    rw_pad = jnp.pad(router_w, ((0, 0), (0, 0), (0, HEAD_DIM - N_EXPERTS)))
    rb_pad = jnp.pad(router_b, ((0, 0), (0, HEAD_DIM - N_EXPERTS)))

    T = B * (L + Lc)
    n_row_tiles = 8
    a_in = _ln_mod_call(h_parts, mod[0], B, L)
    for l in range(depth):
        last = l == depth - 1
        px = _wstat_matmul_call([a_in], w_in, l, T, T // n_row_tiles, 512, "in_proj")
        a_mix = _pool_call(px, pool_w[l].astype(BF16), pool_scale[l], B, L, Lc)
        b_x, b_c = _na_call(px, na_rpb[l], B, L, Lc, c_naq, c_nak, c_nav, na_heads)
        b_mix = jnp.concatenate([b_x, b_c], axis=0)
        o_f, o_b = _hgrn_call(px, lower[0, l].reshape(hg_heads, 1, HEAD_DIM),
                              lower[1, l].reshape(hg_heads, 1, HEAD_DIM), B, L, Lc,
                              c_hq, c_hff, c_hfb, c_hi, hg_heads)
        c_mix = _readout_call(o_f, o_b, px, hg_norm_g[l], c_hg, T // n_row_tiles)
        n_out = B * L if last else T
        y = _wstat_matmul_call([a_mix, b_mix, c_mix], w_out, l, n_out, n_out // n_row_tiles, 512, "out_proj")
        h1, fin, gates, topi = _post_mix_call(y, h_parts, mod[l], ln1_g[l], ln1_b[l], rw_pad[l],
                                              rb_pad[l].reshape(1, -1), B, L, alpha)
        sorted_tok, sorted_w, pos_kmajor, visits = _moe_dispatch(topi[:, :TOP_K], gates, MOE_TILE)
        xs = fin.at[sorted_tok].get(mode="promise_in_bounds")
        ys = _moe_call(xs, sorted_w, visits, exp_w1, exp_b1, exp_w2, exp_b2, l)
        y_assign = ys.at[pos_kmajor].get(mode="promise_in_bounds")
        if last:
            hall = _final_call(h1, y_assign, mod[l], ln2_g[l], ln2_b[l], B, L, alpha)
        else:
            hall, a_in = _final_call(h1, y_assign, mod[l], ln2_g[l], ln2_b[l], B, L, alpha, mod_next=mod[l + 1])
        h_parts = (hall,)
    return hall.reshape(B, L, D)
```

```python
import functools

import jax
import jax.numpy as jnp
import numpy as np
from jax import lax
from jax.experimental import pallas as pl
from jax.experimental.pallas import tpu as pltpu

F32 = jnp.float32
BF16 = jnp.bfloat16

HEAD_DIM = 128
GRID_W = 64
POOL_WINDOWS = (2, 4, 8, 16)
NA_ROWS = 8
NA_COLS = 16
FORGET_EPS = 1e-20
N_EXPERTS = 32
TOP_K = 4
SWIGLU_LIMIT = 7.0
SWIGLU_ALPHA = 1.702
ROPE_BASE = 10000.0
LN_EPS = 1e-6

VMEM_LIMIT_BYTES = 56 * 1024 * 1024
HG_CHUNK = 256
NEG_BIG = -1e30


def _params(sem):
    return pltpu.CompilerParams(dimension_semantics=sem, vmem_limit_bytes=VMEM_LIMIT_BYTES)


def _dot(a, b):
    return jnp.dot(a, b, preferred_element_type=F32)


def _dot_nt(a, b):
    return lax.dot_general(a, b, (((1,), (1,)), ((), ())), preferred_element_type=F32)


def _sigmoid(x):
    return 1.0 / (1.0 + jnp.exp(-x))


LOG2E = 1.4426950408889634
HG_HALF = HG_CHUNK // 2
HG_HEADS_PER_STEP = 2


def _neg_abs(x):
    bits = lax.bitcast_convert_type(x, jnp.uint32) | jnp.uint32(0x80000000)
    return lax.bitcast_convert_type(bits, F32)


def _hgrn_half_scores(q, k, b2, g2, fc, lev, reverse):
    n = q.shape[0]
    row = lax.broadcasted_iota(jnp.int32, (n, HEAD_DIM), 0)
    a = jnp.where(lev == -1, _dot_nt(q.astype(BF16), k.astype(BF16)), 0.0)
    h, lvl = 1, 0
    while h < n:
        if h < 8:
            upper = (row & h) != 0
            t_role = jnp.logical_not(upper) if reverse else upper
            if h == 1:
                x = jnp.where(t_role, q * fc, k)
            elif h == 2:
                r4 = row & 3
                g_prev, g_next = pltpu.roll(g2, 1, 0), pltpu.roll(g2, n - 1, 0)
                if reverse:
                    e = jnp.where(r4 == 0, g2 + g_next, jnp.where(r4 == 1, g2, jnp.where(r4 == 2, 0.0, g_prev)))
                else:
                    e = jnp.where(r4 == 0, g_next, jnp.where(r4 == 1, 0.0, jnp.where(r4 == 2, g2, g2 + g_prev)))
                x = jnp.where(t_role, q, k) * jnp.exp2(e)
            else:
                m0 = h if reverse else h - 1
                b3 = b2.reshape(n // (2 * h), 2 * h, HEAD_DIM)
                ref = jnp.broadcast_to(b3[:, m0:m0 + 1, :], b3.shape).reshape(n, HEAD_DIM)
                x = jnp.where(t_role, q, k) * jnp.exp2(_neg_abs(b2 - ref))
            x = x.astype(BF16)
            a = jnp.where(lev == lvl, _dot_nt(x, x), a)
        else:
            nb = n // (2 * h)
            q4 = q.reshape(nb, 2, h, HEAD_DIM)
            k4 = k.reshape(nb, 2, h, HEAD_DIM)
            b4 = b2.reshape(nb, 2, h, HEAD_DIM)
            s_half, t_half, m_row = (1, 0, 0) if reverse else (0, 1, h - 1)
            ref = b4[:, s_half:s_half + 1, m_row:m_row + 1, :]
            xt = q4[:, t_half] * jnp.exp2(b4[:, t_half] - ref[:, 0])
            xs = k4[:, s_half] * jnp.exp2(ref[:, 0] - b4[:, s_half])
            halves = [xt, xs] if reverse else [xs, xt]
            x = jnp.stack(halves, axis=1).reshape(n, HEAD_DIM).astype(BF16)
            g = _dot_nt(xt.reshape(n // 2, HEAD_DIM).astype(BF16), x)
            a4 = a.reshape(nb, 2, h, n)
            lev_t = lev.reshape(nb, 2, h, n)[:, t_half]
            a_t = jnp.where(lev_t == lvl, g.reshape(nb, h, n), a4[:, t_half])
            halves = [a_t, a4[:, s_half]] if reverse else [a4[:, s_half], a_t]
            a = jnp.stack(halves, axis=1).reshape(n, n)
        h *= 2
        lvl += 1
    return a


def _hgrn_direction(q, z, v_bf, lb, st_ref, lev, tri_bf, reverse):
    C, H = HG_CHUNK, HG_HALF
    q = q * _sigmoid(q)
    sig = _sigmoid(z)
    fc = jnp.maximum(lb + (1.0 - lb) * sig, FORGET_EPS)
    g2 = jnp.log(fc) * LOG2E
    k = (1.0 - lb) * (1.0 - sig)

    g_hi = g2.astype(BF16)
    r1 = g2 - g_hi.astype(F32)
    g_mid = r1.astype(BF16)
    g_lo = (r1 - g_mid.astype(F32)).astype(BF16)
    b2 = _dot(tri_bf, g_hi) + _dot(tri_bf, g_mid) + _dot(tri_bf, g_lo)

    lo, hi = slice(0, H), slice(H, C)
    a_lo = _hgrn_half_scores(q[lo], k[lo], b2[lo], g2[lo], fc[lo], lev, reverse).astype(BF16)
    a_hi = _hgrn_half_scores(q[hi], k[hi], b2[hi], g2[hi], fc[hi], lev, reverse).astype(BF16)
    t_sl, s_sl, m = (lo, hi, H) if reverse else (hi, lo, H - 1)
    ref = b2[m:m + 1, :]
    qt = (q[t_sl] * jnp.exp2(b2[t_sl] - ref)).astype(BF16)
    ks = (k[s_sl] * jnp.exp2(ref - b2[s_sl])).astype(BF16)
    cross = _dot(_dot_nt(qt, ks).astype(BF16), v_bf[s_sl])
    o_lo = _dot(a_lo, v_bf[lo])
    o_hi = _dot(a_hi, v_bf[hi])
    o_intra = jnp.concatenate([o_lo + cross, o_hi] if reverse else [o_lo, o_hi + cross], axis=0)

    st = st_ref[...]
    tot = b2[0:1, :] if reverse else b2[C - 1:C, :]
    qdec = (q * jnp.exp2(b2)).astype(BF16)
    o_inter = _dot_nt(qdec, st.astype(BF16))
    kdec = (k * jnp.exp2(tot - b2)).astype(BF16)
    vt = v_bf.astype(F32).T.astype(BF16)
    st_ref[...] = jnp.exp2(tot) * st + _dot(vt, kdec)
    return o_intra + o_inter


def _hgrn_kernel(qf_ref, zf_ref, vf_ref, qb_ref, zb_ref, vb_ref, lbf_ref, lbb_ref, levf_ref, levb_ref,
                 trif_ref, trib_ref, of_ref, ob_ref, sf_ref, sb_ref):
    @pl.when(pl.program_id(2) == 0)
    def _():
        sf_ref[...] = jnp.zeros_like(sf_ref)
        sb_ref[...] = jnp.zeros_like(sb_ref)

    for hh in range(HG_HEADS_PER_STEP):
        sl = slice(hh * HEAD_DIM, (hh + 1) * HEAD_DIM)
        of_ref[:, sl] = _hgrn_direction(qf_ref[:, sl].astype(F32), zf_ref[:, sl].astype(F32), vf_ref[:, sl],
                                        lbf_ref[0][:, sl], sf_ref.at[hh], levf_ref[...], trif_ref[...],
                                        False).astype(of_ref.dtype)
        ob_ref[:, sl] = _hgrn_direction(qb_ref[:, sl].astype(F32), zb_ref[:, sl].astype(F32), vb_ref[:, sl],
                                        lbb_ref[0][:, sl], sb_ref.at[hh], levb_ref[...], trib_ref[...],
                                        True).astype(ob_ref.dtype)


def _hgrn_call(px, lb_f, lb_b, B, L, Lc, col_q, col_ff, col_fb, col_i, n_heads):
    C, H = HG_CHUNK, HG_HALF
    T = px.shape[0]
    nlc, ncc = L // C, Lc // C
    ns = nlc + ncc
    ctx0 = B * nlc

    def fwd_row(b, s):
        return jnp.where(s < ncc, ctx0 + b * ncc + s, b * nlc + s - ncc)

    def bwd_row(b, s):
        return jnp.where(s < ncc, ctx0 + b * ncc + (ncc - 1 - s), b * nlc + (nlc - 1 - (s - ncc)))

    G = HG_HEADS_PER_STEP
    GW = G * HEAD_DIM
    assert n_heads % G == 0 and all(c % G == 0 for c in (col_q, col_ff, col_fb, col_i))

    def spec(row_fn, col):
        return pl.BlockSpec((C, GW), lambda b, h, s: (row_fn(b, s), col // G + h))

    idx = np.arange(H)
    xr = idx[:, None] ^ idx[None, :]
    level = np.floor(np.log2(np.maximum(xr, 1))).astype(np.int32)
    lev_f = np.where(xr == 0, -1, np.where(idx[:, None] > idx[None, :], level, -2)).astype(np.int32)
    lev_b = np.ascontiguousarray(lev_f.T)
    idc = np.arange(C)
    tri_f = (idc[:, None] >= idc[None, :]).astype(np.float32)
    tri_b = (idc[:, None] <= idc[None, :]).astype(np.float32)
    const = lambda shape: pl.BlockSpec(shape, lambda b, h, s: (0,) * len(shape))
    lb_spec = pl.BlockSpec((1, 1, GW), lambda b, h, s: (h, 0, 0))
    out_sds = jax.ShapeDtypeStruct((T, n_heads * HEAD_DIM), BF16)
    state = pltpu.VMEM((G, HEAD_DIM, HEAD_DIM), F32)
    return pl.pallas_call(
        _hgrn_kernel,
        grid=(B, n_heads // G, ns),
        in_specs=[spec(fwd_row, col_q), spec(fwd_row, col_ff), spec(fwd_row, col_i),
                  spec(bwd_row, col_q), spec(bwd_row, col_fb), spec(bwd_row, col_i),
                  lb_spec, lb_spec, const((H, H)), const((H, H)), const((C, C)), const((C, C))],
        out_specs=[pl.BlockSpec((C, GW), lambda b, h, s: (fwd_row(b, s), h)),
                   pl.BlockSpec((C, GW), lambda b, h, s: (bwd_row(b, s), h))],
        out_shape=[out_sds, out_sds],
        scratch_shapes=[state, state],
        compiler_params=_params(("parallel", "parallel", "arbitrary")),
        name="hgrn_scan",
    )(px, px, px, px, px, px, lb_f.reshape(n_heads // G, 1, GW), lb_b.reshape(n_heads // G, 1, GW),
      jnp.asarray(lev_f), jnp.asarray(lev_b), jnp.asarray(tri_f, BF16), jnp.asarray(tri_b, BF16))


NA_QROWS = 4
NA_KROWS = NA_QROWS + NA_ROWS


def _na_block_cases(rows):
    assert rows % NA_QROWS == 0 and rows >= 2 * NA_ROWS
    nb = rows // NA_QROWS
    cases = []
    for i in (0, 1, nb - 1):
        r0 = NA_QROWS * i
        ks = min(max(r0 - NA_ROWS // 2, 0), rows - NA_KROWS)
        per_q = []
        for qr in range(NA_QROWS):
            r = r0 + qr
            rs = min(max(r - NA_ROWS // 2, 0), rows - NA_ROWS)
            per_q.append((rs - ks, ks - r + NA_ROWS - 1))
        cases.append(per_q)
    return cases


def _na_kernel(rpb_ref, q_ref, k_ref, v_ref, qc_ref, kc_ref, vc_ref, cos_ref, sin_ref,
               ox_ref, oc_ref, qr_ref, kr_ref, tb_ref, bias_ref, *, rows):
    h = pl.program_id(0)
    L = q_ref.shape[0]
    W = GRID_W
    scale = HEAD_DIM ** -0.5
    n_dr = 2 * NA_ROWS - 1
    n_dc = 2 * NA_COLS - 1

    @pl.when(pl.program_id(1) == 0)
    def _():
        c_i = lax.broadcasted_iota(jnp.int32, (W, W), 0)
        kc_i = lax.broadcasted_iota(jnp.int32, (W, W), 1)
        cs = jnp.clip(c_i - NA_COLS // 2, 0, W - NA_COLS)
        in_win = (kc_i >= cs) & (kc_i < cs + NA_COLS)
        c_off = kc_i - c_i + (NA_COLS - 1)
        for dr in range(n_dr):
            acc = jnp.zeros((W, W), F32)
            for j in range(n_dc):
                acc = jnp.where(c_off == j, rpb_ref[h, dr, j], acc)
            tb_ref[dr] = jnp.where(in_win, acc, NEG_BIG)
        neg = jnp.full((W, W), NEG_BIG, F32)
        for ci, per_q in enumerate(_na_block_cases(rows)):
            for qr, (j0, dr0) in enumerate(per_q):
                for j in range(NA_KROWS):
                    valid = j0 <= j < j0 + NA_ROWS
                    blk = tb_ref[dr0 + j] if valid else neg
                    bias_ref[ci, qr * W:(qr + 1) * W, j * W:(j + 1) * W] = blk

    RC = 256
    lane = lax.broadcasted_iota(jnp.int32, (RC, HEAD_DIM), 1)
    first = (lane & (HEAD_DIM // 2 - 1)) < HEAD_DIM // 4

    def rope_chunk(i, carry):
        sl = pl.ds(pl.multiple_of(i * RC, RC), RC)
        cos = cos_ref[sl, :]
        sin = sin_ref[sl, :]
        for src, dst, mul in ((q_ref, qr_ref, scale), (k_ref, kr_ref, 1.0)):
            t = src[sl, :].astype(F32)
            partner = jnp.where(first, pltpu.roll(t, HEAD_DIM - HEAD_DIM // 4, 1), pltpu.roll(t, HEAD_DIM // 4, 1))
            dst[sl, :] = ((t * cos + partner * sin) * mul).astype(BF16)
        return carry

    lax.fori_loop(0, L // RC, rope_chunk, 0)

    kc = kc_ref[...]
    vc = vc_ref[...]
    nb = rows // NA_QROWS
    QB = NA_QROWS * W
    KB = NA_KROWS * W

    def block(i, carry):
        ks = jnp.clip(i * NA_QROWS - NA_ROWS // 2, 0, rows - NA_KROWS)
        case = jnp.where(i == 0, 0, jnp.where(i == nb - 1, 2, 1))
        qs = pl.ds(pl.multiple_of(i * QB, QB), QB)
        kslice = pl.ds(pl.multiple_of(ks * W, W), KB)
        qb = qr_ref[qs, :]
        s_loc = _dot_nt(qb, kr_ref[kslice, :]) + bias_ref[case]
        s_ctx = _dot_nt(qb, kc)
        m = jnp.maximum(jnp.max(s_loc, -1, keepdims=True), jnp.max(s_ctx, -1, keepdims=True))
        p_loc = jnp.exp(s_loc - m)
        p_ctx = jnp.exp(s_ctx - m)
        den = jnp.sum(p_loc, -1, keepdims=True) + jnp.sum(p_ctx, -1, keepdims=True)
        o = _dot(p_loc.astype(BF16), v_ref[kslice, :]) + _dot(p_ctx.astype(BF16), vc)
        ox_ref[qs, :] = (o / den).astype(ox_ref.dtype)
        return carry

    lax.fori_loop(0, nb, block, 0, unroll=4)

    qc = (qc_ref[...].astype(F32) * scale).astype(BF16)
    s = _dot_nt(qc, kc)
    p = jnp.exp(s - jnp.max(s, -1, keepdims=True))
    oc = _dot(p.astype(BF16), vc) / jnp.sum(p, -1, keepdims=True)
    oc_ref[...] = oc.astype(oc_ref.dtype)


def _rope_tables(L):
    pos = jnp.arange(L)
    half = HEAD_DIM // 2
    inv = ROPE_BASE ** (-jnp.arange(0, half, 2, dtype=F32) / half)

    def tab(p):
        ang = p.astype(F32)[:, None] * inv[None]
        cos, sin = jnp.cos(ang), jnp.sin(ang)
        return jnp.concatenate([cos, cos], -1), jnp.concatenate([-sin, sin], -1)

    c_r, s_r = tab(pos // GRID_W)
    c_c, s_c = tab(pos % GRID_W)
    return jnp.concatenate([c_r, c_c], -1), jnp.concatenate([s_r, s_c], -1)


def _na_call(px, rpb, B, L, Lc, col_q, col_k, col_v, n_heads):
    rows = L // GRID_W
    cos, sin = _rope_tables(L)
    ctx0 = B * L // Lc

    def lat(col):
        return pl.BlockSpec((L, HEAD_DIM), lambda h, b: (b, col + h))

    def cx(col):
        return pl.BlockSpec((Lc, HEAD_DIM), lambda h, b: (ctx0 + b, col + h))

    tab = pl.BlockSpec((L, HEAD_DIM), lambda h, b: (0, 0))
    W = GRID_W
    return pl.pallas_call(
        functools.partial(_na_kernel, rows=rows),
        grid=(n_heads, B),
        in_specs=[pl.BlockSpec(memory_space=pltpu.SMEM), lat(col_q), lat(col_k), lat(col_v),
                  cx(col_q), cx(col_k), cx(col_v), tab, tab],
        out_specs=[pl.BlockSpec((L, HEAD_DIM), lambda h, b: (b, h)),
                   pl.BlockSpec((Lc, HEAD_DIM), lambda h, b: (b, h))],
        out_shape=[jax.ShapeDtypeStruct((B * L, n_heads * HEAD_DIM), BF16),
                   jax.ShapeDtypeStruct((B * Lc, n_heads * HEAD_DIM), BF16)],
        scratch_shapes=[pltpu.VMEM((L, HEAD_DIM), BF16), pltpu.VMEM((L, HEAD_DIM), BF16),
                        pltpu.VMEM((2 * NA_ROWS - 1, W, W), F32),
                        pltpu.VMEM((3, NA_QROWS * W, NA_KROWS * W), F32)],
        compiler_params=_params(("parallel", "arbitrary")),
        name="neighborhood_attention",
    )(rpb, px, px, px, px, px, px, cos, sin)


POOL_TILE = 256
POOL_HALO = 16


def _pool_kernel(prev_ref, cur_ref, next_ref, w_ref, scale_ref, o_ref, *, n_lat_tiles, nl, nc):
    i = pl.program_id(0)
    is_lat = i < n_lat_tiles
    j = jnp.where(is_lat, i % nl, (i - n_lat_tiles) % nc)
    n = jnp.where(is_lat, nl, nc)
    TQ, HL = POOL_TILE, POOL_HALO
    lo_min = jnp.where(j > 0, -HL, 0)
    hi_max = jnp.where(j < n - 1, TQ + HL, TQ)
    src = jnp.concatenate([prev_ref[...], cur_ref[...], next_ref[...]], axis=0)
    t = lax.broadcasted_iota(jnp.int32, (TQ, TQ + 2 * HL), 0)
    s = lax.broadcasted_iota(jnp.int32, (TQ, TQ + 2 * HL), 1) - HL
    Dg = w_ref.shape[1]
    t1 = lax.broadcasted_iota(jnp.int32, (TQ, Dg), 0)
    for g, w in enumerate(POOL_WINDOWS):
        lo = jnp.maximum(t - w // 2, lo_min)
        hi = jnp.minimum(t + (w - w // 2), hi_max)
        band = jnp.where(s >= lo, jnp.where(s < hi, 1.0, 0.0), 0.0).astype(BF16)
        cnt = (jnp.minimum(t1 + (w - w // 2), hi_max) - jnp.maximum(t1 - w // 2, lo_min)).astype(F32)
        cols = slice(g * Dg, (g + 1) * Dg)
        win = _dot(band, src[:, cols])
        d = win / cnt - cur_ref[:, cols].astype(F32)
        y = _dot(d.astype(BF16), w_ref[g]) * scale_ref[:, cols]
        o_ref[:, cols] = y.astype(o_ref.dtype)


def _pool_call(px, pool_w_bf, pool_scale, B, L, Lc):
    T = px.shape[0]
    G, Dg, _ = pool_w_bf.shape
    PW = G * Dg
    TQ, HL = POOL_TILE, POOL_HALO
    r = TQ // HL
    n_tiles = T // TQ
    n_halo = T // HL
    kern = functools.partial(_pool_kernel, n_lat_tiles=B * L // TQ, nl=L // TQ, nc=Lc // TQ)
    return pl.pallas_call(
        kern,
        grid=(n_tiles,),
        in_specs=[pl.BlockSpec((HL, PW), lambda i: (jnp.maximum(i * r - 1, 0), 0)),
                  pl.BlockSpec((TQ, PW), lambda i: (i, 0)),
                  pl.BlockSpec((HL, PW), lambda i: (jnp.minimum((i + 1) * r, n_halo - 1), 0)),
                  pl.BlockSpec((G, Dg, Dg), lambda i: (0, 0, 0)),
                  pl.BlockSpec((1, PW), lambda i: (0, 0))],
        out_specs=pl.BlockSpec((TQ, PW), lambda i: (i, 0)),
        out_shape=jax.ShapeDtypeStruct((T, PW), BF16),
        compiler_params=_params(("parallel",)),
        name="multiscale_pool",
    )(px, px, px, pool_w_bf, pool_scale.reshape(1, PW).astype(F32))


READOUT_HEADS = 4


def _readout_kernel(of_ref, ob_ref, gate_ref, ng_ref, o_ref):
    for hh in range(READOUT_HEADS):
        sl = slice(hh * HEAD_DIM, (hh + 1) * HEAD_DIM)
        o = of_ref[:, sl].astype(F32) + ob_ref[:, sl].astype(F32)
        o = o * lax.rsqrt(jnp.mean(o * o, -1, keepdims=True) + LN_EPS)
        g = gate_ref[:, sl].astype(F32)
        o_ref[:, sl] = (o * ng_ref[:, sl] * (g * _sigmoid(g))).astype(o_ref.dtype)


def _readout_call(o_f, o_b, px, norm_g, col_gate, tm):
    T, HW = o_f.shape
    GW = READOUT_HEADS * HEAD_DIM
    assert HW % GW == 0 and col_gate % READOUT_HEADS == 0
    blk = lambda col: pl.BlockSpec((tm, GW), lambda i, h: (i, col // READOUT_HEADS + h))
    return pl.pallas_call(
        _readout_kernel,
        grid=(T // tm, HW // GW),
        in_specs=[blk(0), blk(0), blk(col_gate), pl.BlockSpec((1, GW), lambda i, h: (0, h))],
        out_specs=blk(0),
        out_shape=jax.ShapeDtypeStruct((T, HW), BF16),
        compiler_params=_params(("parallel", "parallel")),
        name="hgrn_readout",
    )(o_f, o_b, px, norm_g.reshape(1, HW).astype(F32))


COND_ROWS = 16


def _adaln_kernel(cond_ref, w_ref, b_ref, o_ref):
    cnd = cond_ref[...]
    a = (cnd * _sigmoid(cnd)).astype(BF16)
    o_ref[0] = _dot(a, w_ref[0].astype(BF16)) + b_ref[0]


def _adaln_call(cond, w_mod, b_mod, tn=1024):
    depth, D, N = w_mod.shape
    return pl.pallas_call(
        _adaln_kernel,
        grid=(depth, N // tn),
        in_specs=[pl.BlockSpec((COND_ROWS, D), lambda l, j: (0, 0)),
                  pl.BlockSpec((1, D, tn), lambda l, j: (l, 0, j)),
                  pl.BlockSpec((1, 1, tn), lambda l, j: (l, 0, j))],
        out_specs=pl.BlockSpec((1, COND_ROWS, tn), lambda l, j: (l, 0, j)),
        out_shape=jax.ShapeDtypeStruct((depth, COND_ROWS, N), F32),
        compiler_params=_params(("parallel", "parallel")),
        name="adaln",
    )(cond, w_mod, b_mod.reshape(depth, 1, N))


def _cond_row(i, tm, n_lat_rows, L, B):
    return jnp.where(i * tm < n_lat_rows, (i * tm) // L, B)


def _layer_norm_rows(x):
    mu = jnp.mean(x, -1, keepdims=True)
    xc = x - mu
    var = jnp.mean(xc * xc, -1, keepdims=True)
    return xc * lax.rsqrt(var + LN_EPS)


def _row_part_specs(parts, tm):
    specs, firsts, lo = [], [], 0
    for part in parts:
        n_t = part.shape[0] // tm
        assert part.shape[0] % tm == 0
        specs.append(pl.BlockSpec((tm, part.shape[1]),
                                  functools.partial(lambda i, lo, n_t: (jnp.clip(i - lo, 0, n_t - 1), 0), lo=lo, n_t=n_t)))
        firsts.append(lo)
        lo += n_t
    return specs, tuple(firsts)


def _read_row_parts(refs, firsts, i):
    val = refs[0][...]
    for ref, lo in zip(refs[1:], firsts[1:]):
        val = jnp.where(i >= lo, ref[...], val)
    return val


def _ln_mod_kernel(*refs, firsts, tm, n_lat_rows, L, B, D):
    n = len(firsts)
    mod_ref, o_ref = refs[n], refs[n + 1]
    i = pl.program_id(0)
    r = _cond_row(i, tm, n_lat_rows, L, B)
    shift = mod_ref[pl.ds(r, 1), 0:D]
    scale = mod_ref[pl.ds(r, 1), D:2 * D]
    x = _read_row_parts(refs[:n], firsts, i)
    o_ref[...] = (_layer_norm_rows(x) * (1.0 + scale) + shift).astype(o_ref.dtype)


def _ln_mod_call(h_parts, mod_l, B, L, tm=256):
    T = sum(p.shape[0] for p in h_parts)
    D = h_parts[0].shape[1]
    specs, firsts = _row_part_specs(h_parts, tm)
    kern = functools.partial(_ln_mod_kernel, firsts=firsts, tm=tm, n_lat_rows=B * L, L=L, B=B, D=D)
    return pl.pallas_call(
        kern,
        grid=(T // tm,),
        in_specs=specs + [pl.BlockSpec(mod_l.shape, lambda i: (0, 0))],
        out_specs=pl.BlockSpec((tm, D), lambda i: (i, 0)),
        out_shape=jax.ShapeDtypeStruct((T, D), BF16),
        compiler_params=_params(("parallel",)),
        name="ln_modulate",
    )(*h_parts, mod_l)


def _wstat_matmul_kernel(*refs, widths):
    n = len(widths)
    a_refs, w_ref, o_ref, wbf_ref = refs[:n], refs[n], refs[n + 1], refs[n + 2]

    @pl.when(pl.program_id(1) == 0)
    def _():
        wbf_ref[...] = w_ref[0].astype(BF16)

    acc = None
    k0 = 0
    for a_ref, kw in zip(a_refs, widths):
        part = _dot(a_ref[...], wbf_ref[k0:k0 + kw, :])
        acc = part if acc is None else acc + part
        k0 += kw
    o_ref[...] = acc.astype(o_ref.dtype)


def _wstat_matmul_call(a_list, w_all, layer, n_rows, tm, tn, name):
    widths = tuple(a.shape[1] for a in a_list)
    _, K, N = w_all.shape
    assert sum(widths) == K and n_rows % tm == 0 and N % tn == 0
    return pl.pallas_call(
        functools.partial(_wstat_matmul_kernel, widths=widths),
        grid=(N // tn, n_rows // tm),
        in_specs=[pl.BlockSpec((tm, kw), lambda j, i: (i, 0)) for kw in widths]
        + [pl.BlockSpec((1, K, tn), lambda j, i: (layer, 0, j))],
        out_specs=pl.BlockSpec((tm, tn), lambda j, i: (i, j)),
        out_shape=jax.ShapeDtypeStruct((n_rows, N), BF16),
        scratch_shapes=[pltpu.VMEM((K, tn), BF16)],
        compiler_params=_params(("parallel", "arbitrary")),
        name=name,
    )(*a_list, w_all)


def _split_bf16(x):
    hi = x.astype(BF16)
    return hi, (x - hi.astype(F32)).astype(BF16)


def _post_mix_kernel(y_ref, *refs, firsts, tm, n_lat_rows, L, B, D, alpha):
    n = len(firsts)
    mod_ref, g_ref, b_ref, rw_ref, rb_ref, h1_ref, fin_ref, gates_ref, topi_ref = refs[n:]
    i = pl.program_id(0)
    r = _cond_row(i, tm, n_lat_rows, L, B)
    mrow = lambda slot: mod_ref[pl.ds(r, 1), slot * D:(slot + 1) * D]
    u = alpha * _read_row_parts(refs[:n], firsts, i) + mrow(2) * y_ref[...].astype(F32)
    h1 = _layer_norm_rows(u) * g_ref[...] + b_ref[...]
    h1_ref[...] = h1
    fin = _layer_norm_rows(h1) * (1.0 + mrow(4)) + mrow(3)
    fin_ref[...] = fin.astype(BF16)
    f_hi, f_lo = _split_bf16(fin)
    w_hi, w_lo = _split_bf16(rw_ref[...])
    logits = _dot(f_hi, w_hi) + _dot(f_hi, w_lo) + _dot(f_lo, w_hi) + rb_ref[...]
    lane = lax.broadcasted_iota(jnp.int32, logits.shape, 1)
    ninf = jnp.float32(-jnp.inf)
    xs = jnp.where(lane < N_EXPERTS, logits, ninf)
    sel_v, sel_i = [], []
    for _ in range(TOP_K):
        m = jnp.max(xs, -1, keepdims=True)
        idx = jnp.min(jnp.where(xs == m, lane, logits.shape[1]), -1, keepdims=True)
        sel_v.append(m)
        sel_i.append(idx)
        xs = jnp.where(lane == idx, ninf, xs)
    ex = [jnp.exp(v - sel_v[0]) for v in sel_v]
    den = ex[0]
    for e in ex[1:]:
        den = den + e
    gates = jnp.zeros(logits.shape, F32)
    topi = jnp.zeros(logits.shape, jnp.int32)
    for j in range(TOP_K):
        gates = jnp.where(lane == sel_i[j], ex[j] / den, gates)
        topi = jnp.where(lane == j, sel_i[j], topi)
    gates_ref[...] = gates
    topi_ref[...] = topi


def _post_mix_call(y, h_parts, mod_l, ln_g, ln_b, rw_pad, rb_pad, B, L, alpha, tm=256):
    n_out, D = y.shape
    NE = rw_pad.shape[1]
    h_specs, firsts = _row_part_specs(h_parts, tm)
    kern = functools.partial(_post_mix_kernel, firsts=firsts, tm=tm, n_lat_rows=B * L, L=L, B=B, D=D, alpha=alpha)
    row = lambda i: (i, 0)
    fixed = lambda i: (0, 0)
    return pl.pallas_call(
        kern,
        grid=(n_out // tm,),
        in_specs=[pl.BlockSpec((tm, D), row)] + h_specs
        + [pl.BlockSpec(mod_l.shape, fixed),
                  pl.BlockSpec((1, D), fixed), pl.BlockSpec((1, D), fixed),
                  pl.BlockSpec((D, NE), fixed), pl.BlockSpec((1, NE), fixed)],
        out_specs=[pl.BlockSpec((tm, D), row), pl.BlockSpec((tm, D), row),
                   pl.BlockSpec((tm, NE), row), pl.BlockSpec((tm, NE), row)],
        out_shape=[jax.ShapeDtypeStruct((n_out, D), F32), jax.ShapeDtypeStruct((n_out, D), BF16),
                   jax.ShapeDtypeStruct((n_out, NE), F32), jax.ShapeDtypeStruct((n_out, NE), jnp.int32)],
        compiler_params=_params(("parallel",)),
        name="post_mix_router",
    )(y, *h_parts, mod_l, ln_g.reshape(1, D), ln_b.reshape(1, D), rw_pad, rb_pad)


MOE_TILE = 512
MOE_ALIGN = 16


def _first_visit_of_expert(v, ve_ref):
    return (v == 0) | (ve_ref[jnp.maximum(v - 1, 0)] != ve_ref[v])


VISIT_SKIP, VISIT_FULL, VISIT_HALF = 0, 1, 2


def _run_visit(kind, run, tm):
    @pl.when(kind == VISIT_FULL)
    def _():
        run(slice(0, tm))

    @pl.when(kind == VISIT_HALF)
    def _():
        run(slice(0, tm // 2))


def _moe_up_kernel(start_ref, ve_ref, vout_ref, vv_ref, x_ref, w1_ref, b1_ref, o_ref, wbf_ref, *, DE):
    v = pl.program_id(0)

    @pl.when(_first_visit_of_expert(v, ve_ref))
    def _():
        wbf_ref[...] = w1_ref[0, 0].astype(BF16)

    def run(rows):
        hcat = _dot(x_ref[rows, :], wbf_ref[...]) + b1_ref[0, 0]
        gate = jnp.minimum(hcat[:, :DE], SWIGLU_LIMIT)
        up = jnp.clip(hcat[:, DE:], -SWIGLU_LIMIT, SWIGLU_LIMIT)
        o_ref[rows, :] = ((up + 1.0) * (gate * _sigmoid(SWIGLU_ALPHA * gate))).astype(o_ref.dtype)

    _run_visit(vv_ref[v], run, x_ref.shape[0])


def _moe_down_kernel(start_ref, ve_ref, vout_ref, vv_ref, h_ref, rw_ref, w2_ref, b2_ref, o_ref, wbf_ref):
    v = pl.program_id(0)

    @pl.when(_first_visit_of_expert(v, ve_ref))
    def _():
        wbf_ref[...] = w2_ref[0, 0].astype(BF16)

    def run(rows):
        y = (_dot(h_ref[rows, :], wbf_ref[...]) + b2_ref[0, 0]) * rw_ref[rows, :]
        o_ref[rows, :] = y.astype(o_ref.dtype)

    _run_visit(vv_ref[v], run, h_ref.shape[0])


def _moe_call(xs, row_w, visits, w1_all, b1_all, w2_all, b2_all, layer):
    n_in, D = xs.shape
    _, E, _, DE2 = w1_all.shape
    DE = DE2 // 2
    tm = MOE_TILE
    nv = visits[0].shape[0]
    window = lambda v, st, ve, vo, vv: (pl.multiple_of(st[v], MOE_ALIGN), 0)
    out_tile = lambda v, st, ve, vo, vv: (vo[v], 0)
    expert = lambda v, st, ve, vo, vv: (layer, ve[v], 0, 0)
    hdn = pl.pallas_call(
        functools.partial(_moe_up_kernel, DE=DE),
        grid_spec=pltpu.PrefetchScalarGridSpec(
            num_scalar_prefetch=4, grid=(nv,),
            in_specs=[pl.BlockSpec((pl.Element(tm), pl.Element(D)), window),
                      pl.BlockSpec((1, 1, D, DE2), expert),
                      pl.BlockSpec((1, 1, 1, DE2), expert)],
            out_specs=pl.BlockSpec((tm, DE), out_tile),
            scratch_shapes=[pltpu.VMEM((D, DE2), BF16)]),
        out_shape=jax.ShapeDtypeStruct((nv * tm, DE), BF16),
        compiler_params=_params(("arbitrary",)),
        name="moe_up",
    )(*visits, xs, w1_all, b1_all.reshape(b1_all.shape[0], E, 1, DE2))
    return pl.pallas_call(
        _moe_down_kernel,
        grid_spec=pltpu.PrefetchScalarGridSpec(
            num_scalar_prefetch=4, grid=(nv,),
            in_specs=[pl.BlockSpec((tm, DE), out_tile),
                      pl.BlockSpec((pl.Element(tm), pl.Element(1)), window),
                      pl.BlockSpec((1, 1, DE, D), expert),
                      pl.BlockSpec((1, 1, 1, D), expert)],
            out_specs=pl.BlockSpec((tm, D), out_tile),
            scratch_shapes=[pltpu.VMEM((DE, D), BF16)]),
        out_shape=jax.ShapeDtypeStruct((nv * tm, D), BF16),
        compiler_params=_params(("arbitrary",)),
        name="moe_down",
    )(*visits, hdn, row_w, w2_all, b2_all.reshape(b2_all.shape[0], E, 1, D))


def _moe_dispatch(topi, gates, tm):
    T = topi.shape[0]
    E = N_EXPERTS
    n = T * TOP_K
    al = MOE_ALIGN
    flat_e = topi.reshape(-1)
    flat_w = jnp.take_along_axis(gates, topi, axis=1).reshape(-1)
    order = jnp.argsort(flat_e, stable=True).astype(jnp.int32)
    inv = jnp.argsort(order).astype(jnp.int32)
    e_col = jnp.arange(E, dtype=jnp.int32)[:, None]
    is_e = flat_e[None, :] == e_col
    counts = jnp.sum(is_e.astype(jnp.int32), axis=1)
    ends = jnp.cumsum(counts)
    offs = ends - counts
    seg_len = ((counts + al - 1) // al) * al
    seg_end = jnp.cumsum(seg_len)
    seg_start = seg_end - seg_len
    n_in = n + E * al + tm
    p = jnp.arange(n_in, dtype=jnp.int32)
    past = p[None, :] >= seg_end[:, None]
    rank = p - jnp.sum(jnp.where(past, (seg_len - counts)[:, None], 0), axis=0)
    end_of_p = ends[0] + jnp.sum(jnp.where(past[:-1], (ends[1:] - ends[:-1])[:, None], 0), axis=0)
    real = jnp.logical_and(rank < end_of_p, p < seg_end[-1])
    src = order[jnp.where(real, rank, 0)]
    row_tok = jnp.where(real, src // TOP_K, p % T)
    row_w = jnp.where(real, flat_w[src], 0.0).reshape(n_in, 1)
    n_tiles_e = (counts + tm - 1) // tm
    tile_end = jnp.cumsum(n_tiles_e)
    tile_start = tile_end - n_tiles_e
    total = tile_end[-1]
    nv = n // tm + E
    v = jnp.arange(nv, dtype=jnp.int32)
    ve = jnp.minimum(jnp.sum((v[:, None] >= tile_end[None, :]).astype(jnp.int32), axis=1), E - 1)
    valid = v < total
    last = total - 1
    ve = jnp.where(valid, ve, ve[last]).astype(jnp.int32)
    vout = jnp.where(valid, v, last).astype(jnp.int32)
    start = (seg_start[ve] + (vout - tile_start[ve]) * tm).astype(jnp.int32)
    rem = counts % tm
    half_last = jnp.logical_and(rem > 0, rem <= tm // 2)
    is_half = jnp.logical_and(v == tile_end[ve] - 1, half_last[ve])
    kind = jnp.where(valid, jnp.where(is_half, VISIT_HALF, VISIT_FULL), VISIT_SKIP).astype(jnp.int32)
    pos = inv + jnp.sum(jnp.where(is_e, (tile_start * tm - offs)[:, None], 0), axis=0)
    pos_kmajor = pos.reshape(T, TOP_K).T.reshape(-1)
    return row_tok, row_w, pos_kmajor, (start, ve, vout, kind)


def _final_kernel(h_ref, *refs, tm, n_lat_rows, L, B, D, alpha, with_next):
    y_refs, rest = refs[:TOP_K], refs[TOP_K:]
    r = _cond_row(pl.program_id(0), tm, n_lat_rows, L, B)
    gate = rest[0][pl.ds(r, 1), 5 * D:6 * D]
    f = y_refs[0][...].astype(F32)
    for y_ref in y_refs[1:]:
        f = f + y_ref[...].astype(F32)
    u = alpha * h_ref[...] + gate * f
    h2 = _layer_norm_rows(u) * rest[1][...] + rest[2][...]
    if with_next:
        modn_ref, o_ref, a_ref = rest[3:]
        a_ref[...] = (_layer_norm_rows(h2) * (1.0 + modn_ref[pl.ds(r, 1), D:2 * D])
                      + modn_ref[pl.ds(r, 1), 0:D]).astype(a_ref.dtype)
    else:
        o_ref = rest[3]
    o_ref[...] = h2


def _final_call(h1, y_assign, mod_l, ln_g, ln_b, B, L, alpha, mod_next=None, tm=256):
    n, D = h1.shape
    nt = n // tm
    with_next = mod_next is not None
    kern = functools.partial(_final_kernel, tm=tm, n_lat_rows=B * L, L=L, B=B, D=D, alpha=alpha,
                             with_next=with_next)
    row = lambda i: (i, 0)
    fixed = lambda i: (0, 0)
    y_specs = [pl.BlockSpec((tm, D), functools.partial(lambda i, k: (k * nt + i, 0), k=k)) for k in range(TOP_K)]
    mod_spec = pl.BlockSpec(mod_l.shape, fixed)
    row_spec = pl.BlockSpec((tm, D), row)
    return pl.pallas_call(
        kern,
        grid=(nt,),
        in_specs=[row_spec] + y_specs + [mod_spec, pl.BlockSpec((1, D), fixed), pl.BlockSpec((1, D), fixed)]
        + ([mod_spec] if with_next else []),
        out_specs=[row_spec, row_spec] if with_next else row_spec,
        out_shape=([jax.ShapeDtypeStruct((n, D), F32), jax.ShapeDtypeStruct((n, D), BF16)] if with_next
                   else jax.ShapeDtypeStruct((n, D), F32)),
        compiler_params=_params(("parallel",)),
        name="final_ln",
    )(h1, *([y_assign] * TOP_K), mod_l, ln_g.reshape(1, D), ln_b.reshape(1, D),
      *([mod_next] if with_next else []))


def kernel(x, c, ctx, c_ctx, w_mod, b_mod, w_in, pool_w, pool_scale, na_rpb, hg_lb, hg_norm_g, w_out, ln1_g, ln1_b, ln2_g, ln2_b, router_w, router_b, exp_w1, exp_b1, exp_w2, exp_b2):
    B, L, D = x.shape
    Lc = ctx.shape[1]
    depth = w_in.shape[0]
    alpha = (2 * depth) ** 0.25
    pool_width = pool_w.shape[1] * pool_w.shape[2]
    na_width = na_rpb.shape[1] * HEAD_DIM
    hg_width = hg_norm_g.shape[1]
    na_heads, hg_heads = na_width // HEAD_DIM, hg_width // HEAD_DIM
    c_naq = pool_width // HEAD_DIM
    c_nak, c_nav = c_naq + na_heads, c_naq + 2 * na_heads
    c_hq = c_naq + 3 * na_heads
    c_hff, c_hfb, c_hi, c_hg = (c_hq + k * hg_heads for k in range(1, 5))
    assert B + 1 <= COND_ROWS

    h_parts = (x.reshape(B * L, D), ctx.reshape(B * Lc, D))
    cond =jnp.concatenate([c, c_ctx[None], jnp.zeros((COND_ROWS - B - 1, D), F32)], axis=0)
    mod = _adaln_call(cond, w_mod, b_mod)
    lb_soft = jax.nn.softmax(hg_lb.astype(F32), axis=1)
    lower = jnp.cumsum(lb_soft, axis=1) - lb_soft[:, :1]
    rw_pad = jnp.pad(router_w, ((0, 0), (0, 0), (0, HEAD_DIM - N_EXPERTS)))
    rb_pad = jnp.pad(router_b, ((0, 0), (0, HEAD_DIM - N_EXPERTS)))

    T = B * (L + Lc)
    n_row_tiles = 8
    a_in = _ln_mod_call(h_parts, mod[0], B, L)
    for l in range(depth):
        last = l == depth - 1
        px = _wstat_matmul_call([a_in], w_in, l, T, T // n_row_tiles, 512, "in_proj")
        a_mix = _pool_call(px, pool_w[l].astype(BF16), pool_scale[l], B, L, Lc)
        b_x, b_c = _na_call(px, na_rpb[l], B, L, Lc, c_naq, c_nak, c_nav, na_heads)
        b_mix = jnp.concatenate([b_x, b_c], axis=0)
        o_f, o_b = _hgrn_call(px, lower[0, l].reshape(hg_heads, 1, HEAD_DIM),
                              lower[1, l].reshape(hg_heads, 1, HEAD_DIM), B, L, Lc,
                              c_hq, c_hff, c_hfb, c_hi, hg_heads)
        c_mix = _readout_call(o_f, o_b, px, hg_norm_g[l], c_hg, T // n_row_tiles)
        n_out = B * L if last else T
        y = _wstat_matmul_call([a_mix, b_mix, c_mix], w_out, l, n_out, n_out // n_row_tiles, 512, "out_proj")
        h1, fin, gates, topi = _post_mix_call(y, h_parts, mod[l], ln1_g[l], ln1_b[l], rw_pad[l],
                                              rb_pad[l].reshape(1, -1), B, L, alpha)
        sorted_tok, sorted_w, pos_kmajor, visits = _moe_dispatch(topi[:, :TOP_K], gates, MOE_TILE)
        xs = fin.at[sorted_tok].get(mode="promise_in_bounds")
        ys = _moe_call(xs, sorted_w, visits, exp_w1, exp_b1, exp_w2, exp_b2, l)
        y_assign = ys.at[pos_kmajor].get(mode="promise_in_bounds")
        if last:
            hall = _final_call(h1, y_assign, mod[l], ln2_g[l], ln2_b[l], B, L, alpha)
        else:
            hall, a_in = _final_call(h1, y_assign, mod[l], ln2_g[l], ln2_b[l], B, L, alpha, mod_next=mod[l + 1])
        h_parts = (hall,)
    return hall.reshape(B, L, D)
```

```python
import functools

import jax
import jax.numpy as jnp
import numpy as np
from jax import lax
from jax.experimental import pallas as pl
from jax.experimental.pallas import tpu as pltpu

F32 = jnp.float32
BF16 = jnp.bfloat16

HEAD_DIM = 128
GRID_W = 64
POOL_WINDOWS = (2, 4, 8, 16)
NA_ROWS = 8
NA_COLS = 16
FORGET_EPS = 1e-20
N_EXPERTS = 32
TOP_K = 4
SWIGLU_LIMIT = 7.0
SWIGLU_ALPHA = 1.702
ROPE_BASE = 10000.0
LN_EPS = 1e-6

VMEM_LIMIT_BYTES = 56 * 1024 * 1024
HG_CHUNK = 256
NEG_BIG = -1e30


def _params(sem):
    return pltpu.CompilerParams(dimension_semantics=sem, vmem_limit_bytes=VMEM_LIMIT_BYTES)


def _dot(a, b):
    return jnp.dot(a, b, preferred_element_type=F32)


def _dot_nt(a, b):
    return lax.dot_general(a, b, (((1,), (1,)), ((), ())), preferred_element_type=F32)


def _sigmoid(x):
    return 1.0 / (1.0 + jnp.exp(-x))


LOG2E = 1.4426950408889634
HG_HALF = HG_CHUNK // 2
HG_HEADS_PER_STEP = 4


def _neg_abs(x):
    bits = lax.bitcast_convert_type(x, jnp.uint32) | jnp.uint32(0x80000000)
    return lax.bitcast_convert_type(bits, F32)


def _hgrn_half_scores(q, k, b2, g2, fc, lev, reverse):
    n = q.shape[0]
    row = lax.broadcasted_iota(jnp.int32, (n, HEAD_DIM), 0)
    a = jnp.where(lev == -1, _dot_nt(q.astype(BF16), k.astype(BF16)), 0.0)
    h, lvl = 1, 0
    while h < n:
        if h < 8:
            upper = (row & h) != 0
            t_role = jnp.logical_not(upper) if reverse else upper
            if h == 1:
                x = jnp.where(t_role, q * fc, k)
            elif h == 2:
                r4 = row & 3
                g_prev, g_next = pltpu.roll(g2, 1, 0), pltpu.roll(g2, n - 1, 0)
                if reverse:
                    e = jnp.where(r4 == 0, g2 + g_next, jnp.where(r4 == 1, g2, jnp.where(r4 == 2, 0.0, g_prev)))
                else:
                    e = jnp.where(r4 == 0, g_next, jnp.where(r4 == 1, 0.0, jnp.where(r4 == 2, g2, g2 + g_prev)))
                x = jnp.where(t_role, q, k) * jnp.exp2(e)
            else:
                m0 = h if reverse else h - 1
                b3 = b2.reshape(n // (2 * h), 2 * h, HEAD_DIM)
                ref = jnp.broadcast_to(b3[:, m0:m0 + 1, :], b3.shape).reshape(n, HEAD_DIM)
                x = jnp.where(t_role, q, k) * jnp.exp2(_neg_abs(b2 - ref))
            x = x.astype(BF16)
            a = jnp.where(lev == lvl, _dot_nt(x, x), a)
        else:
            nb = n // (2 * h)
            q4 = q.reshape(nb, 2, h, HEAD_DIM)
            k4 = k.reshape(nb, 2, h, HEAD_DIM)
            b4 = b2.reshape(nb, 2, h, HEAD_DIM)
            s_half, t_half, m_row = (1, 0, 0) if reverse else (0, 1, h - 1)
            ref = b4[:, s_half:s_half + 1, m_row:m_row + 1, :]
            xt = q4[:, t_half] * jnp.exp2(b4[:, t_half] - ref[:, 0])
            xs = k4[:, s_half] * jnp.exp2(ref[:, 0] - b4[:, s_half])
            halves = [xt, xs] if reverse else [xs, xt]
            x = jnp.stack(halves, axis=1).reshape(n, HEAD_DIM).astype(BF16)
            g = _dot_nt(xt.reshape(n // 2, HEAD_DIM).astype(BF16), x)
            a4 = a.reshape(nb, 2, h, n)
            lev_t = lev.reshape(nb, 2, h, n)[:, t_half]
            a_t = jnp.where(lev_t == lvl, g.reshape(nb, h, n), a4[:, t_half])
            halves = [a_t, a4[:, s_half]] if reverse else [a4[:, s_half], a_t]
            a = jnp.stack(halves, axis=1).reshape(n, n)
        h *= 2
        lvl += 1
    return a


def _hgrn_direction(q, z, v_bf, lb, st_ref, lev, tri_bf, reverse):
    C, H = HG_CHUNK, HG_HALF
    q = q * _sigmoid(q)
    sig = _sigmoid(z)
    fc = jnp.maximum(lb + (1.0 - lb) * sig, FORGET_EPS)
    g2 = jnp.log(fc) * LOG2E
    k = (1.0 - lb) * (1.0 - sig)

    g_hi = g2.astype(BF16)
    r1 = g2 - g_hi.astype(F32)
    g_mid = r1.astype(BF16)
    g_lo = (r1 - g_mid.astype(F32)).astype(BF16)
    b2 = _dot(tri_bf, g_hi) + _dot(tri_bf, g_mid) + _dot(tri_bf, g_lo)

    lo, hi = slice(0, H), slice(H, C)
    a_lo = _hgrn_half_scores(q[lo], k[lo], b2[lo], g2[lo], fc[lo], lev, reverse).astype(BF16)
    a_hi = _hgrn_half_scores(q[hi], k[hi], b2[hi], g2[hi], fc[hi], lev, reverse).astype(BF16)
    t_sl, s_sl, m = (lo, hi, H) if reverse else (hi, lo, H - 1)
    ref = b2[m:m + 1, :]
    qt = (q[t_sl] * jnp.exp2(b2[t_sl] - ref)).astype(BF16)
    ks = (k[s_sl] * jnp.exp2(ref - b2[s_sl])).astype(BF16)
    cross = _dot(_dot_nt(qt, ks).astype(BF16), v_bf[s_sl])
    o_lo = _dot(a_lo, v_bf[lo])
    o_hi = _dot(a_hi, v_bf[hi])
    o_intra = jnp.concatenate([o_lo + cross, o_hi] if reverse else [o_lo, o_hi + cross], axis=0)

    st = st_ref[...]
    tot = b2[0:1, :] if reverse else b2[C - 1:C, :]
    qdec = (q * jnp.exp2(b2)).astype(BF16)
    o_inter = _dot_nt(qdec, st.astype(BF16))
    kdec = (k * jnp.exp2(tot - b2)).astype(BF16)
    vt = v_bf.astype(F32).T.astype(BF16)
    st_ref[...] = jnp.exp2(tot) * st + _dot(vt, kdec)
    return o_intra + o_inter


def _hgrn_kernel(qf_ref, zf_ref, vf_ref, qb_ref, zb_ref, vb_ref, lbf_ref, lbb_ref, levf_ref, levb_ref,
                 trif_ref, trib_ref, of_ref, ob_ref, sf_ref, sb_ref):
    @pl.when(pl.program_id(2) == 0)
    def _():
        sf_ref[...] = jnp.zeros_like(sf_ref)
        sb_ref[...] = jnp.zeros_like(sb_ref)

    for hh in range(HG_HEADS_PER_STEP):
        sl = slice(hh * HEAD_DIM, (hh + 1) * HEAD_DIM)
        of_ref[:, sl] = _hgrn_direction(qf_ref[:, sl].astype(F32), zf_ref[:, sl].astype(F32), vf_ref[:, sl],
                                        lbf_ref[0][:, sl], sf_ref.at[hh], levf_ref[...], trif_ref[...],
                                        False).astype(of_ref.dtype)
        ob_ref[:, sl] = _hgrn_direction(qb_ref[:, sl].astype(F32), zb_ref[:, sl].astype(F32), vb_ref[:, sl],
                                        lbb_ref[0][:, sl], sb_ref.at[hh], levb_ref[...], trib_ref[...],
                                        True).astype(ob_ref.dtype)


def _hgrn_call(px, lb_f, lb_b, B, L, Lc, col_q, col_ff, col_fb, col_i, n_heads):
    C, H = HG_CHUNK, HG_HALF
    T = px.shape[0]
    nlc, ncc = L // C, Lc // C
    ns = nlc + ncc
    ctx0 = B * nlc

    def fwd_row(b, s):
        return jnp.where(s < ncc, ctx0 + b * ncc + s, b * nlc + s - ncc)

    def bwd_row(b, s):
        return jnp.where(s < ncc, ctx0 + b * ncc + (ncc - 1 - s), b * nlc + (nlc - 1 - (s - ncc)))

    G = HG_HEADS_PER_STEP
    GW = G * HEAD_DIM
    assert n_heads % G == 0 and all(c % G == 0 for c in (col_q, col_ff, col_fb, col_i))

    def spec(row_fn, col):
        return pl.BlockSpec((C, GW), lambda b, h, s: (row_fn(b, s), col // G + h))

    idx = np.arange(H)
    xr = idx[:, None] ^ idx[None, :]
    level = np.floor(np.log2(np.maximum(xr, 1))).astype(np.int32)
    lev_f = np.where(xr == 0, -1, np.where(idx[:, None] > idx[None, :], level, -2)).astype(np.int32)
    lev_b = np.ascontiguousarray(lev_f.T)
    idc = np.arange(C)
    tri_f = (idc[:, None] >= idc[None, :]).astype(np.float32)
    tri_b = (idc[:, None] <= idc[None, :]).astype(np.float32)
    const = lambda shape: pl.BlockSpec(shape, lambda b, h, s: (0,) * len(shape))
    lb_spec = pl.BlockSpec((1, 1, GW), lambda b, h, s: (h, 0, 0))
    out_sds = jax.ShapeDtypeStruct((T, n_heads * HEAD_DIM), BF16)
    state = pltpu.VMEM((G, HEAD_DIM, HEAD_DIM), F32)
    return pl.pallas_call(
        _hgrn_kernel,
        grid=(B, n_heads // G, ns),
        in_specs=[spec(fwd_row, col_q), spec(fwd_row, col_ff), spec(fwd_row, col_i),
                  spec(bwd_row, col_q), spec(bwd_row, col_fb), spec(bwd_row, col_i),
                  lb_spec, lb_spec, const((H, H)), const((H, H)), const((C, C)), const((C, C))],
        out_specs=[pl.BlockSpec((C, GW), lambda b, h, s: (fwd_row(b, s), h)),
                   pl.BlockSpec((C, GW), lambda b, h, s: (bwd_row(b, s), h))],
        out_shape=[out_sds, out_sds],
        scratch_shapes=[state, state],
        compiler_params=_params(("parallel", "parallel", "arbitrary")),
        name="hgrn_scan",
    )(px, px, px, px, px, px, lb_f.reshape(n_heads // G, 1, GW), lb_b.reshape(n_heads // G, 1, GW),
      jnp.asarray(lev_f), jnp.asarray(lev_b), jnp.asarray(tri_f, BF16), jnp.asarray(tri_b, BF16))


NA_QROWS = 4
NA_KROWS = NA_QROWS + NA_ROWS


def _na_block_cases(rows):
    assert rows % NA_QROWS == 0 and rows >= 2 * NA_ROWS
    nb = rows // NA_QROWS
    cases = []
    for i in (0, 1, nb - 1):
        r0 = NA_QROWS * i
        ks = min(max(r0 - NA_ROWS // 2, 0), rows - NA_KROWS)
        per_q = []
        for qr in range(NA_QROWS):
            r = r0 + qr
            rs = min(max(r - NA_ROWS // 2, 0), rows - NA_ROWS)
            per_q.append((rs - ks, ks - r + NA_ROWS - 1))
        cases.append(per_q)
    return cases


def _na_kernel(rpb_ref, q_ref, k_ref, v_ref, qc_ref, kc_ref, vc_ref, cos_ref, sin_ref,
               ox_ref, oc_ref, qr_ref, kr_ref, tb_ref, bias_ref, *, rows):
    h = pl.program_id(0)
    L = q_ref.shape[0]
    W = GRID_W
    scale = HEAD_DIM ** -0.5
    n_dr = 2 * NA_ROWS - 1
    n_dc = 2 * NA_COLS - 1

    @pl.when(pl.program_id(1) == 0)
    def _():
        c_i = lax.broadcasted_iota(jnp.int32, (W, W), 0)
        kc_i = lax.broadcasted_iota(jnp.int32, (W, W), 1)
        cs = jnp.clip(c_i - NA_COLS // 2, 0, W - NA_COLS)
        in_win = (kc_i >= cs) & (kc_i < cs + NA_COLS)
        c_off = kc_i - c_i + (NA_COLS - 1)
        for dr in range(n_dr):
            acc = jnp.zeros((W, W), F32)
            for j in range(n_dc):
                acc = jnp.where(c_off == j, rpb_ref[h, dr, j], acc)
            tb_ref[dr] = jnp.where(in_win, acc, NEG_BIG)
        neg = jnp.full((W, W), NEG_BIG, F32)
        for ci, per_q in enumerate(_na_block_cases(rows)):
            for qr, (j0, dr0) in enumerate(per_q):
                for j in range(NA_KROWS):
                    valid = j0 <= j < j0 + NA_ROWS
                    blk = tb_ref[dr0 + j] if valid else neg
                    bias_ref[ci, qr * W:(qr + 1) * W, j * W:(j + 1) * W] = blk

    RC = 256
    lane = lax.broadcasted_iota(jnp.int32, (RC, HEAD_DIM), 1)
    first = (lane & (HEAD_DIM // 2 - 1)) < HEAD_DIM // 4

    def rope_chunk(i, carry):
        sl = pl.ds(pl.multiple_of(i * RC, RC), RC)
        cos = cos_ref[sl, :]
        sin = sin_ref[sl, :]
        for src, dst, mul in ((q_ref, qr_ref, scale), (k_ref, kr_ref, 1.0)):
            t = src[sl, :].astype(F32)
            partner = jnp.where(first, pltpu.roll(t, HEAD_DIM - HEAD_DIM // 4, 1), pltpu.roll(t, HEAD_DIM // 4, 1))
            dst[sl, :] = ((t * cos + partner * sin) * mul).astype(BF16)
        return carry

    lax.fori_loop(0, L // RC, rope_chunk, 0)

    kc = kc_ref[...]
    vc = vc_ref[...]
    nb = rows // NA_QROWS
    QB = NA_QROWS * W
    KB = NA_KROWS * W

    def block(i, carry):
        ks = jnp.clip(i * NA_QROWS - NA_ROWS // 2, 0, rows - NA_KROWS)
        case = jnp.where(i == 0, 0, jnp.where(i == nb - 1, 2, 1))
        qs = pl.ds(pl.multiple_of(i * QB, QB), QB)
        kslice = pl.ds(pl.multiple_of(ks * W, W), KB)
        qb = qr_ref[qs, :]
        s_loc = _dot_nt(qb, kr_ref[kslice, :]) + bias_ref[case]
        s_ctx = _dot_nt(qb, kc)
        m = jnp.maximum(jnp.max(s_loc, -1, keepdims=True), jnp.max(s_ctx, -1, keepdims=True))
        p_loc = jnp.exp(s_loc - m)
        p_ctx = jnp.exp(s_ctx - m)
        den = jnp.sum(p_loc, -1, keepdims=True) + jnp.sum(p_ctx, -1, keepdims=True)
        o = _dot(p_loc.astype(BF16), v_ref[kslice, :]) + _dot(p_ctx.astype(BF16), vc)
        ox_ref[qs, :] = (o / den).astype(ox_ref.dtype)
        return carry

    lax.fori_loop(0, nb, block, 0, unroll=4)

    qc = (qc_ref[...].astype(F32) * scale).astype(BF16)
    s = _dot_nt(qc, kc)
    p = jnp.exp(s - jnp.max(s, -1, keepdims=True))
    oc = _dot(p.astype(BF16), vc) / jnp.sum(p, -1, keepdims=True)
    oc_ref[...] = oc.astype(oc_ref.dtype)


def _rope_tables(L):
    pos = jnp.arange(L)
    half = HEAD_DIM // 2
    inv = ROPE_BASE ** (-jnp.arange(0, half, 2, dtype=F32) / half)

    def tab(p):
        ang = p.astype(F32)[:, None] * inv[None]
        cos, sin = jnp.cos(ang), jnp.sin(ang)
        return jnp.concatenate([cos, cos], -1), jnp.concatenate([-sin, sin], -1)

    c_r, s_r = tab(pos // GRID_W)
    c_c, s_c = tab(pos % GRID_W)
    return jnp.concatenate([c_r, c_c], -1), jnp.concatenate([s_r, s_c], -1)


def _na_call(px, rpb, B, L, Lc, col_q, col_k, col_v, n_heads):
    rows = L // GRID_W
    cos, sin = _rope_tables(L)
    ctx0 = B * L // Lc

    def lat(col):
        return pl.BlockSpec((L, HEAD_DIM), lambda h, b: (b, col + h))

    def cx(col):
        return pl.BlockSpec((Lc, HEAD_DIM), lambda h, b: (ctx0 + b, col + h))

    tab = pl.BlockSpec((L, HEAD_DIM), lambda h, b: (0, 0))
    W = GRID_W
    return pl.pallas_call(
        functools.partial(_na_kernel, rows=rows),
        grid=(n_heads, B),
        in_specs=[pl.BlockSpec(memory_space=pltpu.SMEM), lat(col_q), lat(col_k), lat(col_v),
                  cx(col_q), cx(col_k), cx(col_v), tab, tab],
        out_specs=[pl.BlockSpec((L, HEAD_DIM), lambda h, b: (b, h)),
                   pl.BlockSpec((Lc, HEAD_DIM), lambda h, b: (b, h))],
        out_shape=[jax.ShapeDtypeStruct((B * L, n_heads * HEAD_DIM), BF16),
                   jax.ShapeDtypeStruct((B * Lc, n_heads * HEAD_DIM), BF16)],
        scratch_shapes=[pltpu.VMEM((L, HEAD_DIM), BF16), pltpu.VMEM((L, HEAD_DIM), BF16),
                        pltpu.VMEM((2 * NA_ROWS - 1, W, W), F32),
                        pltpu.VMEM((3, NA_QROWS * W, NA_KROWS * W), F32)],
        compiler_params=_params(("parallel", "arbitrary")),
        name="neighborhood_attention",
    )(rpb, px, px, px, px, px, px, cos, sin)


POOL_TILE = 256
POOL_HALO = 16


def _pool_kernel(prev_ref, cur_ref, next_ref, w_ref, scale_ref, o_ref, *, n_lat_tiles, nl, nc):
    i = pl.program_id(0)
    is_lat = i < n_lat_tiles
    j = jnp.where(is_lat, i % nl, (i - n_lat_tiles) % nc)
    n = jnp.where(is_lat, nl, nc)
    TQ, HL = POOL_TILE, POOL_HALO
    lo_min = jnp.where(j > 0, -HL, 0)
    hi_max = jnp.where(j < n - 1, TQ + HL, TQ)
    src = jnp.concatenate([prev_ref[...], cur_ref[...], next_ref[...]], axis=0)
    t = lax.broadcasted_iota(jnp.int32, (TQ, TQ + 2 * HL), 0)
    s = lax.broadcasted_iota(jnp.int32, (TQ, TQ + 2 * HL), 1) - HL
    Dg = w_ref.shape[1]
    t1 = lax.broadcasted_iota(jnp.int32, (TQ, Dg), 0)
    for g, w in enumerate(POOL_WINDOWS):
        lo = jnp.maximum(t - w // 2, lo_min)
        hi = jnp.minimum(t + (w - w // 2), hi_max)
        band = jnp.where(s >= lo, jnp.where(s < hi, 1.0, 0.0), 0.0).astype(BF16)
        cnt = (jnp.minimum(t1 + (w - w // 2), hi_max) - jnp.maximum(t1 - w // 2, lo_min)).astype(F32)
        cols = slice(g * Dg, (g + 1) * Dg)
        win = _dot(band, src[:, cols])
        d = win / cnt - cur_ref[:, cols].astype(F32)
        y = _dot(d.astype(BF16), w_ref[g]) * scale_ref[:, cols]
        o_ref[:, cols] = y.astype(o_ref.dtype)


def _pool_call(px, pool_w_bf, pool_scale, B, L, Lc):
    T = px.shape[0]
    G, Dg, _ = pool_w_bf.shape
    PW = G * Dg
    TQ, HL = POOL_TILE, POOL_HALO
    r = TQ // HL
    n_tiles = T // TQ
    n_halo = T // HL
    kern = functools.partial(_pool_kernel, n_lat_tiles=B * L // TQ, nl=L // TQ, nc=Lc // TQ)
    return pl.pallas_call(
        kern,
        grid=(n_tiles,),
        in_specs=[pl.BlockSpec((HL, PW), lambda i: (jnp.maximum(i * r - 1, 0), 0)),
                  pl.BlockSpec((TQ, PW), lambda i: (i, 0)),
                  pl.BlockSpec((HL, PW), lambda i: (jnp.minimum((i + 1) * r, n_halo - 1), 0)),
                  pl.BlockSpec((G, Dg, Dg), lambda i: (0, 0, 0)),
                  pl.BlockSpec((1, PW), lambda i: (0, 0))],
        out_specs=pl.BlockSpec((TQ, PW), lambda i: (i, 0)),
        out_shape=jax.ShapeDtypeStruct((T, PW), BF16),
        compiler_params=_params(("parallel",)),
        name="multiscale_pool",
    )(px, px, px, pool_w_bf, pool_scale.reshape(1, PW).astype(F32))


READOUT_HEADS = 4


def _readout_kernel(of_ref, ob_ref, gate_ref, ng_ref, o_ref):
    for hh in range(READOUT_HEADS):
        sl = slice(hh * HEAD_DIM, (hh + 1) * HEAD_DIM)
        o = of_ref[:, sl].astype(F32) + ob_ref[:, sl].astype(F32)
        o = o * lax.rsqrt(jnp.mean(o * o, -1, keepdims=True) + LN_EPS)
        g = gate_ref[:, sl].astype(F32)
        o_ref[:, sl] = (o * ng_ref[:, sl] * (g * _sigmoid(g))).astype(o_ref.dtype)


def _readout_call(o_f, o_b, px, norm_g, col_gate, tm):
    T, HW = o_f.shape
    GW = READOUT_HEADS * HEAD_DIM
    assert HW % GW == 0 and col_gate % READOUT_HEADS == 0
    blk = lambda col: pl.BlockSpec((tm, GW), lambda i, h: (i, col // READOUT_HEADS + h))
    return pl.pallas_call(
        _readout_kernel,
        grid=(T // tm, HW // GW),
        in_specs=[blk(0), blk(0), blk(col_gate), pl.BlockSpec((1, GW), lambda i, h: (0, h))],
        out_specs=blk(0),
        out_shape=jax.ShapeDtypeStruct((T, HW), BF16),
        compiler_params=_params(("parallel", "parallel")),
        name="hgrn_readout",
    )(o_f, o_b, px, norm_g.reshape(1, HW).astype(F32))


COND_ROWS = 16


def _adaln_kernel(cond_ref, w_ref, b_ref, o_ref):
    cnd = cond_ref[...]
    a = (cnd * _sigmoid(cnd)).astype(BF16)
    o_ref[0] = _dot(a, w_ref[0].astype(BF16)) + b_ref[0]


def _adaln_call(cond, w_mod, b_mod, tn=1024):
    depth, D, N = w_mod.shape
    return pl.pallas_call(
        _adaln_kernel,
        grid=(depth, N // tn),
        in_specs=[pl.BlockSpec((COND_ROWS, D), lambda l, j: (0, 0)),
                  pl.BlockSpec((1, D, tn), lambda l, j: (l, 0, j)),
                  pl.BlockSpec((1, 1, tn), lambda l, j: (l, 0, j))],
        out_specs=pl.BlockSpec((1, COND_ROWS, tn), lambda l, j: (l, 0, j)),
        out_shape=jax.ShapeDtypeStruct((depth, COND_ROWS, N), F32),
        compiler_params=_params(("parallel", "parallel")),
        name="adaln",
    )(cond, w_mod, b_mod.reshape(depth, 1, N))


def _cond_row(i, tm, n_lat_rows, L, B):
    return jnp.where(i * tm < n_lat_rows, (i * tm) // L, B)


def _layer_norm_rows(x):
    mu = jnp.mean(x, -1, keepdims=True)
    xc = x - mu
    var = jnp.mean(xc * xc, -1, keepdims=True)
    return xc * lax.rsqrt(var + LN_EPS)


def _row_part_specs(parts, tm):
    specs, firsts, lo = [], [], 0
    for part in parts:
        n_t = part.shape[0] // tm
        assert part.shape[0] % tm == 0
        specs.append(pl.BlockSpec((tm, part.shape[1]),
                                  functools.partial(lambda i, lo, n_t: (jnp.clip(i - lo, 0, n_t - 1), 0), lo=lo, n_t=n_t)))
        firsts.append(lo)
        lo += n_t
    return specs, tuple(firsts)


def _read_row_parts(refs, firsts, i):
    val = refs[0][...]
    for ref, lo in zip(refs[1:], firsts[1:]):
        val = jnp.where(i >= lo, ref[...], val)
    return val


def _ln_mod_kernel(*refs, firsts, tm, n_lat_rows, L, B, D):
    n = len(firsts)
    mod_ref, o_ref = refs[n], refs[n + 1]
    i = pl.program_id(0)
    r = _cond_row(i, tm, n_lat_rows, L, B)
    shift = mod_ref[pl.ds(r, 1), 0:D]
    scale = mod_ref[pl.ds(r, 1), D:2 * D]
    x = _read_row_parts(refs[:n], firsts, i)
    o_ref[...] = (_layer_norm_rows(x) * (1.0 + scale) + shift).astype(o_ref.dtype)


def _ln_mod_call(h_parts, mod_l, B, L, tm=256):
    T = sum(p.shape[0] for p in h_parts)
    D = h_parts[0].shape[1]
    specs, firsts = _row_part_specs(h_parts, tm)
    kern = functools.partial(_ln_mod_kernel, firsts=firsts, tm=tm, n_lat_rows=B * L, L=L, B=B, D=D)
    return pl.pallas_call(
        kern,
        grid=(T // tm,),
        in_specs=specs + [pl.BlockSpec(mod_l.shape, lambda i: (0, 0))],
        out_specs=pl.BlockSpec((tm, D), lambda i: (i, 0)),
        out_shape=jax.ShapeDtypeStruct((T, D), BF16),
        compiler_params=_params(("parallel",)),
        name="ln_modulate",
    )(*h_parts, mod_l)


def _wstat_matmul_kernel(*refs, widths):
    n = len(widths)
    a_refs, w_ref, o_ref, wbf_ref = refs[:n], refs[n], refs[n + 1], refs[n + 2]

    @pl.when(pl.program_id(1) == 0)
    def _():
        wbf_ref[...] = w_ref[0].astype(BF16)

    acc = None
    k0 = 0
    for a_ref, kw in zip(a_refs, widths):
        part = _dot(a_ref[...], wbf_ref[k0:k0 + kw, :])
        acc = part if acc is None else acc + part
        k0 += kw
    o_ref[...] = acc.astype(o_ref.dtype)


def _wstat_matmul_call(a_list, w_all, layer, n_rows, tm, tn, name):
    widths = tuple(a.shape[1] for a in a_list)
    _, K, N = w_all.shape
    assert sum(widths) == K and n_rows % tm == 0 and N % tn == 0
    return pl.pallas_call(
        functools.partial(_wstat_matmul_kernel, widths=widths),
        grid=(N // tn, n_rows // tm),
        in_specs=[pl.BlockSpec((tm, kw), lambda j, i: (i, 0)) for kw in widths]
        + [pl.BlockSpec((1, K, tn), lambda j, i: (layer, 0, j))],
        out_specs=pl.BlockSpec((tm, tn), lambda j, i: (i, j)),
        out_shape=jax.ShapeDtypeStruct((n_rows, N), BF16),
        scratch_shapes=[pltpu.VMEM((K, tn), BF16)],
        compiler_params=_params(("parallel", "arbitrary")),
        name=name,
    )(*a_list, w_all)


def _split_bf16(x):
    hi = x.astype(BF16)
    return hi, (x - hi.astype(F32)).astype(BF16)


def _post_mix_kernel(y_ref, *refs, firsts, tm, n_lat_rows, L, B, D, alpha):
    n = len(firsts)
    mod_ref, g_ref, b_ref, rw_ref, rb_ref, h1_ref, fin_ref, gates_ref, topi_ref = refs[n:]
    i = pl.program_id(0)
    r = _cond_row(i, tm, n_lat_rows, L, B)
    mrow = lambda slot: mod_ref[pl.ds(r, 1), slot * D:(slot + 1) * D]
    u = alpha * _read_row_parts(refs[:n], firsts, i) + mrow(2) * y_ref[...].astype(F32)
    h1 = _layer_norm_rows(u) * g_ref[...] + b_ref[...]
    h1_ref[...] = h1
    fin = _layer_norm_rows(h1) * (1.0 + mrow(4)) + mrow(3)
    fin_ref[...] = fin.astype(BF16)
    f_hi, f_lo = _split_bf16(fin)
    w_hi, w_lo = _split_bf16(rw_ref[...])
    logits = _dot(f_hi, w_hi) + _dot(f_hi, w_lo) + _dot(f_lo, w_hi) + rb_ref[...]
    lane = lax.broadcasted_iota(jnp.int32, logits.shape, 1)
    ninf = jnp.float32(-jnp.inf)
    xs = jnp.where(lane < N_EXPERTS, logits, ninf)
    sel_v, sel_i = [], []
    for _ in range(TOP_K):
        m = jnp.max(xs, -1, keepdims=True)
        idx = jnp.min(jnp.where(xs == m, lane, logits.shape[1]), -1, keepdims=True)
        sel_v.append(m)
        sel_i.append(idx)
        xs = jnp.where(lane == idx, ninf, xs)
    ex = [jnp.exp(v - sel_v[0]) for v in sel_v]
    den = ex[0]
    for e in ex[1:]:
        den = den + e
    topw = jnp.zeros(logits.shape, F32)
    topi = jnp.zeros(logits.shape, jnp.int32)
    for j in range(TOP_K):
        topw = jnp.where(lane == j, ex[j] / den, topw)
        topi = jnp.where(lane == j, sel_i[j], topi)
    gates_ref[...] = topw
    topi_ref[...] = topi


def _post_mix_call(y, h_parts, mod_l, ln_g, ln_b, rw_pad, rb_pad, B, L, alpha, tm=256):
    n_out, D = y.shape
    NE = rw_pad.shape[1]
    h_specs, firsts = _row_part_specs(h_parts, tm)
    kern = functools.partial(_post_mix_kernel, firsts=firsts, tm=tm, n_lat_rows=B * L, L=L, B=B, D=D, alpha=alpha)
    row = lambda i: (i, 0)
    fixed = lambda i: (0, 0)
    return pl.pallas_call(
        kern,
        grid=(n_out // tm,),
        in_specs=[pl.BlockSpec((tm, D), row)] + h_specs
        + [pl.BlockSpec(mod_l.shape, fixed),
                  pl.BlockSpec((1, D), fixed), pl.BlockSpec((1, D), fixed),
                  pl.BlockSpec((D, NE), fixed), pl.BlockSpec((1, NE), fixed)],
        out_specs=[pl.BlockSpec((tm, D), row), pl.BlockSpec((tm, D), row),
                   pl.BlockSpec((tm, NE), row), pl.BlockSpec((tm, NE), row)],
        out_shape=[jax.ShapeDtypeStruct((n_out, D), F32), jax.ShapeDtypeStruct((n_out, D), BF16),
                   jax.ShapeDtypeStruct((n_out, NE), F32), jax.ShapeDtypeStruct((n_out, NE), jnp.int32)],
        compiler_params=_params(("parallel",)),
        name="post_mix_router",
    )(y, *h_parts, mod_l, ln_g.reshape(1, D), ln_b.reshape(1, D), rw_pad, rb_pad)


MOE_TILE = 512
MOE_ALIGN = 16


def _first_visit_of_expert(v, ve_ref):
    return (v == 0) | (ve_ref[jnp.maximum(v - 1, 0)] != ve_ref[v])


VISIT_SKIP, VISIT_FULL, VISIT_HALF = 0, 1, 2


def _run_visit(kind, run, tm):
    @pl.when(kind == VISIT_FULL)
    def _():
        run(slice(0, tm))

    @pl.when(kind == VISIT_HALF)
    def _():
        run(slice(0, tm // 2))


def _moe_up_kernel(start_ref, ve_ref, vout_ref, vv_ref, x_ref, w1_ref, b1_ref, o_ref, wbf_ref, *, DE):
    v = pl.program_id(0)

    @pl.when(_first_visit_of_expert(v, ve_ref))
    def _():
        wbf_ref[...] = w1_ref[0, 0].astype(BF16)

    def run(rows):
        hcat = _dot(x_ref[rows, :], wbf_ref[...]) + b1_ref[0, 0]
        gate = jnp.minimum(hcat[:, :DE], SWIGLU_LIMIT)
        up = jnp.clip(hcat[:, DE:], -SWIGLU_LIMIT, SWIGLU_LIMIT)
        o_ref[rows, :] = ((up + 1.0) * (gate * _sigmoid(SWIGLU_ALPHA * gate))).astype(o_ref.dtype)

    _run_visit(vv_ref[v], run, x_ref.shape[0])


def _moe_down_kernel(start_ref, ve_ref, vout_ref, vv_ref, h_ref, w2_ref, b2_ref, o_ref, wbf_ref):
    v = pl.program_id(0)

    @pl.when(_first_visit_of_expert(v, ve_ref))
    def _():
        wbf_ref[...] = w2_ref[0, 0].astype(BF16)

    def run(rows):
        o_ref[rows, :] = (_dot(h_ref[rows, :], wbf_ref[...]) + b2_ref[0, 0]).astype(o_ref.dtype)

    _run_visit(vv_ref[v], run, h_ref.shape[0])


def _moe_call(xs, visits, w1_all, b1_all, w2_all, b2_all, layer):
    n_in, D = xs.shape
    _, E, _, DE2 = w1_all.shape
    DE = DE2 // 2
    tm = MOE_TILE
    nv = visits[0].shape[0]
    window = lambda v, st, ve, vo, vv: (pl.multiple_of(st[v], MOE_ALIGN), 0)
    out_tile = lambda v, st, ve, vo, vv: (vo[v], 0)
    expert = lambda v, st, ve, vo, vv: (layer, ve[v], 0, 0)
    hdn = pl.pallas_call(
        functools.partial(_moe_up_kernel, DE=DE),
        grid_spec=pltpu.PrefetchScalarGridSpec(
            num_scalar_prefetch=4, grid=(nv,),
            in_specs=[pl.BlockSpec((pl.Element(tm), pl.Element(D)), window),
                      pl.BlockSpec((1, 1, D, DE2), expert),
                      pl.BlockSpec((1, 1, 1, DE2), expert)],
            out_specs=pl.BlockSpec((tm, DE), out_tile),
            scratch_shapes=[pltpu.VMEM((D, DE2), BF16)]),
        out_shape=jax.ShapeDtypeStruct((nv * tm, DE), BF16),
        compiler_params=_params(("arbitrary",)),
        name="moe_up",
    )(*visits, xs, w1_all, b1_all.reshape(b1_all.shape[0], E, 1, DE2))
    return pl.pallas_call(
        _moe_down_kernel,
        grid_spec=pltpu.PrefetchScalarGridSpec(
            num_scalar_prefetch=4, grid=(nv,),
            in_specs=[pl.BlockSpec((tm, DE), out_tile),
                      pl.BlockSpec((1, 1, DE, D), expert),
                      pl.BlockSpec((1, 1, 1, D), expert)],
            out_specs=pl.BlockSpec((tm, D), out_tile),
            scratch_shapes=[pltpu.VMEM((DE, D), BF16)]),
        out_shape=jax.ShapeDtypeStruct((nv * tm, D), BF16),
        compiler_params=_params(("arbitrary",)),
        name="moe_down",
    )(*visits, hdn, w2_all, b2_all.reshape(b2_all.shape[0], E, 1, D))


def _moe_dispatch(topi, tm):
    T = topi.shape[0]
    E = N_EXPERTS
    n = T * TOP_K
    al = MOE_ALIGN
    flat_e = topi.reshape(-1)
    order = jnp.argsort(flat_e, stable=True).astype(jnp.int32)
    inv = jnp.argsort(order).astype(jnp.int32)
    e_col = jnp.arange(E, dtype=jnp.int32)[:, None]
    is_e = flat_e[None, :] == e_col
    counts = jnp.sum(is_e.astype(jnp.int32), axis=1)
    ends = jnp.cumsum(counts)
    offs = ends - counts
    seg_len = ((counts + al - 1) // al) * al
    seg_end = jnp.cumsum(seg_len)
    seg_start = seg_end - seg_len
    n_in = n + E * al + tm
    p = jnp.arange(n_in, dtype=jnp.int32)
    past = p[None, :] >= seg_end[:, None]
    rank = p - jnp.sum(jnp.where(past, (seg_len - counts)[:, None], 0), axis=0)
    end_of_p = ends[0] + jnp.sum(jnp.where(past[:-1], (ends[1:] - ends[:-1])[:, None], 0), axis=0)
    real = jnp.logical_and(rank < end_of_p, p < seg_end[-1])
    src = order[jnp.where(real, rank, 0)]
    row_tok = jnp.where(real, src // TOP_K, p % T)
    n_tiles_e = (counts + tm - 1) // tm
    tile_end = jnp.cumsum(n_tiles_e)
    tile_start = tile_end - n_tiles_e
    total = tile_end[-1]
    nv = n // tm + E
    v = jnp.arange(nv, dtype=jnp.int32)
    ve = jnp.minimum(jnp.sum((v[:, None] >= tile_end[None, :]).astype(jnp.int32), axis=1), E - 1)
    valid = v < total
    last = total - 1
    ve = jnp.where(valid, ve, ve[last]).astype(jnp.int32)
    vout = jnp.where(valid, v, last).astype(jnp.int32)
    start = (seg_start[ve] + (vout - tile_start[ve]) * tm).astype(jnp.int32)
    rem = counts % tm
    half_last = jnp.logical_and(rem > 0, rem <= tm // 2)
    is_half = jnp.logical_and(v == tile_end[ve] - 1, half_last[ve])
    kind = jnp.where(valid, jnp.where(is_half, VISIT_HALF, VISIT_FULL), VISIT_SKIP).astype(jnp.int32)
    pos = inv + jnp.sum(jnp.where(is_e, (tile_start * tm - offs)[:, None], 0), axis=0)
    pos_kmajor = pos.reshape(T, TOP_K).T.reshape(-1)
    return row_tok, pos_kmajor, (start, ve, vout, kind)


def _final_kernel(h_ref, w_ref, *refs, tm, n_lat_rows, L, B, D, alpha, with_next):
    y_refs, rest = refs[:TOP_K], refs[TOP_K:]
    r = _cond_row(pl.program_id(0), tm, n_lat_rows, L, B)
    gate = rest[0][pl.ds(r, 1), 5 * D:6 * D]
    topw = w_ref[...]
    f = None
    for k, y_ref in enumerate(y_refs):
        term = topw[:, k:k + 1] * y_ref[...].astype(F32)
        f = term if f is None else f + term
    u = alpha * h_ref[...] + gate * f
    h2 = _layer_norm_rows(u) * rest[1][...] + rest[2][...]
    if with_next:
        modn_ref, o_ref, a_ref = rest[3:]
        a_ref[...] = (_layer_norm_rows(h2) * (1.0 + modn_ref[pl.ds(r, 1), D:2 * D])
                      + modn_ref[pl.ds(r, 1), 0:D]).astype(a_ref.dtype)
    else:
        o_ref = rest[3]
    o_ref[...] = h2


def _final_call(h1, topw, y_assign, mod_l, ln_g, ln_b, B, L, alpha, mod_next=None, tm=256):
    n, D = h1.shape
    nt = n // tm
    with_next = mod_next is not None
    kern = functools.partial(_final_kernel, tm=tm, n_lat_rows=B * L, L=L, B=B, D=D, alpha=alpha,
                             with_next=with_next)
    row = lambda i: (i, 0)
    fixed = lambda i: (0, 0)
    y_specs = [pl.BlockSpec((tm, D), functools.partial(lambda i, k: (k * nt + i, 0), k=k)) for k in range(TOP_K)]
    mod_spec = pl.BlockSpec(mod_l.shape, fixed)
    row_spec = pl.BlockSpec((tm, D), row)
    return pl.pallas_call(
        kern,
        grid=(nt,),
        in_specs=[row_spec, pl.BlockSpec((tm, topw.shape[1]), row)] + y_specs
        + [mod_spec, pl.BlockSpec((1, D), fixed), pl.BlockSpec((1, D), fixed)]
        + ([mod_spec] if with_next else []),
        out_specs=[row_spec, row_spec] if with_next else row_spec,
        out_shape=([jax.ShapeDtypeStruct((n, D), F32), jax.ShapeDtypeStruct((n, D), BF16)] if with_next
                   else jax.ShapeDtypeStruct((n, D), F32)),
        compiler_params=_params(("parallel",)),
        name="final_ln",
    )(h1, topw, *([y_assign] * TOP_K), mod_l, ln_g.reshape(1, D), ln_b.reshape(1, D),
      *([mod_next] if with_next else []))


def kernel(x, c, ctx, c_ctx, w_mod, b_mod, w_in, pool_w, pool_scale, na_rpb, hg_lb, hg_norm_g, w_out, ln1_g, ln1_b, ln2_g, ln2_b, router_w, router_b, exp_w1, exp_b1, exp_w2, exp_b2):
    B, L, D = x.shape
    Lc = ctx.shape[1]
    depth = w_in.shape[0]
    alpha = (2 * depth) ** 0.25
    pool_width = pool_w.shape[1] * pool_w.shape[2]
    na_width = na_rpb.shape[1] * HEAD_DIM
    hg_width = hg_norm_g.shape[1]
    na_heads, hg_heads = na_width // HEAD_DIM, hg_width // HEAD_DIM
    c_naq = pool_width // HEAD_DIM
    c_nak, c_nav = c_naq + na_heads, c_naq + 2 * na_heads
    c_hq = c_naq + 3 * na_heads
    c_hff, c_hfb, c_hi, c_hg = (c_hq + k * hg_heads for k in range(1, 5))
    assert B + 1 <= COND_ROWS

    h_parts = (x.reshape(B * L, D), ctx.reshape(B * Lc, D))
    cond =jnp.concatenate([c, c_ctx[None], jnp.zeros((COND_ROWS - B - 1, D), F32)], axis=0)
    mod = _adaln_call(cond, w_mod, b_mod)
    lb_soft = jax.nn.softmax(hg_lb.astype(F32), axis=1)
    lower = jnp.cumsum(lb_soft, axis=1) - lb_soft[:, :1]
    rw_pad = jnp.pad(router_w, ((0, 0), (0, 0), (0, HEAD_DIM - N_EXPERTS)))
    rb_pad = jnp.pad(router_b, ((0, 0), (0, HEAD_DIM - N_EXPERTS)))

    T = B * (L + Lc)
    n_row_tiles = 8
    a_in = _ln_mod_call(h_parts, mod[0], B, L)
    for l in range(depth):
        last = l == depth - 1
        px = _wstat_matmul_call([a_in], w_in, l, T, T // n_row_tiles, 512, "in_proj")
        a_mix = _pool_call(px, pool_w[l].astype(BF16), pool_scale[l], B, L, Lc)
        b_x, b_c = _na_call(px, na_rpb[l], B, L, Lc, c_naq, c_nak, c_nav, na_heads)
        b_mix = jnp.concatenate([b_x, b_c], axis=0)
        o_f, o_b = _hgrn_call(px, lower[0, l].reshape(hg_heads, 1, HEAD_DIM),
                              lower[1, l].reshape(hg_heads, 1, HEAD_DIM), B, L, Lc,
                              c_hq, c_hff, c_hfb, c_hi, hg_heads)
        c_mix = _readout_call(o_f, o_b, px, hg_norm_g[l], c_hg, T // n_row_tiles)
        n_out = B * L if last else T
        y = _wstat_matmul_call([a_mix, b_mix, c_mix], w_out, l, n_out, n_out // n_row_tiles, 512, "out_proj")
        h1, fin, topw, topi = _post_mix_call(y, h_parts, mod[l], ln1_g[l], ln1_b[l], rw_pad[l],
                                              rb_pad[l].reshape(1, -1), B, L, alpha)
        row_tok, pos_kmajor, visits = _moe_dispatch(topi[:, :TOP_K], MOE_TILE)
        xs = fin.at[row_tok].get(mode="promise_in_bounds")
        ys = _moe_call(xs, visits, exp_w1, exp_b1, exp_w2, exp_b2, l)
        y_assign = ys.at[pos_kmajor].get(mode="promise_in_bounds")
        if last:
            hall = _final_call(h1, topw, y_assign, mod[l], ln2_g[l], ln2_b[l], B, L, alpha)
        else:
            hall, a_in = _final_call(h1, topw, y_assign, mod[l], ln2_g[l], ln2_b[l], B, L, alpha,
                                     mod_next=mod[l + 1])
        h_parts = (hall,)
    return hall.reshape(B, L, D)
```

```python
import functools

import jax
import jax.numpy as jnp
import numpy as np
from jax import lax
from jax.experimental import pallas as pl
from jax.experimental.pallas import tpu as pltpu

F32 = jnp.float32
BF16 = jnp.bfloat16

HEAD_DIM = 128
GRID_W = 64
POOL_WINDOWS = (2, 4, 8, 16)
NA_ROWS = 8
NA_COLS = 16
FORGET_EPS = 1e-20
N_EXPERTS = 32
TOP_K = 4
SWIGLU_LIMIT = 7.0
SWIGLU_ALPHA = 1.702
ROPE_BASE = 10000.0
LN_EPS = 1e-6

VMEM_LIMIT_BYTES = 56 * 1024 * 1024
HG_CHUNK = 256
NEG_BIG = -1e30


def _params(sem):
    return pltpu.CompilerParams(dimension_semantics=sem, vmem_limit_bytes=VMEM_LIMIT_BYTES)


def _dot(a, b):
    return jnp.dot(a, b, preferred_element_type=F32)


def _dot_nt(a, b):
    return lax.dot_general(a, b, (((1,), (1,)), ((), ())), preferred_element_type=F32)


def _sigmoid(x):
    return 1.0 / (1.0 + jnp.exp(-x))


LOG2E = 1.4426950408889634
HG_HALF = HG_CHUNK // 2
HG_HEADS_PER_STEP = 4


def _neg_abs(x):
    bits = lax.bitcast_convert_type(x, jnp.uint32) | jnp.uint32(0x80000000)
    return lax.bitcast_convert_type(bits, F32)


def _hgrn_half_scores(q, k, b2, g2, fc, lev, reverse):
    n = q.shape[0]
    row = lax.broadcasted_iota(jnp.int32, (n, HEAD_DIM), 0)
    a = jnp.where(lev == -1, _dot_nt(q.astype(BF16), k.astype(BF16)), 0.0)
    h, lvl = 1, 0
    while h < n:
        if h < 8:
            upper = (row & h) != 0
            t_role = jnp.logical_not(upper) if reverse else upper
            if h == 1:
                x = jnp.where(t_role, q * fc, k)
            elif h == 2:
                r4 = row & 3
                g_prev, g_next = pltpu.roll(g2, 1, 0), pltpu.roll(g2, n - 1, 0)
                if reverse:
                    e = jnp.where(r4 == 0, g2 + g_next, jnp.where(r4 == 1, g2, jnp.where(r4 == 2, 0.0, g_prev)))
                else:
                    e = jnp.where(r4 == 0, g_next, jnp.where(r4 == 1, 0.0, jnp.where(r4 == 2, g2, g2 + g_prev)))
                x = jnp.where(t_role, q, k) * jnp.exp2(e)
            else:
                m0 = h if reverse else h - 1
                b3 = b2.reshape(n // (2 * h), 2 * h, HEAD_DIM)
                ref = jnp.broadcast_to(b3[:, m0:m0 + 1, :], b3.shape).reshape(n, HEAD_DIM)
                x = jnp.where(t_role, q, k) * jnp.exp2(_neg_abs(b2 - ref))
            x = x.astype(BF16)
            a = jnp.where(lev == lvl, _dot_nt(x, x), a)
        else:
            nb = n // (2 * h)
            q4 = q.reshape(nb, 2, h, HEAD_DIM)
            k4 = k.reshape(nb, 2, h, HEAD_DIM)
            b4 = b2.reshape(nb, 2, h, HEAD_DIM)
            s_half, t_half, m_row = (1, 0, 0) if reverse else (0, 1, h - 1)
            ref = b4[:, s_half:s_half + 1, m_row:m_row + 1, :]
            xt = q4[:, t_half] * jnp.exp2(b4[:, t_half] - ref[:, 0])
            xs = k4[:, s_half] * jnp.exp2(ref[:, 0] - b4[:, s_half])
            halves = [xt, xs] if reverse else [xs, xt]
            x = jnp.stack(halves, axis=1).reshape(n, HEAD_DIM).astype(BF16)
            g = _dot_nt(xt.reshape(n // 2, HEAD_DIM).astype(BF16), x)
            a4 = a.reshape(nb, 2, h, n)
            lev_t = lev.reshape(nb, 2, h, n)[:, t_half]
            a_t = jnp.where(lev_t == lvl, g.reshape(nb, h, n), a4[:, t_half])
            halves = [a_t, a4[:, s_half]] if reverse else [a4[:, s_half], a_t]
            a = jnp.stack(halves, axis=1).reshape(n, n)
        h *= 2
        lvl += 1
    return a


def _hgrn_direction(q, z, v_bf, lb, st_ref, lev, tri_bf, reverse):
    C, H = HG_CHUNK, HG_HALF
    q = q * _sigmoid(q)
    sig = _sigmoid(z)
    fc = jnp.maximum(lb + (1.0 - lb) * sig, FORGET_EPS)
    g2 = jnp.log(fc) * LOG2E
    k = (1.0 - lb) * (1.0 - sig)

    g_hi = g2.astype(BF16)
    r1 = g2 - g_hi.astype(F32)
    g_mid = r1.astype(BF16)
    g_lo = (r1 - g_mid.astype(F32)).astype(BF16)
    b2 = _dot(tri_bf, g_hi) + _dot(tri_bf, g_mid) + _dot(tri_bf, g_lo)

    lo, hi = slice(0, H), slice(H, C)
    a_lo = _hgrn_half_scores(q[lo], k[lo], b2[lo], g2[lo], fc[lo], lev, reverse).astype(BF16)
    a_hi = _hgrn_half_scores(q[hi], k[hi], b2[hi], g2[hi], fc[hi], lev, reverse).astype(BF16)
    t_sl, s_sl, m = (lo, hi, H) if reverse else (hi, lo, H - 1)
    ref = b2[m:m + 1, :]
    qt = (q[t_sl] * jnp.exp2(b2[t_sl] - ref)).astype(BF16)
    ks = (k[s_sl] * jnp.exp2(ref - b2[s_sl])).astype(BF16)
    cross = _dot(_dot_nt(qt, ks).astype(BF16), v_bf[s_sl])
    o_lo = _dot(a_lo, v_bf[lo])
    o_hi = _dot(a_hi, v_bf[hi])
    o_intra = jnp.concatenate([o_lo + cross, o_hi] if reverse else [o_lo, o_hi + cross], axis=0)

    st = st_ref[...]
    tot = b2[0:1, :] if reverse else b2[C - 1:C, :]
    qdec = (q * jnp.exp2(b2)).astype(BF16)
    o_inter = _dot_nt(qdec, st.astype(BF16))
    kdec = (k * jnp.exp2(tot - b2)).astype(BF16)
    vt = v_bf.astype(F32).T.astype(BF16)
    st_ref[...] = jnp.exp2(tot) * st + _dot(vt, kdec)
    return o_intra + o_inter


def _hgrn_kernel(qf_ref, zf_ref, vf_ref, qb_ref, zb_ref, vb_ref, lbf_ref, lbb_ref, levf_ref, levb_ref,
                 trif_ref, trib_ref, of_ref, ob_ref, sf_ref, sb_ref):
    @pl.when(pl.program_id(2) == 0)
    def _():
        sf_ref[...] = jnp.zeros_like(sf_ref)
        sb_ref[...] = jnp.zeros_like(sb_ref)

    for hh in range(HG_HEADS_PER_STEP):
        sl = slice(hh * HEAD_DIM, (hh + 1) * HEAD_DIM)
        of_ref[:, sl] = _hgrn_direction(qf_ref[:, sl].astype(F32), zf_ref[:, sl].astype(F32), vf_ref[:, sl],
                                        lbf_ref[0][:, sl], sf_ref.at[hh], levf_ref[...], trif_ref[...],
                                        False).astype(of_ref.dtype)
        ob_ref[:, sl] = _hgrn_direction(qb_ref[:, sl].astype(F32), zb_ref[:, sl].astype(F32), vb_ref[:, sl],
                                        lbb_ref[0][:, sl], sb_ref.at[hh], levb_ref[...], trib_ref[...],
                                        True).astype(ob_ref.dtype)


def _hgrn_call(px, lb_f, lb_b, B, L, Lc, col_q, col_ff, col_fb, col_i, n_heads):
    C, H = HG_CHUNK, HG_HALF
    T = px.shape[0]
    nlc, ncc = L // C, Lc // C
    ns = nlc + ncc
    ctx0 = B * nlc

    def fwd_row(b, s):
        return jnp.where(s < ncc, ctx0 + b * ncc + s, b * nlc + s - ncc)

    def bwd_row(b, s):
        return jnp.where(s < ncc, ctx0 + b * ncc + (ncc - 1 - s), b * nlc + (nlc - 1 - (s - ncc)))

    G = HG_HEADS_PER_STEP
    GW = G * HEAD_DIM
    assert n_heads % G == 0 and all(c % G == 0 for c in (col_q, col_ff, col_fb, col_i))

    def spec(row_fn, col):
        return pl.BlockSpec((C, GW), lambda b, h, s: (row_fn(b, s), col // G + h))

    idx = np.arange(H)
    xr = idx[:, None] ^ idx[None, :]
    level = np.floor(np.log2(np.maximum(xr, 1))).astype(np.int32)
    lev_f = np.where(xr == 0, -1, np.where(idx[:, None] > idx[None, :], level, -2)).astype(np.int32)
    lev_b = np.ascontiguousarray(lev_f.T)
    idc = np.arange(C)
    tri_f = (idc[:, None] >= idc[None, :]).astype(np.float32)
    tri_b = (idc[:, None] <= idc[None, :]).astype(np.float32)
    const = lambda shape: pl.BlockSpec(shape, lambda b, h, s: (0,) * len(shape))
    lb_spec = pl.BlockSpec((1, 1, GW), lambda b, h, s: (h, 0, 0))
    out_sds = jax.ShapeDtypeStruct((T, n_heads * HEAD_DIM), BF16)
    state = pltpu.VMEM((G, HEAD_DIM, HEAD_DIM), F32)
    return pl.pallas_call(
        _hgrn_kernel,
        grid=(B, n_heads // G, ns),
        in_specs=[spec(fwd_row, col_q), spec(fwd_row, col_ff), spec(fwd_row, col_i),
                  spec(bwd_row, col_q), spec(bwd_row, col_fb), spec(bwd_row, col_i),
                  lb_spec, lb_spec, const((H, H)), const((H, H)), const((C, C)), const((C, C))],
        out_specs=[pl.BlockSpec((C, GW), lambda b, h, s: (fwd_row(b, s), h)),
                   pl.BlockSpec((C, GW), lambda b, h, s: (bwd_row(b, s), h))],
        out_shape=[out_sds, out_sds],
        scratch_shapes=[state, state],
        compiler_params=_params(("parallel", "parallel", "arbitrary")),
        name="hgrn_scan",
    )(px, px, px, px, px, px, lb_f.reshape(n_heads // G, 1, GW), lb_b.reshape(n_heads // G, 1, GW),
      jnp.asarray(lev_f), jnp.asarray(lev_b), jnp.asarray(tri_f, BF16), jnp.asarray(tri_b, BF16))


NA_QROWS = 4
NA_KROWS = NA_QROWS + NA_ROWS


def _na_block_cases(rows):
    assert rows % NA_QROWS == 0 and rows >= 2 * NA_ROWS
    nb = rows // NA_QROWS
    cases = []
    for i in (0, 1, nb - 1):
        r0 = NA_QROWS * i
        ks = min(max(r0 - NA_ROWS // 2, 0), rows - NA_KROWS)
        per_q = []
        for qr in range(NA_QROWS):
            r = r0 + qr
            rs = min(max(r - NA_ROWS // 2, 0), rows - NA_ROWS)
            per_q.append((rs - ks, ks - r + NA_ROWS - 1))
        cases.append(per_q)
    return cases


def _na_kernel(rpb_ref, q_ref, k_ref, v_ref, qc_ref, kc_ref, vc_ref, cos_ref, sin_ref,
               ox_ref, oc_ref, qr_ref, kr_ref, tb_ref, bias_ref, *, rows):
    h = pl.program_id(0)
    L = q_ref.shape[0]
    W = GRID_W
    scale = HEAD_DIM ** -0.5
    n_dr = 2 * NA_ROWS - 1
    n_dc = 2 * NA_COLS - 1

    @pl.when(pl.program_id(1) == 0)
    def _():
        c_i = lax.broadcasted_iota(jnp.int32, (W, W), 0)
        kc_i = lax.broadcasted_iota(jnp.int32, (W, W), 1)
        cs = jnp.clip(c_i - NA_COLS // 2, 0, W - NA_COLS)
        in_win = (kc_i >= cs) & (kc_i < cs + NA_COLS)
        c_off = kc_i - c_i + (NA_COLS - 1)
        for dr in range(n_dr):
            acc = jnp.zeros((W, W), F32)
            for j in range(n_dc):
                acc = jnp.where(c_off == j, rpb_ref[h, dr, j], acc)
            tb_ref[dr] = jnp.where(in_win, acc, NEG_BIG)
        neg = jnp.full((W, W), NEG_BIG, F32)
        for ci, per_q in enumerate(_na_block_cases(rows)):
            for qr, (j0, dr0) in enumerate(per_q):
                for j in range(NA_KROWS):
                    valid = j0 <= j < j0 + NA_ROWS
                    blk = tb_ref[dr0 + j] if valid else neg
                    bias_ref[ci, qr * W:(qr + 1) * W, j * W:(j + 1) * W] = blk

    RC = 256
    lane = lax.broadcasted_iota(jnp.int32, (RC, HEAD_DIM), 1)
    first = (lane & (HEAD_DIM // 2 - 1)) < HEAD_DIM // 4

    def rope_chunk(i, carry):
        sl = pl.ds(pl.multiple_of(i * RC, RC), RC)
        cos = cos_ref[sl, :]
        sin = sin_ref[sl, :]
        for src, dst, mul in ((q_ref, qr_ref, scale), (k_ref, kr_ref, 1.0)):
            t = src[sl, :].astype(F32)
            partner = jnp.where(first, pltpu.roll(t, HEAD_DIM - HEAD_DIM // 4, 1), pltpu.roll(t, HEAD_DIM // 4, 1))
            dst[sl, :] = ((t * cos + partner * sin) * mul).astype(BF16)
        return carry

    lax.fori_loop(0, L // RC, rope_chunk, 0)

    kc = kc_ref[...]
    vc = vc_ref[...]
    nb = rows // NA_QROWS
    QB = NA_QROWS * W
    KB = NA_KROWS * W

    def block(i, carry):
        ks = jnp.clip(i * NA_QROWS - NA_ROWS // 2, 0, rows - NA_KROWS)
        case = jnp.where(i == 0, 0, jnp.where(i == nb - 1, 2, 1))
        qs = pl.ds(pl.multiple_of(i * QB, QB), QB)
        kslice = pl.ds(pl.multiple_of(ks * W, W), KB)
        qb = qr_ref[qs, :]
        s_loc = _dot_nt(qb, kr_ref[kslice, :]) + bias_ref[case]
        s_ctx = _dot_nt(qb, kc)
        m = jnp.maximum(jnp.max(s_loc, -1, keepdims=True), jnp.max(s_ctx, -1, keepdims=True))
        p_loc = jnp.exp(s_loc - m)
        p_ctx = jnp.exp(s_ctx - m)
        den = jnp.sum(p_loc, -1, keepdims=True) + jnp.sum(p_ctx, -1, keepdims=True)
        o = _dot(p_loc.astype(BF16), v_ref[kslice, :]) + _dot(p_ctx.astype(BF16), vc)
        ox_ref[qs, :] = (o / den).astype(ox_ref.dtype)
        return carry

    lax.fori_loop(0, nb, block, 0, unroll=4)

    qc = (qc_ref[...].astype(F32) * scale).astype(BF16)
    s = _dot_nt(qc, kc)
    p = jnp.exp(s - jnp.max(s, -1, keepdims=True))
    oc = _dot(p.astype(BF16), vc) / jnp.sum(p, -1, keepdims=True)
    oc_ref[...] = oc.astype(oc_ref.dtype)


def _rope_tables(L):
    pos = jnp.arange(L)
    half = HEAD_DIM // 2
    inv = ROPE_BASE ** (-jnp.arange(0, half, 2, dtype=F32) / half)

    def tab(p):
        ang = p.astype(F32)[:, None] * inv[None]
        cos, sin = jnp.cos(ang), jnp.sin(ang)
        return jnp.concatenate([cos, cos], -1), jnp.concatenate([-sin, sin], -1)

    c_r, s_r = tab(pos // GRID_W)
    c_c, s_c = tab(pos % GRID_W)
    return jnp.concatenate([c_r, c_c], -1), jnp.concatenate([s_r, s_c], -1)


def _na_call(px, rpb, B, L, Lc, col_q, col_k, col_v, n_heads):
    rows = L // GRID_W
    cos, sin = _rope_tables(L)
    ctx0 = B * L // Lc

    def lat(col):
        return pl.BlockSpec((L, HEAD_DIM), lambda h, b: (b, col + h))

    def cx(col):
        return pl.BlockSpec((Lc, HEAD_DIM), lambda h, b: (ctx0 + b, col + h))

    tab = pl.BlockSpec((L, HEAD_DIM), lambda h, b: (0, 0))
    W = GRID_W
    return pl.pallas_call(
        functools.partial(_na_kernel, rows=rows),
        grid=(n_heads, B),
        in_specs=[pl.BlockSpec(memory_space=pltpu.SMEM), lat(col_q), lat(col_k), lat(col_v),
                  cx(col_q), cx(col_k), cx(col_v), tab, tab],
        out_specs=[pl.BlockSpec((L, HEAD_DIM), lambda h, b: (b, h)),
                   pl.BlockSpec((Lc, HEAD_DIM), lambda h, b: (b, h))],
        out_shape=[jax.ShapeDtypeStruct((B * L, n_heads * HEAD_DIM), BF16),
                   jax.ShapeDtypeStruct((B * Lc, n_heads * HEAD_DIM), BF16)],
        scratch_shapes=[pltpu.VMEM((L, HEAD_DIM), BF16), pltpu.VMEM((L, HEAD_DIM), BF16),
                        pltpu.VMEM((2 * NA_ROWS - 1, W, W), F32),
                        pltpu.VMEM((3, NA_QROWS * W, NA_KROWS * W), F32)],
        compiler_params=_params(("parallel", "arbitrary")),
        name="neighborhood_attention",
    )(rpb, px, px, px, px, px, px, cos, sin)


POOL_TILE = 256
POOL_HALO = 16


def _pool_kernel(prev_ref, cur_ref, next_ref, w_ref, scale_ref, o_ref, *, n_lat_tiles, nl, nc):
    i = pl.program_id(0)
    is_lat = i < n_lat_tiles
    j = jnp.where(is_lat, i % nl, (i - n_lat_tiles) % nc)
    n = jnp.where(is_lat, nl, nc)
    TQ, HL = POOL_TILE, POOL_HALO
    lo_min = jnp.where(j > 0, -HL, 0)
    hi_max = jnp.where(j < n - 1, TQ + HL, TQ)
    src = jnp.concatenate([prev_ref[...], cur_ref[...], next_ref[...]], axis=0)
    t = lax.broadcasted_iota(jnp.int32, (TQ, TQ + 2 * HL), 0)
    s = lax.broadcasted_iota(jnp.int32, (TQ, TQ + 2 * HL), 1) - HL
    Dg = w_ref.shape[1]
    t1 = lax.broadcasted_iota(jnp.int32, (TQ, Dg), 0)
    for g, w in enumerate(POOL_WINDOWS):
        lo = jnp.maximum(t - w // 2, lo_min)
        hi = jnp.minimum(t + (w - w // 2), hi_max)
        band = jnp.where(s >= lo, jnp.where(s < hi, 1.0, 0.0), 0.0).astype(BF16)
        cnt = (jnp.minimum(t1 + (w - w // 2), hi_max) - jnp.maximum(t1 - w // 2, lo_min)).astype(F32)
        cols = slice(g * Dg, (g + 1) * Dg)
        win = _dot(band, src[:, cols])
        d = win / cnt - cur_ref[:, cols].astype(F32)
        y = _dot(d.astype(BF16), w_ref[g]) * scale_ref[:, cols]
        o_ref[:, cols] = y.astype(o_ref.dtype)


def _pool_call(px, pool_w_bf, pool_scale, B, L, Lc):
    T = px.shape[0]
    G, Dg, _ = pool_w_bf.shape
    PW = G * Dg
    TQ, HL = POOL_TILE, POOL_HALO
    r = TQ // HL
    n_tiles = T // TQ
    n_halo = T // HL
    kern = functools.partial(_pool_kernel, n_lat_tiles=B * L // TQ, nl=L // TQ, nc=Lc // TQ)
    return pl.pallas_call(
        kern,
        grid=(n_tiles,),
        in_specs=[pl.BlockSpec((HL, PW), lambda i: (jnp.maximum(i * r - 1, 0), 0)),
                  pl.BlockSpec((TQ, PW), lambda i: (i, 0)),
                  pl.BlockSpec((HL, PW), lambda i: (jnp.minimum((i + 1) * r, n_halo - 1), 0)),
                  pl.BlockSpec((G, Dg, Dg), lambda i: (0, 0, 0)),
                  pl.BlockSpec((1, PW), lambda i: (0, 0))],
        out_specs=pl.BlockSpec((TQ, PW), lambda i: (i, 0)),
        out_shape=jax.ShapeDtypeStruct((T, PW), BF16),
        compiler_params=_params(("parallel",)),
        name="multiscale_pool",
    )(px, px, px, pool_w_bf, pool_scale.reshape(1, PW).astype(F32))


READOUT_HEADS = 4


def _readout_kernel(of_ref, ob_ref, gate_ref, ng_ref, o_ref):
    for hh in range(READOUT_HEADS):
        sl = slice(hh * HEAD_DIM, (hh + 1) * HEAD_DIM)
        o = of_ref[:, sl].astype(F32) + ob_ref[:, sl].astype(F32)
        o = o * lax.rsqrt(jnp.mean(o * o, -1, keepdims=True) + LN_EPS)
        g = gate_ref[:, sl].astype(F32)
        o_ref[:, sl] = (o * ng_ref[:, sl] * (g * _sigmoid(g))).astype(o_ref.dtype)


def _readout_call(o_f, o_b, px, norm_g, col_gate, tm):
    T, HW = o_f.shape
    GW = READOUT_HEADS * HEAD_DIM
    assert HW % GW == 0 and col_gate % READOUT_HEADS == 0
    blk = lambda col: pl.BlockSpec((tm, GW), lambda i, h: (i, col // READOUT_HEADS + h))
    return pl.pallas_call(
        _readout_kernel,
        grid=(T // tm, HW // GW),
        in_specs=[blk(0), blk(0), blk(col_gate), pl.BlockSpec((1, GW), lambda i, h: (0, h))],
        out_specs=blk(0),
        out_shape=jax.ShapeDtypeStruct((T, HW), BF16),
        compiler_params=_params(("parallel", "parallel")),
        name="hgrn_readout",
    )(o_f, o_b, px, norm_g.reshape(1, HW).astype(F32))


COND_ROWS = 16


def _adaln_kernel(cond_ref, w_ref, b_ref, o_ref):
    cnd = cond_ref[...]
    a = (cnd * _sigmoid(cnd)).astype(BF16)
    o_ref[0] = _dot(a, w_ref[0].astype(BF16)) + b_ref[0]


def _adaln_call(cond, w_mod, b_mod, tn=1024):
    depth, D, N = w_mod.shape
    return pl.pallas_call(
        _adaln_kernel,
        grid=(depth, N // tn),
        in_specs=[pl.BlockSpec((COND_ROWS, D), lambda l, j: (0, 0)),
                  pl.BlockSpec((1, D, tn), lambda l, j: (l, 0, j)),
                  pl.BlockSpec((1, 1, tn), lambda l, j: (l, 0, j))],
        out_specs=pl.BlockSpec((1, COND_ROWS, tn), lambda l, j: (l, 0, j)),
        out_shape=jax.ShapeDtypeStruct((depth, COND_ROWS, N), F32),
        compiler_params=_params(("parallel", "parallel")),
        name="adaln",
    )(cond, w_mod, b_mod.reshape(depth, 1, N))


def _cond_row(i, tm, n_lat_rows, L, B):
    return jnp.where(i * tm < n_lat_rows, (i * tm) // L, B)


def _layer_norm_rows(x):
    mu = jnp.mean(x, -1, keepdims=True)
    xc = x - mu
    var = jnp.mean(xc * xc, -1, keepdims=True)
    return xc * lax.rsqrt(var + LN_EPS)


def _row_part_specs(parts, tm):
    specs, firsts, lo = [], [], 0
    for part in parts:
        n_t = part.shape[0] // tm
        assert part.shape[0] % tm == 0
        specs.append(pl.BlockSpec((tm, part.shape[1]),
                                  functools.partial(lambda i, lo, n_t: (jnp.clip(i - lo, 0, n_t - 1), 0), lo=lo, n_t=n_t)))
        firsts.append(lo)
        lo += n_t
    return specs, tuple(firsts)


def _read_row_parts(refs, firsts, i):
    val = refs[0][...]
    for ref, lo in zip(refs[1:], firsts[1:]):
        val = jnp.where(i >= lo, ref[...], val)
    return val


def _ln_mod_kernel(*refs, firsts, tm, n_lat_rows, L, B, D):
    n = len(firsts)
    mod_ref, o_ref = refs[n], refs[n + 1]
    i = pl.program_id(0)
    r = _cond_row(i, tm, n_lat_rows, L, B)
    shift = mod_ref[pl.ds(r, 1), 0:D]
    scale = mod_ref[pl.ds(r, 1), D:2 * D]
    x = _read_row_parts(refs[:n], firsts, i)
    o_ref[...] = (_layer_norm_rows(x) * (1.0 + scale) + shift).astype(o_ref.dtype)


def _ln_mod_call(h_parts, mod_l, B, L, tm=256):
    T = sum(p.shape[0] for p in h_parts)
    D = h_parts[0].shape[1]
    specs, firsts = _row_part_specs(h_parts, tm)
    kern = functools.partial(_ln_mod_kernel, firsts=firsts, tm=tm, n_lat_rows=B * L, L=L, B=B, D=D)
    return pl.pallas_call(
        kern,
        grid=(T // tm,),
        in_specs=specs + [pl.BlockSpec(mod_l.shape, lambda i: (0, 0))],
        out_specs=pl.BlockSpec((tm, D), lambda i: (i, 0)),
        out_shape=jax.ShapeDtypeStruct((T, D), BF16),
        compiler_params=_params(("parallel",)),
        name="ln_modulate",
    )(*h_parts, mod_l)


def _wstat_matmul_kernel(*refs, widths):
    n = len(widths)
    a_refs, w_ref, o_ref, wbf_ref = refs[:n], refs[n], refs[n + 1], refs[n + 2]

    @pl.when(pl.program_id(1) == 0)
    def _():
        wbf_ref[...] = w_ref[0].astype(BF16)

    acc = None
    k0 = 0
    for a_ref, kw in zip(a_refs, widths):
        part = _dot(a_ref[...], wbf_ref[k0:k0 + kw, :])
        acc = part if acc is None else acc + part
        k0 += kw
    o_ref[...] = acc.astype(o_ref.dtype)


def _wstat_matmul_call(a_list, w_all, layer, n_rows, tm, tn, name):
    widths = tuple(a.shape[1] for a in a_list)
    _, K, N = w_all.shape
    assert sum(widths) == K and n_rows % tm == 0 and N % tn == 0
    return pl.pallas_call(
        functools.partial(_wstat_matmul_kernel, widths=widths),
        grid=(N // tn, n_rows // tm),
        in_specs=[pl.BlockSpec((tm, kw), lambda j, i: (i, 0)) for kw in widths]
        + [pl.BlockSpec((1, K, tn), lambda j, i: (layer, 0, j))],
        out_specs=pl.BlockSpec((tm, tn), lambda j, i: (i, j)),
        out_shape=jax.ShapeDtypeStruct((n_rows, N), BF16),
        scratch_shapes=[pltpu.VMEM((K, tn), BF16)],
        compiler_params=_params(("parallel", "arbitrary")),
        name=name,
    )(*a_list, w_all)


def _split_bf16(x):
    hi = x.astype(BF16)
    return hi, (x - hi.astype(F32)).astype(BF16)


def _post_mix_kernel(y_ref, *refs, firsts, tm, n_lat_rows, L, B, D, alpha):
    n = len(firsts)
    mod_ref, g_ref, b_ref, rw_ref, rb_ref, h1_ref, fin_ref, gates_ref, topi_ref = refs[n:]
    i = pl.program_id(0)
    r = _cond_row(i, tm, n_lat_rows, L, B)
    mrow = lambda slot: mod_ref[pl.ds(r, 1), slot * D:(slot + 1) * D]
    u = alpha * _read_row_parts(refs[:n], firsts, i) + mrow(2) * y_ref[...].astype(F32)
    h1 = _layer_norm_rows(u) * g_ref[...] + b_ref[...]
    h1_ref[...] = h1
    fin = _layer_norm_rows(h1) * (1.0 + mrow(4)) + mrow(3)
    fin_ref[...] = fin.astype(BF16)
    f_hi, f_lo = _split_bf16(fin)
    w_hi, w_lo = _split_bf16(rw_ref[...])
    logits = _dot(f_hi, w_hi) + _dot(f_hi, w_lo) + _dot(f_lo, w_hi) + rb_ref[...]
    lane = lax.broadcasted_iota(jnp.int32, logits.shape, 1)
    ninf = jnp.float32(-jnp.inf)
    xs = jnp.where(lane < N_EXPERTS, logits, ninf)
    sel_v, sel_i = [], []
    for _ in range(TOP_K):
        m = jnp.max(xs, -1, keepdims=True)
        idx = jnp.min(jnp.where(xs == m, lane, logits.shape[1]), -1, keepdims=True)
        sel_v.append(m)
        sel_i.append(idx)
        xs = jnp.where(lane == idx, ninf, xs)
    ex = [jnp.exp(v - sel_v[0]) for v in sel_v]
    den = ex[0]
    for e in ex[1:]:
        den = den + e
    topw = jnp.zeros(logits.shape, F32)
    topi = jnp.zeros(logits.shape, jnp.int32)
    for j in range(TOP_K):
        topw = jnp.where(lane == j, ex[j] / den, topw)
        topi = jnp.where(lane == j, sel_i[j], topi)
    gates_ref[...] = topw
    topi_ref[...] = topi


def _post_mix_call(y, h_parts, mod_l, ln_g, ln_b, rw_pad, rb_pad, B, L, alpha, tm=256):
    n_out, D = y.shape
    NE = rw_pad.shape[1]
    h_specs, firsts = _row_part_specs(h_parts, tm)
    kern = functools.partial(_post_mix_kernel, firsts=firsts, tm=tm, n_lat_rows=B * L, L=L, B=B, D=D, alpha=alpha)
    row = lambda i: (i, 0)
    fixed = lambda i: (0, 0)
    return pl.pallas_call(
        kern,
        grid=(n_out // tm,),
        in_specs=[pl.BlockSpec((tm, D), row)] + h_specs
        + [pl.BlockSpec(mod_l.shape, fixed),
                  pl.BlockSpec((1, D), fixed), pl.BlockSpec((1, D), fixed),
                  pl.BlockSpec((D, NE), fixed), pl.BlockSpec((1, NE), fixed)],
        out_specs=[pl.BlockSpec((tm, D), row), pl.BlockSpec((tm, D), row),
                   pl.BlockSpec((tm, NE), row), pl.BlockSpec((tm, NE), row)],
        out_shape=[jax.ShapeDtypeStruct((n_out, D), F32), jax.ShapeDtypeStruct((n_out, D), BF16),
                   jax.ShapeDtypeStruct((n_out, NE), F32), jax.ShapeDtypeStruct((n_out, NE), jnp.int32)],
        compiler_params=_params(("parallel",)),
        name="post_mix_router",
    )(y, *h_parts, mod_l, ln_g.reshape(1, D), ln_b.reshape(1, D), rw_pad, rb_pad)


MOE_TILE = 512
MOE_ALIGN = 16


def _first_visit_of_expert(v, ve_ref):
    return (v == 0) | (ve_ref[jnp.maximum(v - 1, 0)] != ve_ref[v])


def _moe_up_kernel(start_ref, ve_ref, vout_ref, vv_ref, x_ref, w1_ref, b1_ref, o_ref, wbf_ref, *, DE):
    v = pl.program_id(0)

    @pl.when(_first_visit_of_expert(v, ve_ref))
    def _():
        wbf_ref[...] = w1_ref[0, 0].astype(BF16)

    @pl.when(vv_ref[v] > 0)
    def _():
        hcat = _dot(x_ref[...], wbf_ref[...]) + b1_ref[0, 0]
        gate = jnp.minimum(hcat[:, :DE], SWIGLU_LIMIT)
        up = jnp.clip(hcat[:, DE:], -SWIGLU_LIMIT, SWIGLU_LIMIT)
        o_ref[...] = ((up + 1.0) * (gate * _sigmoid(SWIGLU_ALPHA * gate))).astype(o_ref.dtype)


def _moe_down_kernel(start_ref, ve_ref, vout_ref, vv_ref, h_ref, w2_ref, b2_ref, o_ref, wbf_ref):
    v = pl.program_id(0)

    @pl.when(_first_visit_of_expert(v, ve_ref))
    def _():
        wbf_ref[...] = w2_ref[0, 0].astype(BF16)

    @pl.when(vv_ref[v] > 0)
    def _():
        o_ref[...] = (_dot(h_ref[...], wbf_ref[...]) + b2_ref[0, 0]).astype(o_ref.dtype)


def _moe_call(xs, visits, w1_all, b1_all, w2_all, b2_all, layer):
    n_in, D = xs.shape
    _, E, _, DE2 = w1_all.shape
    DE = DE2 // 2
    tm = MOE_TILE
    nv = visits[0].shape[0]
    window = lambda v, st, ve, vo, vv: (pl.multiple_of(st[v], MOE_ALIGN), 0)
    out_tile = lambda v, st, ve, vo, vv: (vo[v], 0)
    expert = lambda v, st, ve, vo, vv: (layer, ve[v], 0, 0)
    hdn = pl.pallas_call(
        functools.partial(_moe_up_kernel, DE=DE),
        grid_spec=pltpu.PrefetchScalarGridSpec(
            num_scalar_prefetch=4, grid=(nv,),
            in_specs=[pl.BlockSpec((pl.Element(tm), pl.Element(D)), window),
                      pl.BlockSpec((1, 1, D, DE2), expert),
                      pl.BlockSpec((1, 1, 1, DE2), expert)],
            out_specs=pl.BlockSpec((tm, DE), out_tile),
            scratch_shapes=[pltpu.VMEM((D, DE2), BF16)]),
        out_shape=jax.ShapeDtypeStruct((nv * tm, DE), BF16),
        compiler_params=_params(("arbitrary",)),
        name="moe_up",
    )(*visits, xs, w1_all, b1_all.reshape(b1_all.shape[0], E, 1, DE2))
    return pl.pallas_call(
        _moe_down_kernel,
        grid_spec=pltpu.PrefetchScalarGridSpec(
            num_scalar_prefetch=4, grid=(nv,),
            in_specs=[pl.BlockSpec((tm, DE), out_tile),
                      pl.BlockSpec((1, 1, DE, D), expert),
                      pl.BlockSpec((1, 1, 1, D), expert)],
            out_specs=pl.BlockSpec((tm, D), out_tile),
            scratch_shapes=[pltpu.VMEM((DE, D), BF16)]),
        out_shape=jax.ShapeDtypeStruct((nv * tm, D), BF16),
        compiler_params=_params(("arbitrary",)),
        name="moe_down",
    )(*visits, hdn, w2_all, b2_all.reshape(b2_all.shape[0], E, 1, D))


def _moe_dispatch(topi, tm):
    T = topi.shape[0]
    E = N_EXPERTS
    n = T * TOP_K
    al = MOE_ALIGN
    flat_e = topi.reshape(-1)
    order = jnp.argsort(flat_e, stable=True).astype(jnp.int32)
    inv = jnp.argsort(order).astype(jnp.int32)
    e_col = jnp.arange(E, dtype=jnp.int32)[:, None]
    is_e = flat_e[None, :] == e_col
    counts = jnp.sum(is_e.astype(jnp.int32), axis=1)
    ends = jnp.cumsum(counts)
    offs = ends - counts
    seg_len = ((counts + al - 1) // al) * al
    seg_end = jnp.cumsum(seg_len)
    seg_start = seg_end - seg_len
    n_in = n + E * al + tm
    p = jnp.arange(n_in, dtype=jnp.int32)
    past = p[None, :] >= seg_end[:, None]
    rank = p - jnp.sum(jnp.where(past, (seg_len - counts)[:, None], 0), axis=0)
    end_of_p = ends[0] + jnp.sum(jnp.where(past[:-1], (ends[1:] - ends[:-1])[:, None], 0), axis=0)
    real = jnp.logical_and(rank < end_of_p, p < seg_end[-1])
    src = order[jnp.where(real, rank, 0)]
    row_tok = jnp.where(real, src // TOP_K, p % T)
    n_tiles_e = (counts + tm - 1) // tm
    tile_end = jnp.cumsum(n_tiles_e)
    tile_start = tile_end - n_tiles_e
    total = tile_end[-1]
    nv = n // tm + E
    v = jnp.arange(nv, dtype=jnp.int32)
    ve = jnp.minimum(jnp.sum((v[:, None] >= tile_end[None, :]).astype(jnp.int32), axis=1), E - 1)
    valid = v < total
    last = total - 1
    ve = jnp.where(valid, ve, ve[last]).astype(jnp.int32)
    vout = jnp.where(valid, v, last).astype(jnp.int32)
    start = (seg_start[ve] + (vout - tile_start[ve]) * tm).astype(jnp.int32)
    pos = inv + jnp.sum(jnp.where(is_e, (tile_start * tm - offs)[:, None], 0), axis=0)
    return row_tok, pos.reshape(T, TOP_K), (start, ve, vout, valid.astype(jnp.int32))


def _final_kernel(h_ref, w_ref, *refs, tile0, n_prev, tm, n_lat_rows, L, B, D, alpha, with_next):
    y_refs, rest = refs[:TOP_K], refs[TOP_K:]
    r = _cond_row(tile0 + pl.program_id(0), tm, n_lat_rows, L, B)
    gate = rest[0][pl.ds(r, 1), 5 * D:6 * D]
    topw = w_ref[...]
    f = None
    for k, y_ref in enumerate(y_refs):
        term = topw[:, k:k + 1] * y_ref[...].astype(F32)
        f = term if f is None else f + term
    u = alpha * h_ref[...] + gate * f
    h2 = _layer_norm_rows(u) * rest[1][...] + rest[2][...]
    outs = rest[3 + (1 if with_next else 0) + n_prev:]
    if with_next:
        modn_ref = rest[3]
        outs[1][...] = (_layer_norm_rows(h2) * (1.0 + modn_ref[pl.ds(r, 1), D:2 * D])
                        + modn_ref[pl.ds(r, 1), 0:D]).astype(outs[1].dtype)
    outs[0][...] = h2


def _final_call(h1, topw, y_part, part, n_parts, prev, mod_l, ln_g, ln_b, B, L, alpha, mod_next=None, tm=256):
    n, D = h1.shape
    ntp = n // tm // n_parts
    tile0 = part * ntp
    with_next = mod_next is not None
    n_prev = len(prev)
    kern = functools.partial(_final_kernel, tile0=tile0, n_prev=n_prev, tm=tm, n_lat_rows=B * L, L=L, B=B,
                             D=D, alpha=alpha, with_next=with_next)
    row = lambda i: (tile0 + i, 0)
    fixed = lambda i: (0, 0)
    y_specs = [pl.BlockSpec((tm, D), functools.partial(lambda i, k: (k * ntp + i, 0), k=k)) for k in range(TOP_K)]
    mod_spec = pl.BlockSpec(mod_l.shape, fixed)
    row_spec = pl.BlockSpec((tm, D), row)
    in_specs = ([row_spec, pl.BlockSpec((tm, topw.shape[1]), row)] + y_specs
                + [mod_spec, pl.BlockSpec((1, D), fixed), pl.BlockSpec((1, D), fixed)]
                + ([mod_spec] if with_next else []))
    operands = [h1, topw] + [y_part] * TOP_K + [mod_l, ln_g.reshape(1, D), ln_b.reshape(1, D)] + (
        [mod_next] if with_next else [])
    aliases = {len(operands) + j: j for j in range(n_prev)}
    out_shape = [jax.ShapeDtypeStruct((n, D), F32)] + ([jax.ShapeDtypeStruct((n, D), BF16)] if with_next else [])
    assert n_prev in (0, len(out_shape)) and (n // tm) % n_parts == 0
    return pl.pallas_call(
        kern,
        grid=(ntp,),
        in_specs=in_specs + [pl.BlockSpec(memory_space=pl.ANY)] * n_prev,
        out_specs=[row_spec] * len(out_shape),
        out_shape=out_shape,
        input_output_aliases=aliases,
        compiler_params=_params(("parallel",)),
        name="final_ln",
    )(*operands, *prev)


def kernel(x, c, ctx, c_ctx, w_mod, b_mod, w_in, pool_w, pool_scale, na_rpb, hg_lb, hg_norm_g, w_out, ln1_g, ln1_b, ln2_g, ln2_b, router_w, router_b, exp_w1, exp_b1, exp_w2, exp_b2):
    B, L, D = x.shape
    Lc = ctx.shape[1]
    depth = w_in.shape[0]
    alpha = (2 * depth) ** 0.25
    pool_width = pool_w.shape[1] * pool_w.shape[2]
    na_width = na_rpb.shape[1] * HEAD_DIM
    hg_width = hg_norm_g.shape[1]
    na_heads, hg_heads = na_width // HEAD_DIM, hg_width // HEAD_DIM
    c_naq = pool_width // HEAD_DIM
    c_nak, c_nav = c_naq + na_heads, c_naq + 2 * na_heads
    c_hq = c_naq + 3 * na_heads
    c_hff, c_hfb, c_hi, c_hg = (c_hq + k * hg_heads for k in range(1, 5))
    assert B + 1 <= COND_ROWS

    h_parts = (x.reshape(B * L, D), ctx.reshape(B * Lc, D))
    cond =jnp.concatenate([c, c_ctx[None], jnp.zeros((COND_ROWS - B - 1, D), F32)], axis=0)
    mod = _adaln_call(cond, w_mod, b_mod)
    lb_soft = jax.nn.softmax(hg_lb.astype(F32), axis=1)
    lower = jnp.cumsum(lb_soft, axis=1) - lb_soft[:, :1]
    rw_pad = jnp.pad(router_w, ((0, 0), (0, 0), (0, HEAD_DIM - N_EXPERTS)))
    rb_pad = jnp.pad(router_b, ((0, 0), (0, HEAD_DIM - N_EXPERTS)))

    T = B * (L + Lc)
    n_row_tiles = 8
    COMBINE_PARTS = 2
    a_in = _ln_mod_call(h_parts, mod[0], B, L)
    for l in range(depth):
        last = l == depth - 1
        px = _wstat_matmul_call([a_in], w_in, l, T, T // n_row_tiles, 512, "in_proj")
        a_mix = _pool_call(px, pool_w[l].astype(BF16), pool_scale[l], B, L, Lc)
        b_x, b_c = _na_call(px, na_rpb[l], B, L, Lc, c_naq, c_nak, c_nav, na_heads)
        b_mix = jnp.concatenate([b_x, b_c], axis=0)
        o_f, o_b = _hgrn_call(px, lower[0, l].reshape(hg_heads, 1, HEAD_DIM),
                              lower[1, l].reshape(hg_heads, 1, HEAD_DIM), B, L, Lc,
                              c_hq, c_hff, c_hfb, c_hi, hg_heads)
        c_mix = _readout_call(o_f, o_b, px, hg_norm_g[l], c_hg, T // n_row_tiles)
        n_out = B * L if last else T
        y = _wstat_matmul_call([a_mix, b_mix, c_mix], w_out, l, n_out, n_out // n_row_tiles, 512, "out_proj")
        h1, fin, topw, topi = _post_mix_call(y, h_parts, mod[l], ln1_g[l], ln1_b[l], rw_pad[l],
                                              rb_pad[l].reshape(1, -1), B, L, alpha)
        row_tok, pos, visits = _moe_dispatch(topi[:, :TOP_K], MOE_TILE)
        xs = fin.at[row_tok].get(mode="promise_in_bounds")
        ys = _moe_call(xs, visits, exp_w1, exp_b1, exp_w2, exp_b2, l)
        outs = ()
        n_part = n_out // COMBINE_PARTS
        for part in range(COMBINE_PARTS):
            pos_part = pos[part * n_part:(part + 1) * n_part].T.reshape(-1)
            y_part = ys.at[pos_part].get(mode="promise_in_bounds")
            outs = tuple(_final_call(h1, topw, y_part, part, COMBINE_PARTS, outs, mod[l], ln2_g[l], ln2_b[l],
                                     B, L, alpha, mod_next=None if last else mod[l + 1]))
        hall = outs[0]
        if not last:
            a_in = outs[1]
        h_parts = (hall,)
    return hall.reshape(B, L, D)
```

```python
import functools

import jax
import jax.numpy as jnp
import numpy as np
from jax import lax
from jax.experimental import pallas as pl
from jax.experimental.pallas import tpu as pltpu

F32 = jnp.float32
BF16 = jnp.bfloat16

HEAD_DIM = 128
GRID_W = 64
POOL_WINDOWS = (2, 4, 8, 16)
NA_ROWS = 8
NA_COLS = 16
FORGET_EPS = 1e-20
N_EXPERTS = 32
TOP_K = 4
SWIGLU_LIMIT = 7.0
SWIGLU_ALPHA = 1.702
ROPE_BASE = 10000.0
LN_EPS = 1e-6

VMEM_LIMIT_BYTES = 56 * 1024 * 1024
HG_CHUNK = 256
NEG_BIG = -1e30
ROW_TILE = 256
MM_ROW_TILES = 8
MM_COL_TILE = 512


def _params(sem):
    return pltpu.CompilerParams(dimension_semantics=sem, vmem_limit_bytes=VMEM_LIMIT_BYTES)


def _dot(a, b):
    return jnp.dot(a, b, preferred_element_type=F32)


def _dot_nt(a, b):
    return lax.dot_general(a, b, (((1,), (1,)), ((), ())), preferred_element_type=F32)


def _sigmoid(x):
    return 1.0 / (1.0 + jnp.exp(-x))


LOG2E = 1.4426950408889634
HG_HALF = HG_CHUNK // 2
HG_HEADS_PER_STEP = 4


def _neg_abs(x):
    bits = lax.bitcast_convert_type(x, jnp.uint32) | jnp.uint32(0x80000000)
    return lax.bitcast_convert_type(bits, F32)


def _hgrn_half_scores(q, k, b2, g2, fc, lev, reverse):
    n = q.shape[0]
    row = lax.broadcasted_iota(jnp.int32, (n, HEAD_DIM), 0)
    a = jnp.where(lev == -1, _dot_nt(q.astype(BF16), k.astype(BF16)), 0.0)
    h, lvl = 1, 0
    while h < n:
        if h < 8:
            upper = (row & h) != 0
            t_role = jnp.logical_not(upper) if reverse else upper
            if h == 1:
                x = jnp.where(t_role, q * fc, k)
            elif h == 2:
                r4 = row & 3
                g_prev, g_next = pltpu.roll(g2, 1, 0), pltpu.roll(g2, n - 1, 0)
                if reverse:
                    e = jnp.where(r4 == 0, g2 + g_next, jnp.where(r4 == 1, g2, jnp.where(r4 == 2, 0.0, g_prev)))
                else:
                    e = jnp.where(r4 == 0, g_next, jnp.where(r4 == 1, 0.0, jnp.where(r4 == 2, g2, g2 + g_prev)))
                x = jnp.where(t_role, q, k) * jnp.exp2(e)
            else:
                m0 = h if reverse else h - 1
                b3 = b2.reshape(n // (2 * h), 2 * h, HEAD_DIM)
                ref = jnp.broadcast_to(b3[:, m0:m0 + 1, :], b3.shape).reshape(n, HEAD_DIM)
                x = jnp.where(t_role, q, k) * jnp.exp2(_neg_abs(b2 - ref))
            x = x.astype(BF16)
            a = jnp.where(lev == lvl, _dot_nt(x, x), a)
        else:
            nb = n // (2 * h)
            q4 = q.reshape(nb, 2, h, HEAD_DIM)
            k4 = k.reshape(nb, 2, h, HEAD_DIM)
            b4 = b2.reshape(nb, 2, h, HEAD_DIM)
            s_half, t_half, m_row = (1, 0, 0) if reverse else (0, 1, h - 1)
            ref = b4[:, s_half:s_half + 1, m_row:m_row + 1, :]
            xt = q4[:, t_half] * jnp.exp2(b4[:, t_half] - ref[:, 0])
            xs = k4[:, s_half] * jnp.exp2(ref[:, 0] - b4[:, s_half])
            halves = [xt, xs] if reverse else [xs, xt]
            x = jnp.stack(halves, axis=1).reshape(n, HEAD_DIM).astype(BF16)
            g = _dot_nt(xt.reshape(n // 2, HEAD_DIM).astype(BF16), x)
            a4 = a.reshape(nb, 2, h, n)
            lev_t = lev.reshape(nb, 2, h, n)[:, t_half]
            a_t = jnp.where(lev_t == lvl, g.reshape(nb, h, n), a4[:, t_half])
            halves = [a_t, a4[:, s_half]] if reverse else [a4[:, s_half], a_t]
            a = jnp.stack(halves, axis=1).reshape(n, n)
        h *= 2
        lvl += 1
    return a


def _hgrn_direction(q, z, v_bf, lb, st_ref, lev, tri_bf, reverse):
    C, H = HG_CHUNK, HG_HALF
    q = q * _sigmoid(q)
    sig = _sigmoid(z)
    fc = jnp.maximum(lb + (1.0 - lb) * sig, FORGET_EPS)
    g2 = jnp.log(fc) * LOG2E
    k = (1.0 - lb) * (1.0 - sig)

    g_hi = g2.astype(BF16)
    r1 = g2 - g_hi.astype(F32)
    g_mid = r1.astype(BF16)
    g_lo = (r1 - g_mid.astype(F32)).astype(BF16)
    b2 = _dot(tri_bf, g_hi) + _dot(tri_bf, g_mid) + _dot(tri_bf, g_lo)

    lo, hi = slice(0, H), slice(H, C)
    a_lo = _hgrn_half_scores(q[lo], k[lo], b2[lo], g2[lo], fc[lo], lev, reverse).astype(BF16)
    a_hi = _hgrn_half_scores(q[hi], k[hi], b2[hi], g2[hi], fc[hi], lev, reverse).astype(BF16)
    t_sl, s_sl, m = (lo, hi, H) if reverse else (hi, lo, H - 1)
    ref = b2[m:m + 1, :]
    qt = (q[t_sl] * jnp.exp2(b2[t_sl] - ref)).astype(BF16)
    ks = (k[s_sl] * jnp.exp2(ref - b2[s_sl])).astype(BF16)
    cross = _dot(_dot_nt(qt, ks).astype(BF16), v_bf[s_sl])
    o_lo = _dot(a_lo, v_bf[lo])
    o_hi = _dot(a_hi, v_bf[hi])
    o_intra = jnp.concatenate([o_lo + cross, o_hi] if reverse else [o_lo, o_hi + cross], axis=0)

    st = st_ref[...]
    tot = b2[0:1, :] if reverse else b2[C - 1:C, :]
    qdec = (q * jnp.exp2(b2)).astype(BF16)
    o_inter = _dot_nt(qdec, st.astype(BF16))
    kdec = (k * jnp.exp2(tot - b2)).astype(BF16)
    vt = v_bf.astype(F32).T.astype(BF16)
    st_ref[...] = jnp.exp2(tot) * st + _dot(vt, kdec)
    return o_intra + o_inter


def _hgrn_kernel(qf_ref, zf_ref, vf_ref, qb_ref, zb_ref, vb_ref, lbf_ref, lbb_ref, levf_ref, levb_ref,
                 trif_ref, trib_ref, of_ref, ob_ref, sf_ref, sb_ref):
    @pl.when(pl.program_id(2) == 0)
    def _():
        sf_ref[...] = jnp.zeros_like(sf_ref)
        sb_ref[...] = jnp.zeros_like(sb_ref)

    for hh in range(HG_HEADS_PER_STEP):
        sl = slice(hh * HEAD_DIM, (hh + 1) * HEAD_DIM)
        of_ref[:, sl] = _hgrn_direction(qf_ref[:, sl].astype(F32), zf_ref[:, sl].astype(F32), vf_ref[:, sl],
                                        lbf_ref[0][:, sl], sf_ref.at[hh], levf_ref[...], trif_ref[...],
                                        False).astype(of_ref.dtype)
        ob_ref[:, sl] = _hgrn_direction(qb_ref[:, sl].astype(F32), zb_ref[:, sl].astype(F32), vb_ref[:, sl],
                                        lbb_ref[0][:, sl], sb_ref.at[hh], levb_ref[...], trib_ref[...],
                                        True).astype(ob_ref.dtype)


def _hgrn_call(px, lb_f, lb_b, B, L, Lc, col_q, col_ff, col_fb, col_i, n_heads):
    C, H = HG_CHUNK, HG_HALF
    T = px.shape[0]
    nlc, ncc = L // C, Lc // C
    ns = nlc + ncc
    ctx0 = B * nlc

    def fwd_row(b, s):
        return jnp.where(s < ncc, ctx0 + b * ncc + s, b * nlc + s - ncc)

    def bwd_row(b, s):
        return jnp.where(s < ncc, ctx0 + b * ncc + (ncc - 1 - s), b * nlc + (nlc - 1 - (s - ncc)))

    G = HG_HEADS_PER_STEP
    GW = G * HEAD_DIM
    assert n_heads % G == 0 and all(c % G == 0 for c in (col_q, col_ff, col_fb, col_i))

    def spec(row_fn, col):
        return pl.BlockSpec((C, GW), lambda b, h, s: (row_fn(b, s), col // G + h))

    idx = np.arange(H)
    xr = idx[:, None] ^ idx[None, :]
    level = np.floor(np.log2(np.maximum(xr, 1))).astype(np.int32)
    lev_f = np.where(xr == 0, -1, np.where(idx[:, None] > idx[None, :], level, -2)).astype(np.int32)
    lev_b = np.ascontiguousarray(lev_f.T)
    idc = np.arange(C)
    tri_f = (idc[:, None] >= idc[None, :]).astype(np.float32)
    tri_b = (idc[:, None] <= idc[None, :]).astype(np.float32)
    const = lambda shape: pl.BlockSpec(shape, lambda b, h, s: (0,) * len(shape))
    lb_spec = pl.BlockSpec((1, 1, GW), lambda b, h, s: (h, 0, 0))
    out_sds = jax.ShapeDtypeStruct((T, n_heads * HEAD_DIM), BF16)
    state = pltpu.VMEM((G, HEAD_DIM, HEAD_DIM), F32)
    return pl.pallas_call(
        _hgrn_kernel,
        grid=(B, n_heads // G, ns),
        in_specs=[spec(fwd_row, col_q), spec(fwd_row, col_ff), spec(fwd_row, col_i),
                  spec(bwd_row, col_q), spec(bwd_row, col_fb), spec(bwd_row, col_i),
                  lb_spec, lb_spec, const((H, H)), const((H, H)), const((C, C)), const((C, C))],
        out_specs=[pl.BlockSpec((C, GW), lambda b, h, s: (fwd_row(b, s), h)),
                   pl.BlockSpec((C, GW), lambda b, h, s: (bwd_row(b, s), h))],
        out_shape=[out_sds, out_sds],
        scratch_shapes=[state, state],
        compiler_params=_params(("parallel", "parallel", "arbitrary")),
        name="hgrn_scan",
    )(px, px, px, px, px, px, lb_f.reshape(n_heads // G, 1, GW), lb_b.reshape(n_heads // G, 1, GW),
      jnp.asarray(lev_f), jnp.asarray(lev_b), jnp.asarray(tri_f, BF16), jnp.asarray(tri_b, BF16))


NA_HEADS_PER_STEP = 2
NA_QROWS = 4
NA_KROWS = NA_QROWS + NA_ROWS


def _na_block_cases(rows):
    assert rows % NA_QROWS == 0 and rows >= 2 * NA_ROWS
    nb = rows // NA_QROWS
    cases = []
    for i in (0, 1, nb - 1):
        r0 = NA_QROWS * i
        ks = min(max(r0 - NA_ROWS // 2, 0), rows - NA_KROWS)
        per_q = []
        for qr in range(NA_QROWS):
            r = r0 + qr
            rs = min(max(r - NA_ROWS // 2, 0), rows - NA_ROWS)
            per_q.append((rs - ks, ks - r + NA_ROWS - 1))
        cases.append(per_q)
    return cases


def _na_kernel(rpb_ref, q_ref, k_ref, v_ref, qc_ref, kc_ref, vc_ref, cos_ref, sin_ref,
               ox_ref, oc_ref, qr_ref, kr_ref, tb_ref, bias_ref, *, rows):
    G = NA_HEADS_PER_STEP
    h0 = pl.program_id(0) * G
    hs = [slice(g * HEAD_DIM, (g + 1) * HEAD_DIM) for g in range(G)]
    L = q_ref.shape[0]
    W = GRID_W
    scale = HEAD_DIM ** -0.5
    n_dr = 2 * NA_ROWS - 1
    n_dc = 2 * NA_COLS - 1

    @pl.when(pl.program_id(1) == 0)
    def _():
        c_i = lax.broadcasted_iota(jnp.int32, (W, W), 0)
        kc_i = lax.broadcasted_iota(jnp.int32, (W, W), 1)
        cs = jnp.clip(c_i - NA_COLS // 2, 0, W - NA_COLS)
        in_win = (kc_i >= cs) & (kc_i < cs + NA_COLS)
        c_off = kc_i - c_i + (NA_COLS - 1)
        neg = jnp.full((W, W), NEG_BIG, F32)
        for g in range(G):
            for dr in range(n_dr):
                acc = jnp.zeros((W, W), F32)
                for j in range(n_dc):
                    acc = jnp.where(c_off == j, rpb_ref[h0 + g, dr, j], acc)
                tb_ref[g, dr] = jnp.where(in_win, acc, NEG_BIG)
            for ci, per_q in enumerate(_na_block_cases(rows)):
                for qr, (j0, dr0) in enumerate(per_q):
                    for j in range(NA_KROWS):
                        valid = j0 <= j < j0 + NA_ROWS
                        blk = tb_ref[g, dr0 + j] if valid else neg
                        bias_ref[g, ci, qr * W:(qr + 1) * W, j * W:(j + 1) * W] = blk

    RC = 256
    lane = lax.broadcasted_iota(jnp.int32, (RC, HEAD_DIM), 1)
    first = (lane & (HEAD_DIM // 2 - 1)) < HEAD_DIM // 4

    def rope_chunk(i, carry):
        sl = pl.ds(pl.multiple_of(i * RC, RC), RC)
        cos = cos_ref[sl, :]
        sin = sin_ref[sl, :]
        for g in range(G):
            for src, dst, mul in ((q_ref, qr_ref, scale), (k_ref, kr_ref, 1.0)):
                t = src[sl, hs[g]].astype(F32)
                partner = jnp.where(first, pltpu.roll(t, HEAD_DIM - HEAD_DIM // 4, 1),
                                    pltpu.roll(t, HEAD_DIM // 4, 1))
                dst[g, sl, :] = ((t * cos + partner * sin) * mul).astype(BF16)
        return carry

    lax.fori_loop(0, L // RC, rope_chunk, 0)

    nb = rows // NA_QROWS
    QB = NA_QROWS * W
    KB = NA_KROWS * W

    def block(i, carry):
        ks = jnp.clip(i * NA_QROWS - NA_ROWS // 2, 0, rows - NA_KROWS)
        case = jnp.where(i == 0, 0, jnp.where(i == nb - 1, 2, 1))
        qs = pl.ds(pl.multiple_of(i * QB, QB), QB)
        kslice = pl.ds(pl.multiple_of(ks * W, W), KB)
        for g in range(G):
            kc, vc = kc_ref[:, hs[g]], vc_ref[:, hs[g]]
            qb = qr_ref[g, qs, :]
            s_loc = _dot_nt(qb, kr_ref[g, kslice, :]) + bias_ref[g, case]
            s_ctx = _dot_nt(qb, kc)
            m = jnp.maximum(jnp.max(s_loc, -1, keepdims=True), jnp.max(s_ctx, -1, keepdims=True))
            p_loc = jnp.exp(s_loc - m)
            p_ctx = jnp.exp(s_ctx - m)
            den = jnp.sum(p_loc, -1, keepdims=True) + jnp.sum(p_ctx, -1, keepdims=True)
            o = _dot(p_loc.astype(BF16), v_ref[kslice, hs[g]]) + _dot(p_ctx.astype(BF16), vc)
            ox_ref[qs, hs[g]] = (o / den).astype(ox_ref.dtype)
        return carry

    lax.fori_loop(0, nb, block, 0, unroll=2)

    for g in range(G):
        kc, vc = kc_ref[:, hs[g]], vc_ref[:, hs[g]]
        qc = (qc_ref[:, hs[g]].astype(F32) * scale).astype(BF16)
        s = _dot_nt(qc, kc)
        p = jnp.exp(s - jnp.max(s, -1, keepdims=True))
        oc = _dot(p.astype(BF16), vc) / jnp.sum(p, -1, keepdims=True)
        oc_ref[:, hs[g]] = oc.astype(oc_ref.dtype)


def _rope_tables(L):
    pos = jnp.arange(L)
    half = HEAD_DIM // 2
    inv = ROPE_BASE ** (-jnp.arange(0, half, 2, dtype=F32) / half)

    def tab(p):
        ang = p.astype(F32)[:, None] * inv[None]
        cos, sin = jnp.cos(ang), jnp.sin(ang)
        return jnp.concatenate([cos, cos], -1), jnp.concatenate([-sin, sin], -1)

    c_r, s_r = tab(pos // GRID_W)
    c_c, s_c = tab(pos % GRID_W)
    return jnp.concatenate([c_r, c_c], -1), jnp.concatenate([s_r, s_c], -1)


def _na_call(px, rpb, B, L, Lc, col_q, col_k, col_v, n_heads):
    rows = L // GRID_W
    cos, sin = _rope_tables(L)
    ctx0 = B * L // Lc

    G = NA_HEADS_PER_STEP
    GW = G * HEAD_DIM
    assert n_heads % G == 0 and all(c % G == 0 for c in (col_q, col_k, col_v))

    def lat(col):
        return pl.BlockSpec((L, GW), lambda h, b: (b, col // G + h))

    def cx(col):
        return pl.BlockSpec((Lc, GW), lambda h, b: (ctx0 + b, col // G + h))

    tab = pl.BlockSpec((L, HEAD_DIM), lambda h, b: (0, 0))
    W = GRID_W
    return pl.pallas_call(
        functools.partial(_na_kernel, rows=rows),
        grid=(n_heads // G, B),
        in_specs=[pl.BlockSpec(memory_space=pltpu.SMEM), lat(col_q), lat(col_k), lat(col_v),
                  cx(col_q), cx(col_k), cx(col_v), tab, tab],
        out_specs=[pl.BlockSpec((L, GW), lambda h, b: (b, h)),
                   pl.BlockSpec((Lc, GW), lambda h, b: (b, h))],
        out_shape=[jax.ShapeDtypeStruct((B * L, n_heads * HEAD_DIM), BF16),
                   jax.ShapeDtypeStruct((B * Lc, n_heads * HEAD_DIM), BF16)],
        scratch_shapes=[pltpu.VMEM((G, L, HEAD_DIM), BF16), pltpu.VMEM((G, L, HEAD_DIM), BF16),
                        pltpu.VMEM((G, 2 * NA_ROWS - 1, W, W), F32),
                        pltpu.VMEM((G, 3, NA_QROWS * W, NA_KROWS * W), F32)],
        compiler_params=_params(("parallel", "arbitrary")),
        name="neighborhood_attention",
    )(rpb, px, px, px, px, px, px, cos, sin)


POOL_TILE = 256
POOL_HALO = 16


def _pool_kernel(prev_ref, cur_ref, next_ref, w_ref, scale_ref, o_ref, *, n_lat_tiles, nl, nc):
    i = pl.program_id(0)
    is_lat = i < n_lat_tiles
    j = jnp.where(is_lat, i % nl, (i - n_lat_tiles) % nc)
    n = jnp.where(is_lat, nl, nc)
    TQ, HL = POOL_TILE, POOL_HALO
    lo_min = jnp.where(j > 0, -HL, 0)
    hi_max = jnp.where(j < n - 1, TQ + HL, TQ)
    src = jnp.concatenate([prev_ref[...], cur_ref[...], next_ref[...]], axis=0)
    t = lax.broadcasted_iota(jnp.int32, (TQ, TQ + 2 * HL), 0)
    s = lax.broadcasted_iota(jnp.int32, (TQ, TQ + 2 * HL), 1) - HL
    Dg = w_ref.shape[1]
    t1 = lax.broadcasted_iota(jnp.int32, (TQ, Dg), 0)
    for g, w in enumerate(POOL_WINDOWS):
        lo = jnp.maximum(t - w // 2, lo_min)
        hi = jnp.minimum(t + (w - w // 2), hi_max)
        band = jnp.where(s >= lo, jnp.where(s < hi, 1.0, 0.0), 0.0).astype(BF16)
        cnt = (jnp.minimum(t1 + (w - w // 2), hi_max) - jnp.maximum(t1 - w // 2, lo_min)).astype(F32)
        cols = slice(g * Dg, (g + 1) * Dg)
        win = _dot(band, src[:, cols])
        d = win / cnt - cur_ref[:, cols].astype(F32)
        y = _dot(d.astype(BF16), w_ref[g]) * scale_ref[:, cols]
        o_ref[:, cols] = y.astype(o_ref.dtype)


def _pool_call(px, pool_w_bf, pool_scale, B, L, Lc):
    T = px.shape[0]
    G, Dg, _ = pool_w_bf.shape
    PW = G * Dg
    TQ, HL = POOL_TILE, POOL_HALO
    r = TQ // HL
    n_tiles = T // TQ
    n_halo = T // HL
    kern = functools.partial(_pool_kernel, n_lat_tiles=B * L // TQ, nl=L // TQ, nc=Lc // TQ)
    return pl.pallas_call(
        kern,
        grid=(n_tiles,),
        in_specs=[pl.BlockSpec((HL, PW), lambda i: (jnp.maximum(i * r - 1, 0), 0)),
                  pl.BlockSpec((TQ, PW), lambda i: (i, 0)),
                  pl.BlockSpec((HL, PW), lambda i: (jnp.minimum((i + 1) * r, n_halo - 1), 0)),
                  pl.BlockSpec((G, Dg, Dg), lambda i: (0, 0, 0)),
                  pl.BlockSpec((1, PW), lambda i: (0, 0))],
        out_specs=pl.BlockSpec((TQ, PW), lambda i: (i, 0)),
        out_shape=jax.ShapeDtypeStruct((T, PW), BF16),
        compiler_params=_params(("parallel",)),
        name="multiscale_pool",
    )(px, px, px, pool_w_bf, pool_scale.reshape(1, PW).astype(F32))


READOUT_HEADS = 4


def _readout_kernel(of_ref, ob_ref, gate_ref, ng_ref, o_ref):
    for hh in range(READOUT_HEADS):
        sl = slice(hh * HEAD_DIM, (hh + 1) * HEAD_DIM)
        o = of_ref[:, sl].astype(F32) + ob_ref[:, sl].astype(F32)
        o = o * lax.rsqrt(jnp.mean(o * o, -1, keepdims=True) + LN_EPS)
        g = gate_ref[:, sl].astype(F32)
        o_ref[:, sl] = (o * ng_ref[:, sl] * (g * _sigmoid(g))).astype(o_ref.dtype)


def _readout_call(o_f, o_b, px, norm_g, col_gate, tm):
    T, HW = o_f.shape
    GW = READOUT_HEADS * HEAD_DIM
    assert HW % GW == 0 and col_gate % READOUT_HEADS == 0
    blk = lambda col: pl.BlockSpec((tm, GW), lambda i, h: (i, col // READOUT_HEADS + h))
    return pl.pallas_call(
        _readout_kernel,
        grid=(T // tm, HW // GW),
        in_specs=[blk(0), blk(0), blk(col_gate), pl.BlockSpec((1, GW), lambda i, h: (0, h))],
        out_specs=blk(0),
        out_shape=jax.ShapeDtypeStruct((T, HW), BF16),
        compiler_params=_params(("parallel", "parallel")),
        name="hgrn_readout",
    )(o_f, o_b, px, norm_g.reshape(1, HW).astype(F32))


COND_ROWS = 16


def _adaln_kernel(cond_ref, w_ref, b_ref, o_ref):
    cnd = cond_ref[...]
    a = (cnd * _sigmoid(cnd)).astype(BF16)
    o_ref[0] = _dot(a, w_ref[0].astype(BF16)) + b_ref[0]


def _adaln_call(cond, w_mod, b_mod, tn=1024):
    depth, D, N = w_mod.shape
    return pl.pallas_call(
        _adaln_kernel,
        grid=(depth, N // tn),
        in_specs=[pl.BlockSpec((COND_ROWS, D), lambda l, j: (0, 0)),
                  pl.BlockSpec((1, D, tn), lambda l, j: (l, 0, j)),
                  pl.BlockSpec((1, 1, tn), lambda l, j: (l, 0, j))],
        out_specs=pl.BlockSpec((1, COND_ROWS, tn), lambda l, j: (l, 0, j)),
        out_shape=jax.ShapeDtypeStruct((depth, COND_ROWS, N), F32),
        compiler_params=_params(("parallel", "parallel")),
        name="adaln",
    )(cond, w_mod, b_mod.reshape(depth, 1, N))


def _cond_row(i, tm, n_lat_rows, L, B):
    return jnp.where(i * tm < n_lat_rows, (i * tm) // L, B)


def _layer_norm_rows(x):
    mu = jnp.mean(x, -1, keepdims=True)
    xc = x - mu
    var = jnp.mean(xc * xc, -1, keepdims=True)
    return xc * lax.rsqrt(var + LN_EPS)


def _row_part_specs(parts, tm):
    specs, firsts, lo = [], [], 0
    for part in parts:
        n_t = part.shape[0] // tm
        assert part.shape[0] % tm == 0
        specs.append(pl.BlockSpec((tm, part.shape[1]),
                                  functools.partial(lambda i, lo, n_t: (jnp.clip(i - lo, 0, n_t - 1), 0), lo=lo, n_t=n_t)))
        firsts.append(lo)
        lo += n_t
    return specs, tuple(firsts)


def _read_row_parts(refs, firsts, i):
    val = refs[0][...]
    for ref, lo in zip(refs[1:], firsts[1:]):
        val = jnp.where(i >= lo, ref[...], val)
    return val


def _ln_mod_kernel(*refs, firsts, tm, n_lat_rows, L, B, D):
    n = len(firsts)
    mod_ref, o_ref = refs[n], refs[n + 1]
    i = pl.program_id(0)
    r = _cond_row(i, tm, n_lat_rows, L, B)
    shift = mod_ref[pl.ds(r, 1), 0:D]
    scale = mod_ref[pl.ds(r, 1), D:2 * D]
    x = _read_row_parts(refs[:n], firsts, i)
    o_ref[...] = (_layer_norm_rows(x) * (1.0 + scale) + shift).astype(o_ref.dtype)


def _ln_mod_call(h_parts, mod_l, B, L, tm=ROW_TILE):
    T = sum(p.shape[0] for p in h_parts)
    D = h_parts[0].shape[1]
    specs, firsts = _row_part_specs(h_parts, tm)
    kern = functools.partial(_ln_mod_kernel, firsts=firsts, tm=tm, n_lat_rows=B * L, L=L, B=B, D=D)
    return pl.pallas_call(
        kern,
        grid=(T // tm,),
        in_specs=specs + [pl.BlockSpec(mod_l.shape, lambda i: (0, 0))],
        out_specs=pl.BlockSpec((tm, D), lambda i: (i, 0)),
        out_shape=jax.ShapeDtypeStruct((T, D), BF16),
        compiler_params=_params(("parallel",)),
        name="ln_modulate",
    )(*h_parts, mod_l)


def _wstat_matmul_kernel(*refs, widths):
    n = len(widths)
    a_refs, w_ref, o_ref, wbf_ref = refs[:n], refs[n], refs[n + 1], refs[n + 2]

    @pl.when(pl.program_id(1) == 0)
    def _():
        wbf_ref[...] = w_ref[0].astype(BF16)

    acc = None
    k0 = 0
    for a_ref, kw in zip(a_refs, widths):
        part = _dot(a_ref[...], wbf_ref[k0:k0 + kw, :])
        acc = part if acc is None else acc + part
        k0 += kw
    o_ref[...] = acc.astype(o_ref.dtype)


def _wstat_matmul_call(a_list, w_all, layer, n_rows, tm, tn, name):
    widths = tuple(a.shape[1] for a in a_list)
    _, K, N = w_all.shape
    assert sum(widths) == K and n_rows % tm == 0 and N % tn == 0
    return pl.pallas_call(
        functools.partial(_wstat_matmul_kernel, widths=widths),
        grid=(N // tn, n_rows // tm),
        in_specs=[pl.BlockSpec((tm, kw), lambda j, i: (i, 0)) for kw in widths]
        + [pl.BlockSpec((1, K, tn), lambda j, i: (layer, 0, j))],
        out_specs=pl.BlockSpec((tm, tn), lambda j, i: (i, j)),
        out_shape=jax.ShapeDtypeStruct((n_rows, N), BF16),
        scratch_shapes=[pltpu.VMEM((K, tn), BF16)],
        compiler_params=_params(("parallel", "arbitrary")),
        name=name,
    )(*a_list, w_all)


def _split_bf16(x):
    hi = x.astype(BF16)
    return hi, (x - hi.astype(F32)).astype(BF16)


def _post_mix_kernel(y_ref, *refs, firsts, tm, n_lat_rows, L, B, D, alpha):
    n = len(firsts)
    mod_ref, g_ref, b_ref, rw_ref, rb_ref, h1_ref, fin_ref, gates_ref, topi_ref = refs[n:]
    i = pl.program_id(0)
    r = _cond_row(i, tm, n_lat_rows, L, B)
    mrow = lambda slot: mod_ref[pl.ds(r, 1), slot * D:(slot + 1) * D]
    u = alpha * _read_row_parts(refs[:n], firsts, i) + mrow(2) * y_ref[...].astype(F32)
    h1 = _layer_norm_rows(u) * g_ref[...] + b_ref[...]
    h1_ref[...] = h1
    fin = _layer_norm_rows(h1) * (1.0 + mrow(4)) + mrow(3)
    fin_ref[...] = fin.astype(BF16)
    f_hi, f_lo = _split_bf16(fin)
    w_hi, w_lo = _split_bf16(rw_ref[...])
    logits = _dot(f_hi, w_hi) + _dot(f_hi, w_lo) + _dot(f_lo, w_hi) + rb_ref[...]
    lane = lax.broadcasted_iota(jnp.int32, logits.shape, 1)
    ninf = jnp.float32(-jnp.inf)
    xs = jnp.where(lane < N_EXPERTS, logits, ninf)
    sel_v, sel_i = [], []
    for _ in range(TOP_K):
        m = jnp.max(xs, -1, keepdims=True)
        idx = jnp.min(jnp.where(xs == m, lane, logits.shape[1]), -1, keepdims=True)
        sel_v.append(m)
        sel_i.append(idx)
        xs = jnp.where(lane == idx, ninf, xs)
    ex = [jnp.exp(v - sel_v[0]) for v in sel_v]
    den = ex[0]
    for e in ex[1:]:
        den = den + e
    topw = jnp.zeros(logits.shape, F32)
    topi = jnp.zeros(logits.shape, jnp.int32)
    for j in range(TOP_K):
        topw = jnp.where(lane == j, ex[j] / den, topw)
        topi = jnp.where(lane == j, sel_i[j], topi)
    gates_ref[...] = topw
    topi_ref[...] = topi


def _post_mix_call(y, h_parts, mod_l, ln_g, ln_b, rw_pad, rb_pad, B, L, alpha, tm=ROW_TILE):
    n_out, D = y.shape
    NE = rw_pad.shape[1]
    h_specs, firsts = _row_part_specs(h_parts, tm)
    kern = functools.partial(_post_mix_kernel, firsts=firsts, tm=tm, n_lat_rows=B * L, L=L, B=B, D=D, alpha=alpha)
    row = lambda i: (i, 0)
    fixed = lambda i: (0, 0)
    return pl.pallas_call(
        kern,
        grid=(n_out // tm,),
        in_specs=[pl.BlockSpec((tm, D), row)] + h_specs
        + [pl.BlockSpec(mod_l.shape, fixed),
                  pl.BlockSpec((1, D), fixed), pl.BlockSpec((1, D), fixed),
                  pl.BlockSpec((D, NE), fixed), pl.BlockSpec((1, NE), fixed)],
        out_specs=[pl.BlockSpec((tm, D), row), pl.BlockSpec((tm, D), row),
                   pl.BlockSpec((tm, NE), row), pl.BlockSpec((tm, NE), row)],
        out_shape=[jax.ShapeDtypeStruct((n_out, D), F32), jax.ShapeDtypeStruct((n_out, D), BF16),
                   jax.ShapeDtypeStruct((n_out, NE), F32), jax.ShapeDtypeStruct((n_out, NE), jnp.int32)],
        compiler_params=_params(("parallel",)),
        name="post_mix_router",
    )(y, *h_parts, mod_l, ln_g.reshape(1, D), ln_b.reshape(1, D), rw_pad, rb_pad)


MOE_TILE = 512
MOE_ALIGN = 16


def _first_visit_of_expert(v, ve_ref):
    return (v == 0) | (ve_ref[jnp.maximum(v - 1, 0)] != ve_ref[v])


def _moe_up_kernel(start_ref, ve_ref, vout_ref, vv_ref, x_ref, w1_ref, b1_ref, o_ref, wbf_ref, *, DE):
    v = pl.program_id(0)

    @pl.when(_first_visit_of_expert(v, ve_ref))
    def _():
        wbf_ref[...] = w1_ref[0, 0].astype(BF16)

    @pl.when(vv_ref[v] > 0)
    def _():
        hcat = _dot(x_ref[...], wbf_ref[...]) + b1_ref[0, 0]
        gate = jnp.minimum(hcat[:, :DE], SWIGLU_LIMIT)
        up = jnp.clip(hcat[:, DE:], -SWIGLU_LIMIT, SWIGLU_LIMIT)
        o_ref[...] = ((up + 1.0) * (gate * _sigmoid(SWIGLU_ALPHA * gate))).astype(o_ref.dtype)


def _moe_down_kernel(start_ref, ve_ref, vout_ref, vv_ref, h_ref, w2_ref, b2_ref, o_ref, wbf_ref):
    v = pl.program_id(0)

    @pl.when(_first_visit_of_expert(v, ve_ref))
    def _():
        wbf_ref[...] = w2_ref[0, 0].astype(BF16)

    @pl.when(vv_ref[v] > 0)
    def _():
        o_ref[...] = (_dot(h_ref[...], wbf_ref[...]) + b2_ref[0, 0]).astype(o_ref.dtype)


def _moe_call(xs, visits, w1_all, b1_all, w2_all, b2_all, layer):
    n_in, D = xs.shape
    _, E, _, DE2 = w1_all.shape
    DE = DE2 // 2
    tm = MOE_TILE
    nv = visits[0].shape[0]
    window = lambda v, st, ve, vo, vv: (pl.multiple_of(st[v], MOE_ALIGN), 0)
    out_tile = lambda v, st, ve, vo, vv: (vo[v], 0)
    expert = lambda v, st, ve, vo, vv: (layer, ve[v], 0, 0)
    hdn = pl.pallas_call(
        functools.partial(_moe_up_kernel, DE=DE),
        grid_spec=pltpu.PrefetchScalarGridSpec(
            num_scalar_prefetch=4, grid=(nv,),
            in_specs=[pl.BlockSpec((pl.Element(tm), pl.Element(D)), window),
                      pl.BlockSpec((1, 1, D, DE2), expert),
                      pl.BlockSpec((1, 1, 1, DE2), expert)],
            out_specs=pl.BlockSpec((tm, DE), out_tile),
            scratch_shapes=[pltpu.VMEM((D, DE2), BF16)]),
        out_shape=jax.ShapeDtypeStruct((nv * tm, DE), BF16),
        compiler_params=_params(("arbitrary",)),
        name="moe_up",
    )(*visits, xs, w1_all, b1_all.reshape(b1_all.shape[0], E, 1, DE2))
    return pl.pallas_call(
        _moe_down_kernel,
        grid_spec=pltpu.PrefetchScalarGridSpec(
            num_scalar_prefetch=4, grid=(nv,),
            in_specs=[pl.BlockSpec((tm, DE), out_tile),
                      pl.BlockSpec((1, 1, DE, D), expert),
                      pl.BlockSpec((1, 1, 1, D), expert)],
            out_specs=pl.BlockSpec((tm, D), out_tile),
            scratch_shapes=[pltpu.VMEM((DE, D), BF16)]),
        out_shape=jax.ShapeDtypeStruct((nv * tm, D), BF16),
        compiler_params=_params(("arbitrary",)),
        name="moe_down",
    )(*visits, hdn, w2_all, b2_all.reshape(b2_all.shape[0], E, 1, D))


def _moe_dispatch(topi, tm):
    T = topi.shape[0]
    E = N_EXPERTS
    n = T * TOP_K
    al = MOE_ALIGN
    flat_e = topi.reshape(-1)
    order = jnp.argsort(flat_e, stable=True).astype(jnp.int32)
    inv = jnp.argsort(order).astype(jnp.int32)
    e_col = jnp.arange(E, dtype=jnp.int32)[:, None]
    is_e = flat_e[None, :] == e_col
    counts = jnp.sum(is_e.astype(jnp.int32), axis=1)
    ends = jnp.cumsum(counts)
    offs = ends - counts
    seg_len = ((counts + al - 1) // al) * al
    seg_end = jnp.cumsum(seg_len)
    seg_start = seg_end - seg_len
    n_in = n + E * al + tm
    p = jnp.arange(n_in, dtype=jnp.int32)
    past = p[None, :] >= seg_end[:, None]
    rank = p - jnp.sum(jnp.where(past, (seg_len - counts)[:, None], 0), axis=0)
    end_of_p = ends[0] + jnp.sum(jnp.where(past[:-1], (ends[1:] - ends[:-1])[:, None], 0), axis=0)
    real = jnp.logical_and(rank < end_of_p, p < seg_end[-1])
    src = order[jnp.where(real, rank, 0)]
    row_tok = jnp.where(real, src // TOP_K, p % T)
    n_tiles_e = (counts + tm - 1) // tm
    tile_end = jnp.cumsum(n_tiles_e)
    tile_start = tile_end - n_tiles_e
    total = tile_end[-1]
    nv = n // tm + E
    v = jnp.arange(nv, dtype=jnp.int32)
    ve = jnp.minimum(jnp.sum((v[:, None] >= tile_end[None, :]).astype(jnp.int32), axis=1), E - 1)
    valid = v < total
    last = total - 1
    ve = jnp.where(valid, ve, ve[last]).astype(jnp.int32)
    vout = jnp.where(valid, v, last).astype(jnp.int32)
    start = (seg_start[ve] + (vout - tile_start[ve]) * tm).astype(jnp.int32)
    pos = inv + jnp.sum(jnp.where(is_e, (tile_start * tm - offs)[:, None], 0), axis=0)
    return row_tok, pos.reshape(T, TOP_K), (start, ve, vout, valid.astype(jnp.int32))


def _final_kernel(h_ref, w_ref, *refs, tm, n_lat_rows, L, B, D, alpha, with_next):
    y_refs, rest = refs[:TOP_K], refs[TOP_K:]
    r = _cond_row(pl.program_id(0), tm, n_lat_rows, L, B)
    gate = rest[0][pl.ds(r, 1), 5 * D:6 * D]
    topw = w_ref[...]
    f = None
    for k, y_ref in enumerate(y_refs):
        term = topw[:, k:k + 1] * y_ref[...].astype(F32)
        f = term if f is None else f + term
    u = alpha * h_ref[...] + gate * f
    h2 = _layer_norm_rows(u) * rest[1][...] + rest[2][...]
    if with_next:
        modn_ref, o_ref, a_ref = rest[3:]
        a_ref[...] = (_layer_norm_rows(h2) * (1.0 + modn_ref[pl.ds(r, 1), D:2 * D])
                      + modn_ref[pl.ds(r, 1), 0:D]).astype(a_ref.dtype)
    else:
        o_ref = rest[3]
    o_ref[...] = h2


def _final_call(h1, topw, y_assign, mod_l, ln_g, ln_b, B, L, alpha, mod_next=None, tm=ROW_TILE):
    n, D = h1.shape
    nt = n // tm
    with_next = mod_next is not None
    kern = functools.partial(_final_kernel, tm=tm, n_lat_rows=B * L, L=L, B=B, D=D, alpha=alpha,
                             with_next=with_next)
    row = lambda i: (i, 0)
    fixed = lambda i: (0, 0)
    y_specs = [pl.BlockSpec((tm, D), functools.partial(lambda i, k: (k * nt + i, 0), k=k)) for k in range(TOP_K)]
    mod_spec = pl.BlockSpec(mod_l.shape, fixed)
    row_spec = pl.BlockSpec((tm, D), row)
    return pl.pallas_call(
        kern,
        grid=(nt,),
        in_specs=[row_spec, pl.BlockSpec((tm, topw.shape[1]), row)] + y_specs
        + [mod_spec, pl.BlockSpec((1, D), fixed), pl.BlockSpec((1, D), fixed)]
        + ([mod_spec] if with_next else []),
        out_specs=[row_spec, row_spec] if with_next else row_spec,
        out_shape=([jax.ShapeDtypeStruct((n, D), F32), jax.ShapeDtypeStruct((n, D), BF16)] if with_next
                   else jax.ShapeDtypeStruct((n, D), F32)),
        compiler_params=_params(("parallel",)),
        name="final_ln",
    )(h1, topw, *([y_assign] * TOP_K), mod_l, ln_g.reshape(1, D), ln_b.reshape(1, D),
      *([mod_next] if with_next else []))


def kernel(x, c, ctx, c_ctx, w_mod, b_mod, w_in, pool_w, pool_scale, na_rpb, hg_lb, hg_norm_g, w_out, ln1_g, ln1_b, ln2_g, ln2_b, router_w, router_b, exp_w1, exp_b1, exp_w2, exp_b2):
    B, L, D = x.shape
    Lc = ctx.shape[1]
    depth = w_in.shape[0]
    alpha = (2 * depth) ** 0.25
    pool_width = pool_w.shape[1] * pool_w.shape[2]
    na_width = na_rpb.shape[1] * HEAD_DIM
    hg_width = hg_norm_g.shape[1]
    na_heads, hg_heads = na_width // HEAD_DIM, hg_width // HEAD_DIM
    c_naq = pool_width // HEAD_DIM
    c_nak, c_nav = c_naq + na_heads, c_naq + 2 * na_heads
    c_hq = c_naq + 3 * na_heads
    c_hff, c_hfb, c_hi, c_hg = (c_hq + k * hg_heads for k in range(1, 5))
    assert B + 1 <= COND_ROWS

    h_parts = (x.reshape(B * L, D), ctx.reshape(B * Lc, D))
    cond =jnp.concatenate([c, c_ctx[None], jnp.zeros((COND_ROWS - B - 1, D), F32)], axis=0)
    mod = _adaln_call(cond, w_mod, b_mod)
    lb_soft = jax.nn.softmax(hg_lb.astype(F32), axis=1)
    lower = jnp.cumsum(lb_soft, axis=1) - lb_soft[:, :1]
    rw_pad = jnp.pad(router_w, ((0, 0), (0, 0), (0, HEAD_DIM - N_EXPERTS)))
    rb_pad = jnp.pad(router_b, ((0, 0), (0, HEAD_DIM - N_EXPERTS)))

    T = B * (L + Lc)
    a_in = _ln_mod_call(h_parts, mod[0], B, L)
    for l in range(depth):
        last = l == depth - 1
        px = _wstat_matmul_call([a_in], w_in, l, T, T // MM_ROW_TILES, MM_COL_TILE, "in_proj")
        a_mix = _pool_call(px, pool_w[l].astype(BF16), pool_scale[l], B, L, Lc)
        b_x, b_c = _na_call(px, na_rpb[l], B, L, Lc, c_naq, c_nak, c_nav, na_heads)
        b_mix = jnp.concatenate([b_x, b_c], axis=0)
        o_f, o_b = _hgrn_call(px, lower[0, l].reshape(hg_heads, 1, HEAD_DIM),
                              lower[1, l].reshape(hg_heads, 1, HEAD_DIM), B, L, Lc,
                              c_hq, c_hff, c_hfb, c_hi, hg_heads)
        c_mix = _readout_call(o_f, o_b, px, hg_norm_g[l], c_hg, T // MM_ROW_TILES)
        n_out = B * L if last else T
        y = _wstat_matmul_call([a_mix, b_mix, c_mix], w_out, l, n_out, n_out // MM_ROW_TILES, MM_COL_TILE,
                               "out_proj")
        h1, fin, topw, topi = _post_mix_call(y, h_parts, mod[l], ln1_g[l], ln1_b[l], rw_pad[l],
                                              rb_pad[l].reshape(1, -1), B, L, alpha)
        row_tok, pos, visits = _moe_dispatch(topi[:, :TOP_K], MOE_TILE)
        xs = fin.at[row_tok].get(mode="promise_in_bounds")
        ys = _moe_call(xs, visits, exp_w1, exp_b1, exp_w2, exp_b2, l)
        y_assign = ys.at[pos.T.reshape(-1)].get(mode="promise_in_bounds")
        if last:
            hall = _final_call(h1, topw, y_assign, mod[l], ln2_g[l], ln2_b[l], B, L, alpha)
        else:
            hall, a_in = _final_call(h1, topw, y_assign, mod[l], ln2_g[l], ln2_b[l], B, L, alpha,
                                     mod_next=mod[l + 1])
        h_parts = (hall,)
    return hall.reshape(B, L, D)
```
